```python
import jax, jax.numpy as jnp
from jax import lax
import numpy as np

D_MODEL = 1024
BATCH = 8
SEQ = 2048
DEPTH = 4
DEC_BATCH = 128
DEC_SEQ = 8
PAST_LEN = 16384
PAGE_SIZE = 128

MIX_WIDTH = D_MODEL
LRU_WIDTH = MIX_WIDTH // 2
SGU_WIDTH = MIX_WIDTH - LRU_WIDTH
HEAD_DIM = 64
LRU_HEADS = LRU_WIDTH // HEAD_DIM
SGU_HEADS = SGU_WIDTH // HEAD_DIM
N_OUT_GROUPS = MIX_WIDTH // HEAD_DIM
CHUNK = 128
CONV_WIDTH = 4
LRU_C = 8.0
IN_COLS = 2 * LRU_WIDTH + 2 * SGU_WIDTH
N_EXPERTS = 64
TOP_K = 8
EXPERT_DIM = D_MODEL // 4
SHARED_DIM = D_MODEL // 4
ROUTED_SCALE = 2.5
EPS = 1e-6

kernel_name = 'hymba_style_rglru_sgu_moe_adaln_step'


def _rmsnorm(x, g):
    xf = x.astype(jnp.float32)
    y = xf * lax.rsqrt(jnp.mean(xf * xf, axis=-1, keepdims=True) + EPS)
    return (y * g.astype(jnp.float32)).astype(x.dtype)


def _group_rmsnorm(x, g, n_groups):
    shp = x.shape
    xf = x.astype(jnp.float32).reshape(shp[:-1] + (n_groups, shp[-1] // n_groups))
    y = xf * lax.rsqrt(jnp.mean(xf * xf, axis=-1, keepdims=True) + EPS)
    return (y.reshape(shp) * g.astype(jnp.float32)).astype(x.dtype)


def _modulations(c, ada_w, ada_b):
    mod = jax.nn.silu(c) @ ada_w + ada_b
    return jnp.split(mod[:, None, :], 6, axis=-1)


def _causal_conv(x, buf, w, b):
    T = x.shape[1]
    xp = jnp.concatenate([buf.astype(x.dtype), x], axis=1)
    y = b
    for k in range(CONV_WIDTH):
        y = y + w[k] * xp[:, k:k + T]
    return y, xp[:, -(CONV_WIDTH - 1):]


def _rg_lru(x, h0, wr, br, wi, bi, lam):
    B, T, _ = x.shape
    xh = x.reshape(B, T, LRU_HEADS, HEAD_DIM)
    r = jax.nn.sigmoid(jnp.einsum('bthi,hij->bthj', xh, wr).reshape(B, T, LRU_WIDTH) + br)
    i = jax.nn.sigmoid(jnp.einsum('bthi,hij->bthj', xh, wi).reshape(B, T, LRU_WIDTH) + bi)
    log_a = -LRU_C * r.astype(jnp.float32) * jax.nn.softplus(-lam.astype(jnp.float32))
    a = jnp.exp(log_a)
    u = jnp.sqrt(-jnp.expm1(2.0 * log_a)) * (i * x).astype(jnp.float32)
    u = u.at[:, 0].add(a[:, 0] * h0.astype(jnp.float32))

    def combine(left, right):
        a1, b1 = left
        a2, b2 = right
        return a1 * a2, a2 * b1 + b2

    _, h = lax.associative_scan(combine, (a, u), axis=1)
    h = h.astype(x.dtype)
    return h, h[:, -1]


def _spatial_gate(u, v, ws, bs):
    B, T, _ = v.shape
    L = min(T, CHUNK)
    n = T // L
    mask = jnp.tril(jnp.ones((L, L), dtype=bool))
    w = jnp.where(mask, ws[:, :L, :L], jnp.zeros((), ws.dtype))
    vc = v.reshape(B, n, L, SGU_HEADS, HEAD_DIM)
    s = jnp.einsum('gpq,bnqgd->bnpgd', w, vc) + bs[:, :L].T[None, None, :, :, None]
    return u * s.reshape(B, T, SGU_WIDTH)


def _mixer(xn, h0, conv_buf, l, p):
    proj = xn @ p['w_in'][l]
    xa, ga, pu, pv = jnp.split(proj, [LRU_WIDTH, 2 * LRU_WIDTH, 2 * LRU_WIDTH + SGU_WIDTH], axis=-1)
    xc, new_buf = _causal_conv(xa, conv_buf, p['conv_w'][l], p['conv_b'][l])
    h, h_last = _rg_lru(xc, h0, p['gate_r_w'][l], p['gate_r_b'][l], p['gate_i_w'][l], p['gate_i_b'][l], p['lru_lambda'][l])
    ya = h * jax.nn.gelu(ga)
    u = jax.nn.gelu(pu)
    v = _rmsnorm(jax.nn.gelu(pv), p['sgu_norm_g'][l])
    yb = _spatial_gate(u, v, p['sgu_w'][l], p['sgu_b'][l])
    y = _group_rmsnorm(jnp.concatenate([ya, yb], axis=-1), p['out_norm_g'][l], N_OUT_GROUPS)
    return y @ p['w_out'][l], h_last, new_buf, v


def _moe(xn, l, p):
    B, T, D = xn.shape
    x = xn.reshape(B * T, D)
    scores = jax.nn.sigmoid((x @ p['router_w'][l]).astype(jnp.float32))
    _, idx = lax.top_k(scores + p['router_bias'][l].astype(jnp.float32), TOP_K)
    sel = jnp.take_along_axis(scores, idx, axis=-1)
    wsel = sel / jnp.sum(sel, axis=-1, keepdims=True) * ROUTED_SCALE
    gates = jnp.sum(jax.nn.one_hot(idx, N_EXPERTS, dtype=jnp.float32) * wsel[..., None], axis=1).astype(x.dtype)
    hid = jax.nn.silu(jnp.einsum('nd,edf->nef', x, p['exp_w_gate'][l])) * jnp.einsum('nd,edf->nef', x, p['exp_w_up'][l])
    routed = jnp.einsum('nef,efd->nd', hid * gates[..., None], p['exp_w_down'][l])
    shared = (jax.nn.silu(x @ p['shared_w_gate'][l]) * (x @ p['shared_w_up'][l])) @ p['shared_w_down'][l]
    return (routed + shared).reshape(B, T, D)


def _trunk(x, c, h_state, conv_state, p):
    hs, bufs, vs = [], [], []
    for l in range(DEPTH):
        sh_m, sc_m, g_m, sh_f, sc_f, g_f = _modulations(c, p['ada_w'][l], p['ada_b'][l])
        xn = _rmsnorm(x, p['norm_mix_g'][l]) * (1.0 + sc_m) + sh_m
        y, h_last, buf, v = _mixer(xn, h_state[l], conv_state[l], l, p)
        x = x + g_m * y
        xn = _rmsnorm(x, p['norm_ffn_g'][l]) * (1.0 + sc_f) + sh_f
        x = x + g_f * _moe(xn, l, p)
        hs.append(h_last)
        bufs.append(buf)
        vs.append(v)
    return _rmsnorm(x, p['final_norm_g']), jnp.stack(hs), jnp.stack(bufs), vs


def setup_inputs(seed: int = 0) -> dict:
    key = jax.random.key(seed)
    ks = jax.random.split(key, 40)
    nrm = lambda k, shape, s: jax.random.normal(k, shape, jnp.float32) * s
    gain = lambda k, shape: 1.0 + 0.02 * jax.random.normal(k, shape, jnp.float32)
    a0 = jax.random.uniform(ks[0], (DEPTH, LRU_WIDTH), jnp.float32, minval=0.9, maxval=0.999)
    a_base = a0 ** (1.0 / LRU_C)
    return {
        'x_prompt': nrm(ks[1], (BATCH, SEQ, D_MODEL), 1.0),
        'x_sample': nrm(ks[2], (DEC_BATCH, DEC_SEQ, D_MODEL), 1.0),
        'c_prompt': nrm(ks[3], (BATCH, D_MODEL), 1.0),
        'c_sample': nrm(ks[4], (DEC_BATCH, D_MODEL), 1.0),
        'state_lru_h': nrm(ks[5], (DEPTH, DEC_BATCH, LRU_WIDTH), 1.0),
        'state_conv': nrm(ks[6], (DEPTH, DEC_BATCH, CONV_WIDTH - 1, LRU_WIDTH), 1.0),
        'ada_w': nrm(ks[7], (DEPTH, D_MODEL, 6 * D_MODEL), 0.3 * D_MODEL ** -0.5),
        'ada_b': nrm(ks[8], (DEPTH, 6 * D_MODEL), 0.02),
        'norm_mix_g': gain(ks[9], (DEPTH, D_MODEL)),
        'norm_ffn_g': gain(ks[10], (DEPTH, D_MODEL)),
        'w_in': nrm(ks[11], (DEPTH, D_MODEL, IN_COLS), D_MODEL ** -0.5),
        'conv_w': nrm(ks[12], (DEPTH, CONV_WIDTH, LRU_WIDTH), CONV_WIDTH ** -0.5),
        'conv_b': nrm(ks[13], (DEPTH, LRU_WIDTH), 0.02),
        'gate_r_w': nrm(ks[14], (DEPTH, LRU_HEADS, HEAD_DIM, HEAD_DIM), HEAD_DIM ** -0.5),
        'gate_r_b': nrm(ks[15], (DEPTH, LRU_WIDTH), 0.02),
        'gate_i_w': nrm(ks[16], (DEPTH, LRU_HEADS, HEAD_DIM, HEAD_DIM), HEAD_DIM ** -0.5),
        'gate_i_b': nrm(ks[17], (DEPTH, LRU_WIDTH), 0.02),
        'lru_lambda': jnp.log(a_base) - jnp.log1p(-a_base),
        'sgu_norm_g': gain(ks[18], (DEPTH, SGU_WIDTH)),
        'sgu_w': nrm(ks[19], (DEPTH, SGU_HEADS, CHUNK, CHUNK), 0.5 * CHUNK ** -0.5),
        'sgu_b': gain(ks[20], (DEPTH, SGU_HEADS, CHUNK)),
        'out_norm_g': gain(ks[21], (DEPTH, MIX_WIDTH)),
        'w_out': nrm(ks[22], (DEPTH, MIX_WIDTH, D_MODEL), MIX_WIDTH ** -0.5),
        'router_w': nrm(ks[23], (DEPTH, D_MODEL, N_EXPERTS), D_MODEL ** -0.5),
        'router_bias': nrm(ks[24], (DEPTH, N_EXPERTS), 0.01),
        'exp_w_gate': nrm(ks[25], (DEPTH, N_EXPERTS, D_MODEL, EXPERT_DIM), D_MODEL ** -0.5),
        'exp_w_up': nrm(ks[26], (DEPTH, N_EXPERTS, D_MODEL, EXPERT_DIM), D_MODEL ** -0.5),
        'exp_w_down': nrm(ks[27], (DEPTH, N_EXPERTS, EXPERT_DIM, D_MODEL), EXPERT_DIM ** -0.5),
        'shared_w_gate': nrm(ks[28], (DEPTH, D_MODEL, SHARED_DIM), D_MODEL ** -0.5),
        'shared_w_up': nrm(ks[29], (DEPTH, D_MODEL, SHARED_DIM), D_MODEL ** -0.5),
        'shared_w_down': nrm(ks[30], (DEPTH, SHARED_DIM, D_MODEL), SHARED_DIM ** -0.5),
        'final_norm_g': gain(ks[31], (D_MODEL,)),
    }


def reference(x_prompt, x_sample, c_prompt, c_sample, state_lru_h, state_conv, ada_w, ada_b, norm_mix_g, norm_ffn_g, w_in, conv_w, conv_b, gate_r_w, gate_r_b, gate_i_w, gate_i_b, lru_lambda, sgu_norm_g, sgu_w, sgu_b, out_norm_g, w_out, router_w, router_bias, exp_w_gate, exp_w_up, exp_w_down, shared_w_gate, shared_w_up, shared_w_down, final_norm_g):
    p = dict(ada_w=ada_w, ada_b=ada_b, norm_mix_g=norm_mix_g, norm_ffn_g=norm_ffn_g, w_in=w_in, conv_w=conv_w, conv_b=conv_b,
             gate_r_w=gate_r_w, gate_r_b=gate_r_b, gate_i_w=gate_i_w, gate_i_b=gate_i_b, lru_lambda=lru_lambda,
             sgu_norm_g=sgu_norm_g, sgu_w=sgu_w, sgu_b=sgu_b, out_norm_g=out_norm_g, w_out=w_out,
             router_w=router_w, router_bias=router_bias, exp_w_gate=exp_w_gate, exp_w_up=exp_w_up, exp_w_down=exp_w_down,
             shared_w_gate=shared_w_gate, shared_w_up=shared_w_up, shared_w_down=shared_w_down, final_norm_g=final_norm_g)
    nb = x_prompt.shape[0]
    h0_prompt = jnp.zeros((DEPTH, nb, LRU_WIDTH), x_prompt.dtype)
    conv0_prompt = jnp.zeros((DEPTH, nb, CONV_WIDTH - 1, LRU_WIDTH), x_prompt.dtype)
    y_prompt, new_lru_h_prompt, new_conv_prompt, _ = _trunk(x_prompt, c_prompt, h0_prompt, conv0_prompt, p)
    y_sample, new_lru_h_sample, new_conv_sample, v_sample = _trunk(x_sample, c_sample, state_lru_h, state_conv, p)
    new_sgu_v_sample = jnp.stack(v_sample)
    return (y_prompt, y_sample, new_lru_h_prompt, new_conv_prompt, new_lru_h_sample, new_conv_sample, new_sgu_v_sample)
```

```python
import functools

import jax
import jax.numpy as jnp
from jax import lax
from jax.experimental import pallas as pl
from jax.experimental.pallas import tpu as pltpu

F32 = jnp.float32
BF16 = jnp.bfloat16
I32 = jnp.int32

D = 1024
DEPTH = 4
PB = 8
SEQ = 2048
SB = 128
DEC_SEQ = 8
NP = PB * SEQ
NS = SB * DEC_SEQ
NT = NP + NS
TILE = 1024
N_TILES = NT // TILE
P_TILES = NP // TILE
LW = 512
SW = 512
HEAD = 64
CHUNK = 128
CONV_W = 4
LRU_C = 8.0
NE = 64
TOPK = 8
ED = 256
ROUTED_SCALE = 2.5
EPS = 1e-6
NSLOT = NT * TOPK
ETILE = 256
N_ETILES = NSLOT // ETILE
N_VISITS = N_ETILES + NE - 1
CTILE = 512
N_CTILES = NT // CTILE
PC_TILES = NP // CTILE
MOD_COLS = 6 * D
MOD_BLK = 1536
VMEM_LIMIT = 56 * 1024 * 1024


def _dot(a, b):
    return jnp.dot(a, b, preferred_element_type=F32)


def _const_spec(shape):
    nd = len(shape)
    return pl.BlockSpec(shape, lambda *_: (0,) * nd)


def _mod_kernel(c_ref, w_ref, b_ref, o_ref):
    s = jax.nn.silu(c_ref[...]).astype(BF16)
    o_ref[0] = _dot(s, w_ref[0].astype(BF16)) + b_ref[0]


def _modulations(c_all, ada_w, ada_b):
    nb = c_all.shape[0]
    return pl.pallas_call(
        _mod_kernel,
        grid=(DEPTH, MOD_COLS // MOD_BLK),
        in_specs=[
            pl.BlockSpec((nb, D), lambda l, j: (0, 0)),
            pl.BlockSpec((1, D, MOD_BLK), lambda l, j: (l, 0, j)),
            pl.BlockSpec((1, 1, MOD_BLK), lambda l, j: (l, 0, j)),
        ],
        out_specs=pl.BlockSpec((1, nb, MOD_BLK), lambda l, j: (l, 0, j)),
        out_shape=jax.ShapeDtypeStruct((DEPTH, nb, MOD_COLS), F32),
        compiler_params=pltpu.CompilerParams(
            dimension_semantics=("arbitrary", "arbitrary"), vmem_limit_bytes=VMEM_LIMIT),
        name="adaln_mod",
    )(c_all, ada_w, ada_b.reshape(DEPTH, 1, MOD_COLS))


def _sgu_prompt(v, wpair_ref, bmap_ref, vbuf, sbuf):
    for j in range(4):
        vbuf[j] = v[:, 128 * j:128 * (j + 1)]
    lane = lax.broadcasted_iota(I32, (CHUNK, 128), 1)
    left = lane < HEAD
    for b in range(PB):
        for j in range(4):
            vp = vbuf[j, pl.ds(b, CHUNK, stride=PB), :]
            rhs = jnp.concatenate(
                [jnp.where(left, vp, 0.0), jnp.where(left, 0.0, vp)], axis=0).astype(BF16)
            sj = _dot(wpair_ref[j], rhs) + bmap_ref[:, 128 * j:128 * (j + 1)]
            sbuf[j, pl.ds(b, CHUNK, stride=PB), :] = sj
    return jnp.concatenate([sbuf[j] for j in range(4)], axis=1)


def _sgu_sample(v, coef_ref, bias_ref):
    vt = [v[t * SB:(t + 1) * SB] for t in range(DEC_SEQ)]
    rows = []
    for p in range(DEC_SEQ):
        acc = bias_ref[p:p + 1, :] + coef_ref[0, p:p + 1, :] * vt[0]
        for q in range(1, p + 1):
            acc = acc + coef_ref[q, p:p + 1, :] * vt[q]
        rows.append(acc)
    return jnp.concatenate(rows, axis=0)


def _mixer_tile(G, x, sh, sc, h_in, tail_in, sgu_fn, r, hbuf):
    nst = TILE // G
    x3 = x.reshape(nst, G, D)
    ms = jnp.mean(x3 * x3, axis=-1, keepdims=True)
    xn = x3 * lax.rsqrt(ms + EPS) * r["norm_g"][...]
    xn = (xn * (1.0 + sc[None]) + sh[None]).reshape(TILE, D).astype(BF16)

    xa = _dot(xn, r["w_in"][:, 0:LW])
    xp = jnp.concatenate([tail_in, xa], axis=0)
    cw = r["conv_w"]
    xc = r["conv_b"][...] + cw[0:1, :] * xp[0:TILE]
    for k in range(1, CONV_W):
        xc = xc + cw[k:k + 1, :] * xp[k * G:k * G + TILE]
    new_tail = xa[TILE - (CONV_W - 1) * G:]
    xcb = xc.astype(BF16)
    half = LW // 2
    r_pre = jnp.concatenate(
        [_dot(xcb[:, :half], r["wr"][0]), _dot(xcb[:, half:], r["wr"][1])], axis=1) + r["br"][...]
    i_pre = jnp.concatenate(
        [_dot(xcb[:, :half], r["wi"][0]), _dot(xcb[:, half:], r["wi"][1])], axis=1) + r["bi"][...]
    rg = jax.nn.sigmoid(r_pre)
    ig = jax.nn.sigmoid(i_pre)
    log_a = -LRU_C * rg * jax.nn.softplus(-r["lam"][...])
    a = jnp.exp(log_a)
    th = jnp.tanh(log_a)
    u = jnp.sqrt(-2.0 * th / (1.0 - th)) * (ig * xc)
    h = h_in
    for s in range(nst):
        h = a[s * G:(s + 1) * G] * h + u[s * G:(s + 1) * G]
        hbuf[s * G:(s + 1) * G, :] = h
    ya = hbuf[...] * jax.nn.gelu(_dot(xn, r["w_in"][:, LW:2 * LW]))

    ug = jax.nn.gelu(_dot(xn, r["w_in"][:, 2 * LW:2 * LW + SW]))
    vg = jax.nn.gelu(_dot(xn, r["w_in"][:, 2 * LW + SW:]))
    v = vg * lax.rsqrt(jnp.mean(vg * vg, axis=-1, keepdims=True) + EPS) * r["sgu_g"][...]
    yb = ug * sgu_fn(v)

    y = jnp.concatenate([ya, yb], axis=1)
    bd = r["bd"][...]
    outs = []
    for j in range(D // 256):
        ysl = y[:, 256 * j:256 * (j + 1)]
        sq = ysl * ysl
        hi = sq.astype(BF16)
        lo = (sq - hi.astype(F32)).astype(BF16)
        msq = (_dot(hi, bd) + _dot(lo, bd)) * (1.0 / HEAD)
        outs.append(ysl * lax.rsqrt(msq + EPS) * r["out_g"][:, 256 * j:256 * (j + 1)])
    yn = jnp.concatenate(outs, axis=1).astype(BF16)
    return yn, h, new_tail, v


_MIX_IN = ("x", "mod", "norm_g", "w_in", "conv_w", "conv_b", "wr", "wi", "br", "bi", "lam",
           "sgu_g", "wpair", "bmap", "coef", "bias8", "out_g", "bd", "h0s", "convs")
_MIX_OUT = ("y", "hlast_p", "conv_p", "hlast_s", "conv_s", "v_s")
_MIX_SCR = ("hbuf", "vbuf", "sbuf", "h_carry", "tail_carry")


def _mixer_kernel(*refs):
    names = _MIX_IN + _MIX_OUT + _MIX_SCR
    r = dict(zip(names, refs))
    i = pl.program_id(0)

    @pl.when(i == 0)
    def _():
        r["h_carry"][...] = jnp.zeros((PB, LW), F32)
        r["tail_carry"][...] = jnp.zeros(((CONV_W - 1) * PB, LW), F32)

    @pl.when(i < P_TILES)
    def _():
        sh = r["mod"][0:PB, 0:D]
        sc = r["mod"][0:PB, D:2 * D]
        sgu = functools.partial(_sgu_prompt, wpair_ref=r["wpair"], bmap_ref=r["bmap"],
                                vbuf=r["vbuf"], sbuf=r["sbuf"])
        yn, h, tail, _ = _mixer_tile(PB, r["x"][...], sh, sc, r["h_carry"][...],
                                     r["tail_carry"][...], sgu, r, r["hbuf"])
        r["y"][...] = yn
        r["h_carry"][...] = h
        r["tail_carry"][...] = tail
        r["hlast_p"][...] = h
        r["conv_p"][...] = tail

    @pl.when(i == P_TILES)
    def _():
        sh = r["mod"][PB:PB + SB, 0:D]
        sc = r["mod"][PB:PB + SB, D:2 * D]
        sgu = functools.partial(_sgu_sample, coef_ref=r["coef"], bias_ref=r["bias8"])
        yn, h, tail, v = _mixer_tile(SB, r["x"][...], sh, sc, r["h0s"][...],
                                     r["convs"][...], sgu, r, r["hbuf"])
        r["y"][...] = yn
        r["hlast_s"][...] = h
        r["conv_s"][...] = tail
        r["v_s"][...] = v


def _mixer(x_all, mod_l, w):
    nb = PB + SB
    in_specs = [
        pl.BlockSpec((TILE, D), lambda i: (i, 0)),
        _const_spec((nb, MOD_COLS)),
        _const_spec((1, D)),
        _const_spec((D, 2 * LW + 2 * SW)),
        _const_spec((CONV_W, LW)),
        _const_spec((1, LW)),
        _const_spec((2, LW // 2, LW // 2)),
        _const_spec((2, LW // 2, LW // 2)),
        _const_spec((1, LW)),
        _const_spec((1, LW)),
        _const_spec((1, LW)),
        _const_spec((1, SW)),
        _const_spec((4, CHUNK, 2 * CHUNK)),
        _const_spec((CHUNK, SW)),
        _const_spec((DEC_SEQ, DEC_SEQ, SW)),
        _const_spec((DEC_SEQ, SW)),
        _const_spec((1, D)),
        _const_spec((256, 256)),
        _const_spec((SB, LW)),
        _const_spec(((CONV_W - 1) * SB, LW)),
    ]
    out_shape = (
        jax.ShapeDtypeStruct((NT, D), BF16),
        jax.ShapeDtypeStruct((PB, LW), F32),
        jax.ShapeDtypeStruct(((CONV_W - 1) * PB, LW), F32),
        jax.ShapeDtypeStruct((SB, LW), F32),
        jax.ShapeDtypeStruct(((CONV_W - 1) * SB, LW), F32),
        jax.ShapeDtypeStruct((NS, SW), F32),
    )
    out_specs = (
        pl.BlockSpec((TILE, D), lambda i: (i, 0)),
        _const_spec((PB, LW)),
        _const_spec(((CONV_W - 1) * PB, LW)),
        _const_spec((SB, LW)),
        _const_spec(((CONV_W - 1) * SB, LW)),
        _const_spec((NS, SW)),
    )
    scratch = [
        pltpu.VMEM((TILE, LW), F32),
        pltpu.VMEM((4, TILE, 128), F32),
        pltpu.VMEM((4, TILE, 128), F32),
        pltpu.VMEM((PB, LW), F32),
        pltpu.VMEM(((CONV_W - 1) * PB, LW), F32),
    ]
    return pl.pallas_call(
        _mixer_kernel,
        grid=(N_TILES,),
        in_specs=in_specs,
        out_specs=out_specs,
        out_shape=out_shape,
        scratch_shapes=scratch,
        compiler_params=pltpu.CompilerParams(
            dimension_semantics=("arbitrary",), vmem_limit_bytes=VMEM_LIMIT),
        name="mixer",
    )(x_all, mod_l, w["norm_mix_g"], w["w_in"], w["conv_w"], w["conv_b"], w["wr"], w["wi"],
      w["br"], w["bi"], w["lam"], w["sgu_g"], w["wpair"], w["bmap"], w["coef"], w["bias8"],
      w["out_g"], w["bd"], w["h0s"], w["convs"])


def _router_tile(G, r, gm, sc, sh):
    nst = TILE // G
    out = _dot(r["y"][...], r["w_out"][...])
    xm3 = r["x"][...].reshape(nst, G, D) + gm[None] * out.reshape(nst, G, D)
    r["x_mid"][...] = xm3.reshape(TILE, D)
    ms = jnp.mean(xm3 * xm3, axis=-1, keepdims=True)
    xn3 = xm3 * lax.rsqrt(ms + EPS) * r["norm_g"][...]
    xn = (xn3 * (1.0 + sc[None]) + sh[None]).reshape(TILE, D)
    r["xn"][...] = xn

    logits = lax.dot_general(r["rw_t"][...], xn.astype(BF16), (((1,), (1,)), ((), ())),
                             preferred_element_type=F32)
    sig = jax.nn.sigmoid(logits)
    sel = sig + r["rbias"][...]
    eiota = lax.broadcasted_iota(I32, (NE, TILE), 0)
    chosen = jnp.zeros((NE, TILE), F32)
    e_rows, w_rows, hots = [], [], []
    for _ in range(TOPK):
        m = jnp.max(sel, axis=0, keepdims=True)
        idx = jnp.min(jnp.where(sel == m, eiota, NE), axis=0, keepdims=True)
        hot = eiota == idx
        hots.append(hot)
        e_rows.append(idx)
        w_rows.append(jnp.sum(jnp.where(hot, sig, 0.0), axis=0, keepdims=True))
        chosen = jnp.where(hot, 1.0, chosen)
        sel = jnp.where(hot, -jnp.inf, sel)
    den = w_rows[0]
    for k in range(1, TOPK):
        den = den + w_rows[k]
    scale = ROUTED_SCALE / den
    r["w_t"][0] = jnp.concatenate([wk * scale for wk in w_rows], axis=0)
    r["e_t"][0] = jnp.concatenate(e_rows, axis=0)

    cnt = r["cnt"][...]
    rank = _dot(chosen.astype(BF16), r["tri"][...]) + cnt[:, 0:1]
    rk_rows = [jnp.sum(jnp.where(hot, rank, 0.0), axis=0, keepdims=True) for hot in hots]
    r["rank_t"][0] = jnp.concatenate(rk_rows, axis=0).astype(I32)
    cnt = cnt + jnp.sum(chosen, axis=1, keepdims=True)
    r["cnt"][...] = cnt
    r["counts"][...] = cnt


_RT_IN = ("x", "y", "mod", "w_out", "norm_g", "rw_t", "rbias", "tri")
_RT_OUT = ("x_mid", "xn", "e_t", "rank_t", "w_t", "counts")
_RT_SCR = ("cnt",)


def _router_kernel(*refs):
    r = dict(zip(_RT_IN + _RT_OUT + _RT_SCR, refs))
    i = pl.program_id(0)

    @pl.when(i == 0)
    def _():
        r["cnt"][...] = jnp.zeros((NE, 128), F32)

    def mods(lo, n):
        m = r["mod"]
        return m[lo:lo + n, 2 * D:3 * D], m[lo:lo + n, 4 * D:5 * D], m[lo:lo + n, 3 * D:4 * D]

    @pl.when(i < P_TILES)
    def _():
        _router_tile(PB, r, *mods(0, PB))

    @pl.when(i == P_TILES)
    def _():
        _router_tile(SB, r, *mods(PB, SB))


def _router(x_all, y_all, mod_l, w):
    nb = PB + SB
    in_specs = [
        pl.BlockSpec((TILE, D), lambda i: (i, 0)),
        pl.BlockSpec((TILE, D), lambda i: (i, 0)),
        _const_spec((nb, MOD_COLS)),
        _const_spec((D, D)),
        _const_spec((1, D)),
        _const_spec((NE, D)),
        _const_spec((NE, 1)),
        _const_spec((TILE, TILE)),
    ]
    out_shape = (
        jax.ShapeDtypeStruct((NT, D), F32),
        jax.ShapeDtypeStruct((NT, D), F32),
        jax.ShapeDtypeStruct((N_TILES, TOPK, TILE), I32),
        jax.ShapeDtypeStruct((N_TILES, TOPK, TILE), I32),
        jax.ShapeDtypeStruct((N_TILES, TOPK, TILE), F32),
        jax.ShapeDtypeStruct((NE, 128), F32),
    )
    out_specs = (
        pl.BlockSpec((TILE, D), lambda i: (i, 0)),
        pl.BlockSpec((TILE, D), lambda i: (i, 0)),
        pl.BlockSpec((1, TOPK, TILE), lambda i: (i, 0, 0)),
        pl.BlockSpec((1, TOPK, TILE), lambda i: (i, 0, 0)),
        pl.BlockSpec((1, TOPK, TILE), lambda i: (i, 0, 0)),
        _const_spec((NE, 128)),
    )
    return pl.pallas_call(
        _router_kernel,
        grid=(N_TILES,),
        in_specs=in_specs,
        out_specs=out_specs,
        out_shape=out_shape,
        scratch_shapes=[pltpu.VMEM((NE, 128), F32)],
        compiler_params=pltpu.CompilerParams(
            dimension_semantics=("arbitrary",), vmem_limit_bytes=VMEM_LIMIT),
        name="router",
    )(x_all, y_all, mod_l, w["w_out"], w["norm_ffn_g"], w["rw_t"], w["rbias"], w["tri"])


def _dispatch_kernel(dest_ref, xn_ref, xs_ref, sem):
    def row_copy(t, k):
        return pltpu.make_async_copy(
            xn_ref.at[pl.ds(t, 1)], xs_ref.at[pl.ds(dest_ref[0, k, t], 1)], sem)

    def body(t, carry):
        for k in range(TOPK):
            row_copy(t, k).start()
        return carry

    lax.fori_loop(0, TILE, body, 0)
    for _ in range(TOPK):
        pltpu.make_async_copy(xn_ref, xs_ref.at[pl.ds(0, TILE)], sem).wait()


def _dispatch(dest, xn):
    return pl.pallas_call(
        _dispatch_kernel,
        grid=(N_TILES,),
        in_specs=[
            pl.BlockSpec((1, TOPK, TILE), lambda i: (i, 0, 0), memory_space=pltpu.SMEM),
            pl.BlockSpec((TILE, D), lambda i: (i, 0)),
        ],
        out_specs=pl.BlockSpec(memory_space=pl.ANY),
        out_shape=jax.ShapeDtypeStruct((NSLOT, D), F32),
        scratch_shapes=[pltpu.SemaphoreType.DMA(())],
        compiler_params=pltpu.CompilerParams(
            dimension_semantics=("arbitrary",), vmem_limit_bytes=VMEM_LIMIT),
        name="dispatch",
    )(dest, xn)


def _expert_kernel(tile_ref, exp_ref, lo_ref, hi_ref, flag_ref, xs_ref, wg_ref, wu_ref, wd_ref,
                   ys_ref):
    v = pl.program_id(0)
    flag = flag_ref[v]

    @pl.when(flag > 0)
    def _():
        x = xs_ref[...].astype(BF16)
        hid = jax.nn.silu(_dot(x, wg_ref[0])) * _dot(x, wu_ref[0])
        y = _dot(hid.astype(BF16), wd_ref[0])
        row = lax.broadcasted_iota(I32, (ETILE, 1), 0)
        mine = (row >= lo_ref[v]) & (row < hi_ref[v])

        @pl.when(flag == 2)
        def _():
            ys_ref[...] = jnp.where(mine, y, 0.0)

        @pl.when(flag == 1)
        def _():
            ys_ref[...] = jnp.where(mine, y, ys_ref[...])


def _experts(meta, xs, wg, wu, wd):
    grid_spec = pltpu.PrefetchScalarGridSpec(
        num_scalar_prefetch=5,
        grid=(N_VISITS,),
        in_specs=[
            pl.BlockSpec((ETILE, D), lambda v, t, e, lo, hi, f: (t[v], 0)),
            pl.BlockSpec((1, D, ED), lambda v, t, e, lo, hi, f: (e[v], 0, 0)),
            pl.BlockSpec((1, D, ED), lambda v, t, e, lo, hi, f: (e[v], 0, 0)),
            pl.BlockSpec((1, ED, D), lambda v, t, e, lo, hi, f: (e[v], 0, 0)),
        ],
        out_specs=pl.BlockSpec((ETILE, D), lambda v, t, e, lo, hi, f: (t[v], 0)),
    )
    return pl.pallas_call(
        _expert_kernel,
        grid_spec=grid_spec,
        out_shape=jax.ShapeDtypeStruct((NSLOT, D), F32),
        compiler_params=pltpu.CompilerParams(
            dimension_semantics=("arbitrary",), vmem_limit_bytes=VMEM_LIMIT),
        name="experts",
    )(*meta, xs, wg, wu, wd)


def _combine_tile(G, r, gf, final):
    nst = CTILE // G
    xb = r["xn"][...].astype(BF16)
    hid = jax.nn.silu(_dot(xb, r["sg"][...])) * _dot(xb, r["su"][...])
    moe = _dot(hid.astype(BF16), r["sd"][...])
    for _ in range(TOPK):
        pltpu.make_async_copy(r["ys"].at[pl.ds(0, CTILE)], r["buf"].at[0], r["sem"]).wait()
    wts = r["wts"][...]
    for k in range(TOPK):
        moe = moe + wts[:, k:k + 1] * r["buf"][k]
    xo = r["x_mid"][...].reshape(nst, G, D) + gf[None] * moe.reshape(nst, G, D)
    r["x_out"][...] = xo.reshape(CTILE, D)
    if final:
        ms = jnp.mean(xo * xo, axis=-1, keepdims=True)
        r["y_out"][...] = (xo * lax.rsqrt(ms + EPS) * r["fin_g"][...]).reshape(CTILE, D)


def _combine_kernel(final, *refs):
    names = ["dest", "ys", "x_mid", "xn", "wts", "mod", "sg", "su", "sd"]
    names += ["fin_g"] if final else []
    names += ["x_out"] + (["y_out"] if final else []) + ["buf", "sem"]
    r = dict(zip(names, refs))
    i = pl.program_id(0)

    def body(t, carry):
        for k in range(TOPK):
            pltpu.make_async_copy(
                r["ys"].at[pl.ds(r["dest"][0, k, t], 1)], r["buf"].at[k, pl.ds(t, 1)],
                r["sem"]).start()
        return carry

    lax.fori_loop(0, CTILE, body, 0)

    @pl.when(i < PC_TILES)
    def _():
        _combine_tile(PB, r, r["mod"][0:PB, 5 * D:6 * D], final)

    @pl.when(i >= PC_TILES)
    def _():
        _combine_tile(SB, r, r["mod"][PB:PB + SB, 5 * D:6 * D], final)


def _combine(dest_c, ys, x_mid, xn, wts, mod_l, w, fin_g):
    final = fin_g is not None
    nb = PB + SB
    in_specs = [
        pl.BlockSpec((1, TOPK, CTILE), lambda i: (i, 0, 0), memory_space=pltpu.SMEM),
        pl.BlockSpec(memory_space=pl.ANY),
        pl.BlockSpec((CTILE, D), lambda i: (i, 0)),
        pl.BlockSpec((CTILE, D), lambda i: (i, 0)),
        pl.BlockSpec((CTILE, TOPK), lambda i: (i, 0)),
        _const_spec((nb, MOD_COLS)),
        _const_spec((D, ED)),
        _const_spec((D, ED)),
        _const_spec((ED, D)),
    ]
    args = [dest_c, ys, x_mid, xn, wts, mod_l, w["sg"], w["su"], w["sd"]]
    out_shape = [jax.ShapeDtypeStruct((NT, D), F32)]
    out_specs = [pl.BlockSpec((CTILE, D), lambda i: (i, 0))]
    if final:
        in_specs.append(_const_spec((1, D)))
        args.append(fin_g)
        out_shape.append(jax.ShapeDtypeStruct((NT, D), F32))
        out_specs.append(pl.BlockSpec((CTILE, D), lambda i: (i, 0)))
    res = pl.pallas_call(
        functools.partial(_combine_kernel, final),
        grid=(N_CTILES,),
        in_specs=in_specs,
        out_specs=tuple(out_specs),
        out_shape=tuple(out_shape),
        scratch_shapes=[pltpu.VMEM((TOPK, CTILE, D), F32), pltpu.SemaphoreType.DMA(())],
        compiler_params=pltpu.CompilerParams(
            dimension_semantics=("arbitrary",), vmem_limit_bytes=VMEM_LIMIT),
        name="combine_final" if final else "combine",
    )(*args)
    return res if final else (res[0], None)


def _routing_meta(counts, e_t, rank_t):
    cnt = counts[:, 0].astype(I32)
    end = jnp.cumsum(cnt)
    off = end - cnt
    dest = jnp.take(off, e_t) + rank_t
    first_tile = off // ETILE
    last_tile = (end - 1) // ETILE
    nvis = jnp.where(cnt > 0, last_tile - first_tile + 1, 0)
    vend = jnp.cumsum(nvis)
    vstart = vend - nvis
    total = vend[-1]
    v = jnp.arange(N_VISITS, dtype=I32)
    valid = v < total
    vv = jnp.minimum(v, total - 1)
    e = jnp.minimum(jnp.searchsorted(vend, vv, side="right").astype(I32), NE - 1)
    tile = jnp.take(first_tile, e) + (vv - jnp.take(vstart, e))
    base = tile * ETILE
    lo = jnp.clip(jnp.take(off, e) - base, 0, ETILE)
    hi = jnp.clip(jnp.take(end, e) - base, 0, ETILE)
    first = jnp.take(off, e) <= base
    flag = jnp.where(valid, jnp.where(first, 2, 1), 0).astype(I32)
    return dest, (tile.astype(I32), e, lo.astype(I32), hi.astype(I32), flag)


def _block_diag(wh):
    out = jnp.zeros((2, 4, HEAD, 4, HEAD), wh.dtype)
    for hh in range(4):
        out = out.at[:, hh, :, hh, :].set(wh.reshape(2, 4, HEAD, HEAD)[:, hh])
    return out.reshape(2, 4 * HEAD, 4 * HEAD)


def _layer_weights(l, p):
    tril = jnp.tril(jnp.ones((CHUNK, CHUNK), bool))
    wt = jnp.where(tril, p["sgu_w"][l], 0.0)
    wpair = wt.reshape(4, 2, CHUNK, CHUNK).transpose(0, 2, 1, 3).reshape(4, CHUNK, 2 * CHUNK)
    bmap = jnp.repeat(p["sgu_b"][l].T, HEAD, axis=1)
    w8 = jnp.where(tril[:DEC_SEQ, :DEC_SEQ], p["sgu_w"][l][:, :DEC_SEQ, :DEC_SEQ], 0.0)
    coef = jnp.repeat(w8.transpose(2, 1, 0), HEAD, axis=2)
    bias8 = jnp.repeat(p["sgu_b"][l][:, :DEC_SEQ].T, HEAD, axis=1)
    gid = jnp.arange(256) // HEAD
    return dict(
        norm_mix_g=p["norm_mix_g"][l][None], norm_ffn_g=p["norm_ffn_g"][l][None],
        w_in=p["w_in"][l].astype(BF16), w_out=p["w_out"][l].astype(BF16),
        conv_w=p["conv_w"][l], conv_b=p["conv_b"][l][None],
        wr=_block_diag(p["gate_r_w"][l]).astype(BF16), wi=_block_diag(p["gate_i_w"][l]).astype(BF16),
        br=p["gate_r_b"][l][None], bi=p["gate_i_b"][l][None], lam=p["lru_lambda"][l][None],
        sgu_g=p["sgu_norm_g"][l][None], wpair=wpair.astype(BF16), bmap=bmap, coef=coef, bias8=bias8,
        out_g=p["out_norm_g"][l][None],
        bd=(gid[:, None] == gid[None, :]).astype(BF16),
        rw_t=p["router_w"][l].T.astype(BF16), rbias=p["router_bias"][l][:, None],
        tri=jnp.triu(jnp.ones((TILE, TILE), BF16), k=1),
        wg=p["exp_w_gate"][l].astype(BF16), wu=p["exp_w_up"][l].astype(BF16),
        wd=p["exp_w_down"][l].astype(BF16),
        sg=p["shared_w_gate"][l].astype(BF16), su=p["shared_w_up"][l].astype(BF16),
        sd=p["shared_w_down"][l].astype(BF16),
    )


def kernel(x_prompt, x_sample, c_prompt, c_sample, state_lru_h, state_conv, ada_w, ada_b, norm_mix_g, norm_ffn_g, w_in, conv_w, conv_b, gate_r_w, gate_r_b, gate_i_w, gate_i_b, lru_lambda, sgu_norm_g, sgu_w, sgu_b, out_norm_g, w_out, router_w, router_bias, exp_w_gate, exp_w_up, exp_w_down, shared_w_gate, shared_w_up, shared_w_down, final_norm_g):
    p = dict(norm_mix_g=norm_mix_g, norm_ffn_g=norm_ffn_g, w_in=w_in, conv_w=conv_w, conv_b=conv_b,
             gate_r_w=gate_r_w, gate_r_b=gate_r_b, gate_i_w=gate_i_w, gate_i_b=gate_i_b,
             lru_lambda=lru_lambda, sgu_norm_g=sgu_norm_g, sgu_w=sgu_w, sgu_b=sgu_b,
             out_norm_g=out_norm_g, w_out=w_out, router_w=router_w, router_bias=router_bias,
             exp_w_gate=exp_w_gate, exp_w_up=exp_w_up, exp_w_down=exp_w_down,
             shared_w_gate=shared_w_gate, shared_w_up=shared_w_up, shared_w_down=shared_w_down)
    x_all = jnp.concatenate([x_prompt.transpose(1, 0, 2).reshape(NP, D),
                             x_sample.transpose(1, 0, 2).reshape(NS, D)], axis=0)
    mod = _modulations(jnp.concatenate([c_prompt, c_sample], axis=0), ada_w, ada_b)

    hp, cp, hs, cs, vs = [], [], [], [], []
    y_all = None
    for l in range(DEPTH):
        w = _layer_weights(l, p)
        w["h0s"] = state_lru_h[l]
        w["convs"] = state_conv[l].transpose(1, 0, 2).reshape((CONV_W - 1) * SB, LW)
        yn, hlp, cvp, hls, cvs, v_s = _mixer(x_all, mod[l], w)
        x_mid, xn, e_t, rank_t, w_t, counts = _router(x_all, yn, mod[l], w)
        dest, meta = _routing_meta(counts, e_t, rank_t)
        xs = _dispatch(dest, xn)
        ys = _experts(meta, xs, w["wg"], w["wu"], w["wd"])
        half = TILE // CTILE
        dest_c = dest.reshape(N_TILES, TOPK, half, CTILE).transpose(0, 2, 1, 3).reshape(
            N_CTILES, TOPK, CTILE)
        wts = w_t.transpose(0, 2, 1).reshape(NT, TOPK)
        fin = final_norm_g[None] if l == DEPTH - 1 else None
        x_all, y_all = _combine(dest_c, ys, x_mid, xn, wts, mod[l], w, fin)
        hp.append(hlp)
        cp.append(cvp.reshape(CONV_W - 1, PB, LW).transpose(1, 0, 2))
        hs.append(hls)
        cs.append(cvs.reshape(CONV_W - 1, SB, LW).transpose(1, 0, 2))
        vs.append(v_s.reshape(DEC_SEQ, SB, SW).transpose(1, 0, 2))

    y_prompt = y_all[:NP].reshape(SEQ, PB, D).transpose(1, 0, 2)
    y_sample = y_all[NP:].reshape(DEC_SEQ, SB, D).transpose(1, 0, 2)
    return (y_prompt, y_sample, jnp.stack(hp), jnp.stack(cp), jnp.stack(hs), jnp.stack(cs),
            jnp.stack(vs))
```

```python
import functools

import jax
import jax.numpy as jnp
from jax import lax
from jax.experimental import pallas as pl
from jax.experimental.pallas import tpu as pltpu

F32 = jnp.float32
BF16 = jnp.bfloat16
I32 = jnp.int32

D = 1024
DEPTH = 4
PB = 8
SEQ = 2048
SB = 128
DEC_SEQ = 8
NP = PB * SEQ
NS = SB * DEC_SEQ
NT = NP + NS
TILE = 1024
N_TILES = NT // TILE
P_TILES = NP // TILE
LW = 512
SW = 512
HEAD = 64
CHUNK = 128
CONV_W = 4
LRU_C = 8.0
NE = 64
TOPK = 8
ED = 256
ROUTED_SCALE = 2.5
EPS = 1e-6
NSLOT = NT * TOPK
ETILE = 256
N_ETILES = NSLOT // ETILE
N_VISITS = N_ETILES + NE - 1
CTILE = 512
N_CTILES = NT // CTILE
PC_TILES = NP // CTILE
MOD_COLS = 6 * D
MOD_BLK = 1536
VMEM_LIMIT = 56 * 1024 * 1024


def _dot(a, b):
    return jnp.dot(a, b, preferred_element_type=F32)


def _const_spec(shape):
    nd = len(shape)
    return pl.BlockSpec(shape, lambda *_: (0,) * nd)


def _mod_kernel(c_ref, w_ref, b_ref, o_ref):
    s = jax.nn.silu(c_ref[...]).astype(BF16)
    o_ref[0] = _dot(s, w_ref[0].astype(BF16)) + b_ref[0]


def _modulations(c_all, ada_w, ada_b):
    nb = c_all.shape[0]
    return pl.pallas_call(
        _mod_kernel,
        grid=(DEPTH, MOD_COLS // MOD_BLK),
        in_specs=[
            pl.BlockSpec((nb, D), lambda l, j: (0, 0)),
            pl.BlockSpec((1, D, MOD_BLK), lambda l, j: (l, 0, j)),
            pl.BlockSpec((1, 1, MOD_BLK), lambda l, j: (l, 0, j)),
        ],
        out_specs=pl.BlockSpec((1, nb, MOD_BLK), lambda l, j: (l, 0, j)),
        out_shape=jax.ShapeDtypeStruct((DEPTH, nb, MOD_COLS), F32),
        compiler_params=pltpu.CompilerParams(
            dimension_semantics=("arbitrary", "arbitrary"), vmem_limit_bytes=VMEM_LIMIT),
        name="adaln_mod",
    )(c_all, ada_w, ada_b.reshape(DEPTH, 1, MOD_COLS))


def _sgu_prompt(v, wpair_ref, bmap_ref, vbuf, sbuf):
    for j in range(4):
        vbuf[j] = v[:, 128 * j:128 * (j + 1)]
    lane = lax.broadcasted_iota(I32, (CHUNK, 128), 1)
    left = lane < HEAD
    for b in range(PB):
        for j in range(4):
            vp = vbuf[j, pl.ds(b, CHUNK, stride=PB), :]
            rhs = jnp.concatenate(
                [jnp.where(left, vp, 0.0), jnp.where(left, 0.0, vp)], axis=0).astype(BF16)
            sj = _dot(wpair_ref[j], rhs) + bmap_ref[:, 128 * j:128 * (j + 1)]
            sbuf[j, pl.ds(b, CHUNK, stride=PB), :] = sj
    return jnp.concatenate([sbuf[j] for j in range(4)], axis=1)


def _sgu_sample(v, coef_ref, bias_ref):
    vt = [v[t * SB:(t + 1) * SB] for t in range(DEC_SEQ)]
    rows = []
    for p in range(DEC_SEQ):
        acc = bias_ref[p:p + 1, :] + coef_ref[0, p:p + 1, :] * vt[0]
        for q in range(1, p + 1):
            acc = acc + coef_ref[q, p:p + 1, :] * vt[q]
        rows.append(acc)
    return jnp.concatenate(rows, axis=0)


def _mixer_tile(G, x, sh, sc, h_in, tail_in, sgu_fn, r, hbuf):
    nst = TILE // G
    x3 = x.reshape(nst, G, D)
    ms = jnp.mean(x3 * x3, axis=-1, keepdims=True)
    xn = x3 * lax.rsqrt(ms + EPS) * r["norm_g"][...]
    xn = (xn * (1.0 + sc[None]) + sh[None]).reshape(TILE, D).astype(BF16)

    xa = _dot(xn, r["w_in"][:, 0:LW])
    xp = jnp.concatenate([tail_in, xa], axis=0)
    cw = r["conv_w"]
    xc = r["conv_b"][...] + cw[0:1, :] * xp[0:TILE]
    for k in range(1, CONV_W):
        xc = xc + cw[k:k + 1, :] * xp[k * G:k * G + TILE]
    new_tail = xa[TILE - (CONV_W - 1) * G:]
    xcb = xc.astype(BF16)
    half = LW // 2
    r_pre = jnp.concatenate(
        [_dot(xcb[:, :half], r["wr"][0]), _dot(xcb[:, half:], r["wr"][1])], axis=1) + r["br"][...]
    i_pre = jnp.concatenate(
        [_dot(xcb[:, :half], r["wi"][0]), _dot(xcb[:, half:], r["wi"][1])], axis=1) + r["bi"][...]
    rg = jax.nn.sigmoid(r_pre)
    ig = jax.nn.sigmoid(i_pre)
    log_a = -LRU_C * rg * jax.nn.softplus(-r["lam"][...])
    a = jnp.exp(log_a)
    th = jnp.tanh(log_a)
    u = jnp.sqrt(-2.0 * th / (1.0 - th)) * (ig * xc)
    h = h_in
    for s in range(nst):
        h = a[s * G:(s + 1) * G] * h + u[s * G:(s + 1) * G]
        hbuf[s * G:(s + 1) * G, :] = h
    ya = hbuf[...] * jax.nn.gelu(_dot(xn, r["w_in"][:, LW:2 * LW]))

    ug = jax.nn.gelu(_dot(xn, r["w_in"][:, 2 * LW:2 * LW + SW]))
    vg = jax.nn.gelu(_dot(xn, r["w_in"][:, 2 * LW + SW:]))
    v = vg * lax.rsqrt(jnp.mean(vg * vg, axis=-1, keepdims=True) + EPS) * r["sgu_g"][...]
    yb = ug * sgu_fn(v)

    y = jnp.concatenate([ya, yb], axis=1)
    bd = r["bd"][...]
    outs = []
    for j in range(D // 256):
        ysl = y[:, 256 * j:256 * (j + 1)]
        sq = ysl * ysl
        hi = sq.astype(BF16)
        lo = (sq - hi.astype(F32)).astype(BF16)
        msq = (_dot(hi, bd) + _dot(lo, bd)) * (1.0 / HEAD)
        outs.append(ysl * lax.rsqrt(msq + EPS) * r["out_g"][:, 256 * j:256 * (j + 1)])
    yn = jnp.concatenate(outs, axis=1).astype(BF16)
    return yn, h, new_tail, v


_MIX_IN = ("x", "mod", "norm_g", "w_in", "conv_w", "conv_b", "wr", "wi", "br", "bi", "lam",
           "sgu_g", "wpair", "bmap", "coef", "bias8", "out_g", "bd", "h0s", "convs")
_MIX_OUT = ("y", "hlast_p", "conv_p", "hlast_s", "conv_s", "v_s")
_MIX_SCR = ("hbuf", "vbuf", "sbuf", "h_carry", "tail_carry")


def _mixer_kernel(*refs):
    names = _MIX_IN + _MIX_OUT + _MIX_SCR
    r = dict(zip(names, refs))
    i = pl.program_id(0)

    @pl.when(i == 0)
    def _():
        r["h_carry"][...] = jnp.zeros((PB, LW), F32)
        r["tail_carry"][...] = jnp.zeros(((CONV_W - 1) * PB, LW), F32)

    @pl.when(i < P_TILES)
    def _():
        sh = r["mod"][0:PB, 0:D]
        sc = r["mod"][0:PB, D:2 * D]
        sgu = functools.partial(_sgu_prompt, wpair_ref=r["wpair"], bmap_ref=r["bmap"],
                                vbuf=r["vbuf"], sbuf=r["sbuf"])
        yn, h, tail, _ = _mixer_tile(PB, r["x"][...], sh, sc, r["h_carry"][...],
                                     r["tail_carry"][...], sgu, r, r["hbuf"])
        r["y"][...] = yn
        r["h_carry"][...] = h
        r["tail_carry"][...] = tail
        r["hlast_p"][...] = h
        r["conv_p"][...] = tail

    @pl.when(i == P_TILES)
    def _():
        sh = r["mod"][PB:PB + SB, 0:D]
        sc = r["mod"][PB:PB + SB, D:2 * D]
        sgu = functools.partial(_sgu_sample, coef_ref=r["coef"], bias_ref=r["bias8"])
        yn, h, tail, v = _mixer_tile(SB, r["x"][...], sh, sc, r["h0s"][...],
                                     r["convs"][...], sgu, r, r["hbuf"])
        r["y"][...] = yn
        r["hlast_s"][...] = h
        r["conv_s"][...] = tail
        r["v_s"][...] = v


def _mixer(x_all, mod_l, w):
    nb = PB + SB
    in_specs = [
        pl.BlockSpec((TILE, D), lambda i: (i, 0)),
        _const_spec((nb, MOD_COLS)),
        _const_spec((1, D)),
        _const_spec((D, 2 * LW + 2 * SW)),
        _const_spec((CONV_W, LW)),
        _const_spec((1, LW)),
        _const_spec((2, LW // 2, LW // 2)),
        _const_spec((2, LW // 2, LW // 2)),
        _const_spec((1, LW)),
        _const_spec((1, LW)),
        _const_spec((1, LW)),
        _const_spec((1, SW)),
        _const_spec((4, CHUNK, 2 * CHUNK)),
        _const_spec((CHUNK, SW)),
        _const_spec((DEC_SEQ, DEC_SEQ, SW)),
        _const_spec((DEC_SEQ, SW)),
        _const_spec((1, D)),
        _const_spec((256, 256)),
        _const_spec((SB, LW)),
        _const_spec(((CONV_W - 1) * SB, LW)),
    ]
    out_shape = (
        jax.ShapeDtypeStruct((NT, D), BF16),
        jax.ShapeDtypeStruct((PB, LW), F32),
        jax.ShapeDtypeStruct(((CONV_W - 1) * PB, LW), F32),
        jax.ShapeDtypeStruct((SB, LW), F32),
        jax.ShapeDtypeStruct(((CONV_W - 1) * SB, LW), F32),
        jax.ShapeDtypeStruct((NS, SW), F32),
    )
    out_specs = (
        pl.BlockSpec((TILE, D), lambda i: (i, 0)),
        _const_spec((PB, LW)),
        _const_spec(((CONV_W - 1) * PB, LW)),
        _const_spec((SB, LW)),
        _const_spec(((CONV_W - 1) * SB, LW)),
        _const_spec((NS, SW)),
    )
    scratch = [
        pltpu.VMEM((TILE, LW), F32),
        pltpu.VMEM((4, TILE, 128), F32),
        pltpu.VMEM((4, TILE, 128), F32),
        pltpu.VMEM((PB, LW), F32),
        pltpu.VMEM(((CONV_W - 1) * PB, LW), F32),
    ]
    return pl.pallas_call(
        _mixer_kernel,
        grid=(N_TILES,),
        in_specs=in_specs,
        out_specs=out_specs,
        out_shape=out_shape,
        scratch_shapes=scratch,
        compiler_params=pltpu.CompilerParams(
            dimension_semantics=("arbitrary",), vmem_limit_bytes=VMEM_LIMIT),
        name="mixer",
    )(x_all, mod_l, w["norm_mix_g"], w["w_in"], w["conv_w"], w["conv_b"], w["wr"], w["wi"],
      w["br"], w["bi"], w["lam"], w["sgu_g"], w["wpair"], w["bmap"], w["coef"], w["bias8"],
      w["out_g"], w["bd"], w["h0s"], w["convs"])


def _router_tile(G, r, gm, sc, sh):
    nst = TILE // G
    out = _dot(r["y"][...], r["w_out"][...])
    xm3 = r["x"][...].reshape(nst, G, D) + gm[None] * out.reshape(nst, G, D)
    r["x_mid"][...] = xm3.reshape(TILE, D)
    ms = jnp.mean(xm3 * xm3, axis=-1, keepdims=True)
    xn3 = xm3 * lax.rsqrt(ms + EPS) * r["norm_g"][...]
    xn = (xn3 * (1.0 + sc[None]) + sh[None]).reshape(TILE, D)
    r["xn"][...] = xn

    logits = lax.dot_general(r["rw_t"][...], xn.astype(BF16), (((1,), (1,)), ((), ())),
                             preferred_element_type=F32)
    sig = jax.nn.sigmoid(logits)
    sel = sig + r["rbias"][...]
    eiota = lax.broadcasted_iota(I32, (NE, TILE), 0)
    chosen = jnp.zeros((NE, TILE), F32)
    e_rows, w_rows, hots = [], [], []
    for _ in range(TOPK):
        m = jnp.max(sel, axis=0, keepdims=True)
        idx = jnp.min(jnp.where(sel == m, eiota, NE), axis=0, keepdims=True)
        hot = eiota == idx
        hots.append(hot)
        e_rows.append(idx)
        w_rows.append(jnp.sum(jnp.where(hot, sig, 0.0), axis=0, keepdims=True))
        chosen = jnp.where(hot, 1.0, chosen)
        sel = jnp.where(hot, -jnp.inf, sel)
    den = w_rows[0]
    for k in range(1, TOPK):
        den = den + w_rows[k]
    scale = ROUTED_SCALE / den
    r["w_t"][0] = jnp.concatenate([wk * scale for wk in w_rows], axis=0)
    r["e_t"][0] = jnp.concatenate(e_rows, axis=0)

    cnt = r["cnt"][...]
    rank = _dot(chosen.astype(BF16), r["tri"][...]) + cnt[:, 0:1]
    rk_rows = [jnp.sum(jnp.where(hot, rank, 0.0), axis=0, keepdims=True) for hot in hots]
    r["rank_t"][0] = jnp.concatenate(rk_rows, axis=0).astype(I32)
    cnt = cnt + jnp.sum(chosen, axis=1, keepdims=True)
    r["cnt"][...] = cnt
    r["counts"][...] = cnt


_RT_IN = ("x", "y", "mod", "w_out", "norm_g", "rw_t", "rbias", "tri")
_RT_OUT = ("x_mid", "xn", "e_t", "rank_t", "w_t", "counts")
_RT_SCR = ("cnt",)


def _router_kernel(*refs):
    r = dict(zip(_RT_IN + _RT_OUT + _RT_SCR, refs))
    i = pl.program_id(0)

    @pl.when(i == 0)
    def _():
        r["cnt"][...] = jnp.zeros((NE, 128), F32)

    def mods(lo, n):
        m = r["mod"]
        return m[lo:lo + n, 2 * D:3 * D], m[lo:lo + n, 4 * D:5 * D], m[lo:lo + n, 3 * D:4 * D]

    @pl.when(i < P_TILES)
    def _():
        _router_tile(PB, r, *mods(0, PB))

    @pl.when(i == P_TILES)
    def _():
        _router_tile(SB, r, *mods(PB, SB))


def _router(x_all, y_all, mod_l, w):
    nb = PB + SB
    in_specs = [
        pl.BlockSpec((TILE, D), lambda i: (i, 0)),
        pl.BlockSpec((TILE, D), lambda i: (i, 0)),
        _const_spec((nb, MOD_COLS)),
        _const_spec((D, D)),
        _const_spec((1, D)),
        _const_spec((NE, D)),
        _const_spec((NE, 1)),
        _const_spec((TILE, TILE)),
    ]
    out_shape = (
        jax.ShapeDtypeStruct((NT, D), F32),
        jax.ShapeDtypeStruct((NT, D), F32),
        jax.ShapeDtypeStruct((N_TILES, TOPK, TILE), I32),
        jax.ShapeDtypeStruct((N_TILES, TOPK, TILE), I32),
        jax.ShapeDtypeStruct((N_TILES, TOPK, TILE), F32),
        jax.ShapeDtypeStruct((NE, 128), F32),
    )
    out_specs = (
        pl.BlockSpec((TILE, D), lambda i: (i, 0)),
        pl.BlockSpec((TILE, D), lambda i: (i, 0)),
        pl.BlockSpec((1, TOPK, TILE), lambda i: (i, 0, 0)),
        pl.BlockSpec((1, TOPK, TILE), lambda i: (i, 0, 0)),
        pl.BlockSpec((1, TOPK, TILE), lambda i: (i, 0, 0)),
        _const_spec((NE, 128)),
    )
    return pl.pallas_call(
        _router_kernel,
        grid=(N_TILES,),
        in_specs=in_specs,
        out_specs=out_specs,
        out_shape=out_shape,
        scratch_shapes=[pltpu.VMEM((NE, 128), F32)],
        compiler_params=pltpu.CompilerParams(
            dimension_semantics=("arbitrary",), vmem_limit_bytes=VMEM_LIMIT),
        name="router",
    )(x_all, y_all, mod_l, w["w_out"], w["norm_ffn_g"], w["rw_t"], w["rbias"], w["tri"])


def _dispatch_kernel(dest_ref, xn_ref, xs_ref, sem):
    def row_copy(t, k):
        return pltpu.make_async_copy(
            xn_ref.at[pl.ds(t, 1)], xs_ref.at[pl.ds(dest_ref[0, k, t], 1)], sem)

    def body(t, carry):
        for k in range(TOPK):
            row_copy(t, k).start(priority=k % 2)
        return carry

    lax.fori_loop(0, TILE, body, 0)
    for _ in range(TOPK):
        pltpu.make_async_copy(xn_ref, xs_ref.at[pl.ds(0, TILE)], sem).wait()


def _dispatch(dest, xn):
    return pl.pallas_call(
        _dispatch_kernel,
        grid=(N_TILES,),
        in_specs=[
            pl.BlockSpec((1, TOPK, TILE), lambda i: (i, 0, 0), memory_space=pltpu.SMEM),
            pl.BlockSpec((TILE, D), lambda i: (i, 0)),
        ],
        out_specs=pl.BlockSpec(memory_space=pl.ANY),
        out_shape=jax.ShapeDtypeStruct((NSLOT, D), F32),
        scratch_shapes=[pltpu.SemaphoreType.DMA(())],
        compiler_params=pltpu.CompilerParams(
            dimension_semantics=("arbitrary",), vmem_limit_bytes=VMEM_LIMIT),
        name="dispatch",
    )(dest, xn)


def _expert_kernel(tile_ref, exp_ref, lo_ref, hi_ref, flag_ref, xs_ref, wg_ref, wu_ref, wd_ref,
                   ys_ref):
    v = pl.program_id(0)
    flag = flag_ref[v]

    @pl.when(flag > 0)
    def _():
        x = xs_ref[...].astype(BF16)
        hid = jax.nn.silu(_dot(x, wg_ref[0])) * _dot(x, wu_ref[0])
        y = _dot(hid.astype(BF16), wd_ref[0])
        row = lax.broadcasted_iota(I32, (ETILE, 1), 0)
        mine = (row >= lo_ref[v]) & (row < hi_ref[v])

        @pl.when(flag == 2)
        def _():
            ys_ref[...] = jnp.where(mine, y, 0.0)

        @pl.when(flag == 1)
        def _():
            ys_ref[...] = jnp.where(mine, y, ys_ref[...])


def _experts(meta, xs, wg, wu, wd):
    grid_spec = pltpu.PrefetchScalarGridSpec(
        num_scalar_prefetch=5,
        grid=(N_VISITS,),
        in_specs=[
            pl.BlockSpec((ETILE, D), lambda v, t, e, lo, hi, f: (t[v], 0)),
            pl.BlockSpec((1, D, ED), lambda v, t, e, lo, hi, f: (e[v], 0, 0)),
            pl.BlockSpec((1, D, ED), lambda v, t, e, lo, hi, f: (e[v], 0, 0)),
            pl.BlockSpec((1, ED, D), lambda v, t, e, lo, hi, f: (e[v], 0, 0)),
        ],
        out_specs=pl.BlockSpec((ETILE, D), lambda v, t, e, lo, hi, f: (t[v], 0)),
    )
    return pl.pallas_call(
        _expert_kernel,
        grid_spec=grid_spec,
        out_shape=jax.ShapeDtypeStruct((NSLOT, D), F32),
        compiler_params=pltpu.CompilerParams(
            dimension_semantics=("arbitrary",), vmem_limit_bytes=VMEM_LIMIT),
        name="experts",
    )(*meta, xs, wg, wu, wd)


def _combine_tile(G, r, gf, final):
    nst = CTILE // G
    xb = r["xn"][...].astype(BF16)
    hid = jax.nn.silu(_dot(xb, r["sg"][...])) * _dot(xb, r["su"][...])
    moe = _dot(hid.astype(BF16), r["sd"][...])
    for _ in range(TOPK):
        pltpu.make_async_copy(r["ys"].at[pl.ds(0, CTILE)], r["buf"].at[0], r["sem"]).wait()
    wts = r["wts"][...]
    for k in range(TOPK):
        moe = moe + wts[:, k:k + 1] * r["buf"][k]
    xo = r["x_mid"][...].reshape(nst, G, D) + gf[None] * moe.reshape(nst, G, D)
    r["x_out"][...] = xo.reshape(CTILE, D)
    if final:
        ms = jnp.mean(xo * xo, axis=-1, keepdims=True)
        r["y_out"][...] = (xo * lax.rsqrt(ms + EPS) * r["fin_g"][...]).reshape(CTILE, D)


def _combine_kernel(final, *refs):
    names = ["dest", "ys", "x_mid", "xn", "wts", "mod", "sg", "su", "sd"]
    names += ["fin_g"] if final else []
    names += ["x_out"] + (["y_out"] if final else []) + ["buf", "sem"]
    r = dict(zip(names, refs))
    i = pl.program_id(0)

    def body(t, carry):
        for k in range(TOPK):
            pltpu.make_async_copy(
                r["ys"].at[pl.ds(r["dest"][0, k, t], 1)], r["buf"].at[k, pl.ds(t, 1)],
                r["sem"]).start(priority=k % 2)
        return carry

    lax.fori_loop(0, CTILE, body, 0)

    @pl.when(i < PC_TILES)
    def _():
        _combine_tile(PB, r, r["mod"][0:PB, 5 * D:6 * D], final)

    @pl.when(i >= PC_TILES)
    def _():
        _combine_tile(SB, r, r["mod"][PB:PB + SB, 5 * D:6 * D], final)


def _combine(dest_c, ys, x_mid, xn, wts, mod_l, w, fin_g):
    final = fin_g is not None
    nb = PB + SB
    in_specs = [
        pl.BlockSpec((1, TOPK, CTILE), lambda i: (i, 0, 0), memory_space=pltpu.SMEM),
        pl.BlockSpec(memory_space=pl.ANY),
        pl.BlockSpec((CTILE, D), lambda i: (i, 0)),
        pl.BlockSpec((CTILE, D), lambda i: (i, 0)),
        pl.BlockSpec((CTILE, TOPK), lambda i: (i, 0)),
        _const_spec((nb, MOD_COLS)),
        _const_spec((D, ED)),
        _const_spec((D, ED)),
        _const_spec((ED, D)),
    ]
    args = [dest_c, ys, x_mid, xn, wts, mod_l, w["sg"], w["su"], w["sd"]]
    out_shape = [jax.ShapeDtypeStruct((NT, D), F32)]
    out_specs = [pl.BlockSpec((CTILE, D), lambda i: (i, 0))]
    if final:
        in_specs.append(_const_spec((1, D)))
        args.append(fin_g)
        out_shape.append(jax.ShapeDtypeStruct((NT, D), F32))
        out_specs.append(pl.BlockSpec((CTILE, D), lambda i: (i, 0)))
    res = pl.pallas_call(
        functools.partial(_combine_kernel, final),
        grid=(N_CTILES,),
        in_specs=in_specs,
        out_specs=tuple(out_specs),
        out_shape=tuple(out_shape),
        scratch_shapes=[pltpu.VMEM((TOPK, CTILE, D), F32), pltpu.SemaphoreType.DMA(())],
        compiler_params=pltpu.CompilerParams(
            dimension_semantics=("arbitrary",), vmem_limit_bytes=VMEM_LIMIT),
        name="combine_final" if final else "combine",
    )(*args)
    return res if final else (res[0], None)


def _routing_meta(counts, e_t, rank_t):
    cnt = counts[:, 0].astype(I32)
    end = jnp.cumsum(cnt)
    off = end - cnt
    eids = jnp.arange(NE, dtype=I32)

    def lookup(table, idx):
        return jnp.sum(jnp.where(idx[..., None] == eids, table, 0), axis=-1)

    dest = lookup(off, e_t) + rank_t
    first_tile = off // ETILE
    last_tile = (end - 1) // ETILE
    nvis = jnp.where(cnt > 0, last_tile - first_tile + 1, 0)
    vend = jnp.cumsum(nvis)
    vstart = vend - nvis
    total = vend[-1]
    v = jnp.arange(N_VISITS, dtype=I32)
    valid = v < total
    vv = jnp.minimum(v, total - 1)
    e = jnp.minimum(jnp.sum((vend[None, :] <= vv[:, None]).astype(I32), axis=1), NE - 1)
    off_e = lookup(off, e)
    tile = lookup(first_tile, e) + (vv - lookup(vstart, e))
    base = tile * ETILE
    lo = jnp.clip(off_e - base, 0, ETILE)
    hi = jnp.clip(lookup(end, e) - base, 0, ETILE)
    flag = jnp.where(valid, jnp.where(off_e <= base, 2, 1), 0).astype(I32)
    return dest, (tile.astype(I32), e, lo.astype(I32), hi.astype(I32), flag)


def _block_diag(wh):
    out = jnp.zeros((2, 4, HEAD, 4, HEAD), wh.dtype)
    for hh in range(4):
        out = out.at[:, hh, :, hh, :].set(wh.reshape(2, 4, HEAD, HEAD)[:, hh])
    return out.reshape(2, 4 * HEAD, 4 * HEAD)


def _layer_weights(l, p):
    tril = jnp.tril(jnp.ones((CHUNK, CHUNK), bool))
    wt = jnp.where(tril, p["sgu_w"][l], 0.0)
    wpair = wt.reshape(4, 2, CHUNK, CHUNK).transpose(0, 2, 1, 3).reshape(4, CHUNK, 2 * CHUNK)
    bmap = jnp.repeat(p["sgu_b"][l].T, HEAD, axis=1)
    w8 = jnp.where(tril[:DEC_SEQ, :DEC_SEQ], p["sgu_w"][l][:, :DEC_SEQ, :DEC_SEQ], 0.0)
    coef = jnp.repeat(w8.transpose(2, 1, 0), HEAD, axis=2)
    bias8 = jnp.repeat(p["sgu_b"][l][:, :DEC_SEQ].T, HEAD, axis=1)
    gid = jnp.arange(256) // HEAD
    return dict(
        norm_mix_g=p["norm_mix_g"][l][None], norm_ffn_g=p["norm_ffn_g"][l][None],
        w_in=p["w_in"][l].astype(BF16), w_out=p["w_out"][l].astype(BF16),
        conv_w=p["conv_w"][l], conv_b=p["conv_b"][l][None],
        wr=_block_diag(p["gate_r_w"][l]).astype(BF16), wi=_block_diag(p["gate_i_w"][l]).astype(BF16),
        br=p["gate_r_b"][l][None], bi=p["gate_i_b"][l][None], lam=p["lru_lambda"][l][None],
        sgu_g=p["sgu_norm_g"][l][None], wpair=wpair.astype(BF16), bmap=bmap, coef=coef, bias8=bias8,
        out_g=p["out_norm_g"][l][None],
        bd=(gid[:, None] == gid[None, :]).astype(BF16),
        rw_t=p["router_w"][l].T.astype(BF16), rbias=p["router_bias"][l][:, None],
        tri=jnp.triu(jnp.ones((TILE, TILE), BF16), k=1),
        wg=p["exp_w_gate"][l].astype(BF16), wu=p["exp_w_up"][l].astype(BF16),
        wd=p["exp_w_down"][l].astype(BF16),
        sg=p["shared_w_gate"][l].astype(BF16), su=p["shared_w_up"][l].astype(BF16),
        sd=p["shared_w_down"][l].astype(BF16),
    )


def kernel(x_prompt, x_sample, c_prompt, c_sample, state_lru_h, state_conv, ada_w, ada_b, norm_mix_g, norm_ffn_g, w_in, conv_w, conv_b, gate_r_w, gate_r_b, gate_i_w, gate_i_b, lru_lambda, sgu_norm_g, sgu_w, sgu_b, out_norm_g, w_out, router_w, router_bias, exp_w_gate, exp_w_up, exp_w_down, shared_w_gate, shared_w_up, shared_w_down, final_norm_g):
    p = dict(norm_mix_g=norm_mix_g, norm_ffn_g=norm_ffn_g, w_in=w_in, conv_w=conv_w, conv_b=conv_b,
             gate_r_w=gate_r_w, gate_r_b=gate_r_b, gate_i_w=gate_i_w, gate_i_b=gate_i_b,
             lru_lambda=lru_lambda, sgu_norm_g=sgu_norm_g, sgu_w=sgu_w, sgu_b=sgu_b,
             out_norm_g=out_norm_g, w_out=w_out, router_w=router_w, router_bias=router_bias,
             exp_w_gate=exp_w_gate, exp_w_up=exp_w_up, exp_w_down=exp_w_down,
             shared_w_gate=shared_w_gate, shared_w_up=shared_w_up, shared_w_down=shared_w_down)
    x_all = jnp.concatenate([x_prompt.transpose(1, 0, 2).reshape(NP, D),
                             x_sample.transpose(1, 0, 2).reshape(NS, D)], axis=0)
    mod = _modulations(jnp.concatenate([c_prompt, c_sample], axis=0), ada_w, ada_b)

    hp, cp, hs, cs, vs = [], [], [], [], []
    y_all = None
    for l in range(DEPTH):
        w = _layer_weights(l, p)
        w["h0s"] = state_lru_h[l]
        w["convs"] = state_conv[l].transpose(1, 0, 2).reshape((CONV_W - 1) * SB, LW)
        yn, hlp, cvp, hls, cvs, v_s = _mixer(x_all, mod[l], w)
        x_mid, xn, e_t, rank_t, w_t, counts = _router(x_all, yn, mod[l], w)
        dest, meta = _routing_meta(counts, e_t, rank_t)
        xs = _dispatch(dest, xn)
        ys = _experts(meta, xs, w["wg"], w["wu"], w["wd"])
        half = TILE // CTILE
        dest_c = dest.reshape(N_TILES, TOPK, half, CTILE).transpose(0, 2, 1, 3).reshape(
            N_CTILES, TOPK, CTILE)
        wts = w_t.transpose(0, 2, 1).reshape(NT, TOPK)
        fin = final_norm_g[None] if l == DEPTH - 1 else None
        x_all, y_all = _combine(dest_c, ys, x_mid, xn, wts, mod[l], w, fin)
        hp.append(hlp)
        cp.append(cvp.reshape(CONV_W - 1, PB, LW).transpose(1, 0, 2))
        hs.append(hls)
        cs.append(cvs.reshape(CONV_W - 1, SB, LW).transpose(1, 0, 2))
        vs.append(v_s.reshape(DEC_SEQ, SB, SW).transpose(1, 0, 2))

    y_prompt = y_all[:NP].reshape(SEQ, PB, D).transpose(1, 0, 2)
    y_sample = y_all[NP:].reshape(DEC_SEQ, SB, D).transpose(1, 0, 2)
    return (y_prompt, y_sample, jnp.stack(hp), jnp.stack(cp), jnp.stack(hs), jnp.stack(cs),
            jnp.stack(vs))
```

```python
import functools

import jax
import jax.numpy as jnp
from jax import lax
from jax.experimental import pallas as pl
from jax.experimental.pallas import tpu as pltpu

F32 = jnp.float32
BF16 = jnp.bfloat16
I32 = jnp.int32
U32 = jnp.uint32

D = 1024
DEPTH = 4
PB = 8
SEQ = 2048
SB = 128
DEC_SEQ = 8
NP = PB * SEQ
NS = SB * DEC_SEQ
NT = NP + NS
TILE = 1024
N_TILES = NT // TILE
P_TILES = NP // TILE
LW = 512
SW = 512
HEAD = 64
CHUNK = 128
CONV_W = 4
LRU_C = 8.0
NE = 64
TOPK = 8
ED = 256
ROUTED_SCALE = 2.5
EPS = 1e-6
MOD_COLS = 6 * D
MOD_BLK = 1536
VMEM_LIMIT = 56 * 1024 * 1024

TB = 256
NBLK = NT // TB
PBLK = NP // TB
BPT = TILE // TB
ROWS = 8
LS = 2560
NCH = LS // ROWS
SUB = 512
NSLOT_MAX = NT * TOPK + NBLK * NE * (ROWS - 1)
ETILE = 256
N_ETILES = NSLOT_MAX // ETILE
N_VISITS = N_ETILES + NE - 1
XS_ROWS = NSLOT_MAX + 2 * LS
N_XTILES = XS_ROWS // ETILE
DP = D // 2

assert NSLOT_MAX % ETILE == 0 and LS >= TB * TOPK + NE * (ROWS - 1) and LS % SUB == 0


def _dot(a, b):
    return jnp.dot(a, b, preferred_element_type=F32)


def _const_spec(shape):
    nd = len(shape)
    return pl.BlockSpec(shape, lambda *_: (0,) * nd)


def _pack(x):
    hi = lax.bitcast_convert_type(x[:, :DP], U32) & jnp.uint32(0xFFFF0000)
    lo = lax.bitcast_convert_type(x[:, DP:], U32) >> 16
    return hi | lo


def _unpack(w):
    hi = lax.bitcast_convert_type(w & jnp.uint32(0xFFFF0000), F32)
    lo = lax.bitcast_convert_type(w << 16, F32)
    return jnp.concatenate([hi, lo], axis=1).astype(BF16)


def _mod_kernel(c_ref, w_ref, b_ref, o_ref):
    s = jax.nn.silu(c_ref[...]).astype(BF16)
    o_ref[0] = _dot(s, w_ref[0].astype(BF16)) + b_ref[0]


def _modulations(c_all, ada_w, ada_b):
    nb = c_all.shape[0]
    return pl.pallas_call(
        _mod_kernel,
        grid=(DEPTH, MOD_COLS // MOD_BLK),
        in_specs=[
            pl.BlockSpec((nb, D), lambda l, j: (0, 0)),
            pl.BlockSpec((1, D, MOD_BLK), lambda l, j: (l, 0, j)),
            pl.BlockSpec((1, 1, MOD_BLK), lambda l, j: (l, 0, j)),
        ],
        out_specs=pl.BlockSpec((1, nb, MOD_BLK), lambda l, j: (l, 0, j)),
        out_shape=jax.ShapeDtypeStruct((DEPTH, nb, MOD_COLS), F32),
        compiler_params=pltpu.CompilerParams(
            dimension_semantics=("arbitrary", "arbitrary"), vmem_limit_bytes=VMEM_LIMIT),
        name="adaln_mod",
    )(c_all, ada_w, ada_b.reshape(DEPTH, 1, MOD_COLS))


def _sgu_prompt(v, wpair_ref, bmap_ref, vbuf, sbuf):
    for j in range(4):
        vbuf[j] = v[:, 128 * j:128 * (j + 1)]
    lane = lax.broadcasted_iota(I32, (CHUNK, 128), 1)
    left = lane < HEAD
    for b in range(PB):
        for j in range(4):
            vp = vbuf[j, pl.ds(b, CHUNK, stride=PB), :]
            rhs = jnp.concatenate(
                [jnp.where(left, vp, 0.0), jnp.where(left, 0.0, vp)], axis=0).astype(BF16)
            sj = _dot(wpair_ref[j], rhs) + bmap_ref[:, 128 * j:128 * (j + 1)]
            sbuf[j, pl.ds(b, CHUNK, stride=PB), :] = sj
    return jnp.concatenate([sbuf[j] for j in range(4)], axis=1)


def _sgu_sample(v, coef_ref, bias_ref):
    vt = [v[t * SB:(t + 1) * SB] for t in range(DEC_SEQ)]
    rows = []
    for p in range(DEC_SEQ):
        acc = bias_ref[p:p + 1, :] + coef_ref[0, p:p + 1, :] * vt[0]
        for q in range(1, p + 1):
            acc = acc + coef_ref[q, p:p + 1, :] * vt[q]
        rows.append(acc)
    return jnp.concatenate(rows, axis=0)


def _mixer_tile(G, x, sh, sc, h_in, tail_in, sgu_fn, r, hbuf):
    nst = TILE // G
    x3 = x.reshape(nst, G, D)
    ms = jnp.mean(x3 * x3, axis=-1, keepdims=True)
    xn = x3 * lax.rsqrt(ms + EPS) * r["norm_g"][...]
    xn = (xn * (1.0 + sc[None]) + sh[None]).reshape(TILE, D).astype(BF16)

    xa = _dot(xn, r["w_in"][:, 0:LW])
    xp = jnp.concatenate([tail_in, xa], axis=0)
    cw = r["conv_w"]
    xc = r["conv_b"][...] + cw[0:1, :] * xp[0:TILE]
    for k in range(1, CONV_W):
        xc = xc + cw[k:k + 1, :] * xp[k * G:k * G + TILE]
    new_tail = xa[TILE - (CONV_W - 1) * G:]
    xcb = xc.astype(BF16)
    half = LW // 2
    r_pre = jnp.concatenate(
        [_dot(xcb[:, :half], r["wr"][0]), _dot(xcb[:, half:], r["wr"][1])], axis=1) + r["br"][...]
    i_pre = jnp.concatenate(
        [_dot(xcb[:, :half], r["wi"][0]), _dot(xcb[:, half:], r["wi"][1])], axis=1) + r["bi"][...]
    rg = jax.nn.sigmoid(r_pre)
    ig = jax.nn.sigmoid(i_pre)
    log_a = -LRU_C * rg * jax.nn.softplus(-r["lam"][...])
    a = jnp.exp(log_a)
    th = jnp.tanh(log_a)
    u = jnp.sqrt(-2.0 * th / (1.0 - th)) * (ig * xc)
    h = h_in
    for s in range(nst):
        h = a[s * G:(s + 1) * G] * h + u[s * G:(s + 1) * G]
        hbuf[s * G:(s + 1) * G, :] = h
    ya = hbuf[...] * jax.nn.gelu(_dot(xn, r["w_in"][:, LW:2 * LW]))

    ug = jax.nn.gelu(_dot(xn, r["w_in"][:, 2 * LW:2 * LW + SW]))
    vg = jax.nn.gelu(_dot(xn, r["w_in"][:, 2 * LW + SW:]))
    v = vg * lax.rsqrt(jnp.mean(vg * vg, axis=-1, keepdims=True) + EPS) * r["sgu_g"][...]
    yb = ug * sgu_fn(v)

    y = jnp.concatenate([ya, yb], axis=1)
    bd = r["bd"][...]
    outs = []
    for j in range(D // 256):
        ysl = y[:, 256 * j:256 * (j + 1)]
        sq = ysl * ysl
        hi = sq.astype(BF16)
        lo = (sq - hi.astype(F32)).astype(BF16)
        msq = (_dot(hi, bd) + _dot(lo, bd)) * (1.0 / HEAD)
        outs.append(ysl * lax.rsqrt(msq + EPS) * r["out_g"][:, 256 * j:256 * (j + 1)])
    yn = jnp.concatenate(outs, axis=1).astype(BF16)
    return yn, h, new_tail, v


_MIX_IN = ("x", "mod", "norm_g", "w_in", "conv_w", "conv_b", "wr", "wi", "br", "bi", "lam",
           "sgu_g", "wpair", "bmap", "coef", "bias8", "out_g", "bd", "h0s", "convs")
_MIX_OUT = ("y", "hlast_p", "conv_p", "hlast_s", "conv_s", "v_s")
_MIX_SCR = ("hbuf", "vbuf", "sbuf", "h_carry", "tail_carry")


def _mixer_kernel(*refs):
    names = _MIX_IN + _MIX_OUT + _MIX_SCR
    r = dict(zip(names, refs))
    i = pl.program_id(0)

    @pl.when(i == 0)
    def _():
        r["h_carry"][...] = jnp.zeros((PB, LW), F32)
        r["tail_carry"][...] = jnp.zeros(((CONV_W - 1) * PB, LW), F32)

    @pl.when(i < P_TILES)
    def _():
        sh = r["mod"][0:PB, 0:D]
        sc = r["mod"][0:PB, D:2 * D]
        sgu = functools.partial(_sgu_prompt, wpair_ref=r["wpair"], bmap_ref=r["bmap"],
                                vbuf=r["vbuf"], sbuf=r["sbuf"])
        yn, h, tail, _ = _mixer_tile(PB, r["x"][...], sh, sc, r["h_carry"][...],
                                     r["tail_carry"][...], sgu, r, r["hbuf"])
        r["y"][...] = yn
        r["h_carry"][...] = h
        r["tail_carry"][...] = tail
        r["hlast_p"][...] = h
        r["conv_p"][...] = tail

    @pl.when(i == P_TILES)
    def _():
        sh = r["mod"][PB:PB + SB, 0:D]
        sc = r["mod"][PB:PB + SB, D:2 * D]
        sgu = functools.partial(_sgu_sample, coef_ref=r["coef"], bias_ref=r["bias8"])
        yn, h, tail, v = _mixer_tile(SB, r["x"][...], sh, sc, r["h0s"][...],
                                     r["convs"][...], sgu, r, r["hbuf"])
        r["y"][...] = yn
        r["hlast_s"][...] = h
        r["conv_s"][...] = tail
        r["v_s"][...] = v


def _mixer(x_all, mod_l, w):
    nb = PB + SB
    in_specs = [
        pl.BlockSpec((TILE, D), lambda i: (i, 0)),
        _const_spec((nb, MOD_COLS)),
        _const_spec((1, D)),
        _const_spec((D, 2 * LW + 2 * SW)),
        _const_spec((CONV_W, LW)),
        _const_spec((1, LW)),
        _const_spec((2, LW // 2, LW // 2)),
        _const_spec((2, LW // 2, LW // 2)),
        _const_spec((1, LW)),
        _const_spec((1, LW)),
        _const_spec((1, LW)),
        _const_spec((1, SW)),
        _const_spec((4, CHUNK, 2 * CHUNK)),
        _const_spec((CHUNK, SW)),
        _const_spec((DEC_SEQ, DEC_SEQ, SW)),
        _const_spec((DEC_SEQ, SW)),
        _const_spec((1, D)),
        _const_spec((256, 256)),
        _const_spec((SB, LW)),
        _const_spec(((CONV_W - 1) * SB, LW)),
    ]
    out_shape = (
        jax.ShapeDtypeStruct((NT, D), BF16),
        jax.ShapeDtypeStruct((PB, LW), F32),
        jax.ShapeDtypeStruct(((CONV_W - 1) * PB, LW), F32),
        jax.ShapeDtypeStruct((SB, LW), F32),
        jax.ShapeDtypeStruct(((CONV_W - 1) * SB, LW), F32),
        jax.ShapeDtypeStruct((NS, SW), F32),
    )
    out_specs = (
        pl.BlockSpec((TILE, D), lambda i: (i, 0)),
        _const_spec((PB, LW)),
        _const_spec(((CONV_W - 1) * PB, LW)),
        _const_spec((SB, LW)),
        _const_spec(((CONV_W - 1) * SB, LW)),
        _const_spec((NS, SW)),
    )
    scratch = [
        pltpu.VMEM((TILE, LW), F32),
        pltpu.VMEM((4, TILE, 128), F32),
        pltpu.VMEM((4, TILE, 128), F32),
        pltpu.VMEM((PB, LW), F32),
        pltpu.VMEM(((CONV_W - 1) * PB, LW), F32),
    ]
    return pl.pallas_call(
        _mixer_kernel,
        grid=(N_TILES,),
        in_specs=in_specs,
        out_specs=out_specs,
        out_shape=out_shape,
        scratch_shapes=scratch,
        compiler_params=pltpu.CompilerParams(
            dimension_semantics=("arbitrary",), vmem_limit_bytes=VMEM_LIMIT),
        name="mixer",
    )(x_all, mod_l, w["norm_mix_g"], w["w_in"], w["conv_w"], w["conv_b"], w["wr"], w["wi"],
      w["br"], w["bi"], w["lam"], w["sgu_g"], w["wpair"], w["bmap"], w["coef"], w["bias8"],
      w["out_g"], w["bd"], w["h0s"], w["convs"])


def _router_tile(G, r, gm, sc, sh):
    nst = TILE // G
    out = _dot(r["y"][...], r["w_out"][...])
    xm3 = r["x"][...].reshape(nst, G, D) + gm[None] * out.reshape(nst, G, D)
    r["x_mid"][...] = xm3.reshape(TILE, D)
    ms = jnp.mean(xm3 * xm3, axis=-1, keepdims=True)
    xn3 = xm3 * lax.rsqrt(ms + EPS) * r["norm_g"][...]
    xn = (xn3 * (1.0 + sc[None]) + sh[None]).reshape(TILE, D)
    r["xn"][...] = xn

    logits = lax.dot_general(r["rw_t"][...], xn.astype(BF16), (((1,), (1,)), ((), ())),
                             preferred_element_type=F32)
    sig = jax.nn.sigmoid(logits)
    sel = sig + r["rbias"][...]
    eiota = lax.broadcasted_iota(I32, (NE, TILE), 0)
    chosen = jnp.zeros((NE, TILE), F32)
    w_rows, hots = [], []
    for _ in range(TOPK):
        m = jnp.max(sel, axis=0, keepdims=True)
        idx = jnp.min(jnp.where(sel == m, eiota, NE), axis=0, keepdims=True)
        hot = eiota == idx
        hots.append(hot)
        w_rows.append(jnp.sum(jnp.where(hot, sig, 0.0), axis=0, keepdims=True))
        chosen = jnp.where(hot, 1.0, chosen)
        sel = jnp.where(hot, -jnp.inf, sel)
    den = w_rows[0]
    for k in range(1, TOPK):
        den = den + w_rows[k]
    scale = ROUTED_SCALE / den
    r["w_t"][0] = jnp.concatenate([wk * scale for wk in w_rows], axis=0)

    cb = chosen.astype(BF16)
    rank = _dot(cb, r["tri"][...])
    cnt = _dot(cb, r["bsel"][...])
    groups = jnp.floor((cnt + (ROWS - 1.0)) * (1.0 / ROWS))
    start = ROWS * _dot(r["ltri"][...], groups.astype(BF16))
    pos = rank + jnp.concatenate(
        [jnp.broadcast_to(start[:, j:j + 1], (NE, TB)) for j in range(BPT)], axis=1)
    lp_rows = [jnp.sum(jnp.where(hot, pos, 0.0), axis=0, keepdims=True) for hot in hots]
    r["lpos_t"][0] = jnp.concatenate(lp_rows, axis=0).astype(I32)
    r["cnt"][0] = cnt


_RT_IN = ("x", "y", "mod", "w_out", "norm_g", "rw_t", "rbias", "tri", "bsel", "ltri")
_RT_OUT = ("x_mid", "xn", "lpos_t", "w_t", "cnt")


def _router_kernel(*refs):
    r = dict(zip(_RT_IN + _RT_OUT, refs))
    i = pl.program_id(0)

    def mods(lo, n):
        m = r["mod"]
        return m[lo:lo + n, 2 * D:3 * D], m[lo:lo + n, 4 * D:5 * D], m[lo:lo + n, 3 * D:4 * D]

    @pl.when(i < P_TILES)
    def _():
        _router_tile(PB, r, *mods(0, PB))

    @pl.when(i == P_TILES)
    def _():
        _router_tile(SB, r, *mods(PB, SB))


def _router(x_all, y_all, mod_l, w):
    nb = PB + SB
    in_specs = [
        pl.BlockSpec((TILE, D), lambda i: (i, 0)),
        pl.BlockSpec((TILE, D), lambda i: (i, 0)),
        _const_spec((nb, MOD_COLS)),
        _const_spec((D, D)),
        _const_spec((1, D)),
        _const_spec((NE, D)),
        _const_spec((NE, 1)),
        _const_spec((TILE, TILE)),
        _const_spec((TILE, 128)),
        _const_spec((NE, NE)),
    ]
    out_shape = (
        jax.ShapeDtypeStruct((NT, D), F32),
        jax.ShapeDtypeStruct((NT, D), F32),
        jax.ShapeDtypeStruct((N_TILES, TOPK, TILE), I32),
        jax.ShapeDtypeStruct((N_TILES, TOPK, TILE), F32),
        jax.ShapeDtypeStruct((N_TILES, NE, 128), F32),
    )
    out_specs = (
        pl.BlockSpec((TILE, D), lambda i: (i, 0)),
        pl.BlockSpec((TILE, D), lambda i: (i, 0)),
        pl.BlockSpec((1, TOPK, TILE), lambda i: (i, 0, 0)),
        pl.BlockSpec((1, TOPK, TILE), lambda i: (i, 0, 0)),
        pl.BlockSpec((1, NE, 128), lambda i: (i, 0, 0)),
    )
    return pl.pallas_call(
        _router_kernel,
        grid=(N_TILES,),
        in_specs=in_specs,
        out_specs=out_specs,
        out_shape=out_shape,
        compiler_params=pltpu.CompilerParams(
            dimension_semantics=("arbitrary",), vmem_limit_bytes=VMEM_LIMIT),
        name="router",
    )(x_all, y_all, mod_l, w["w_out"], w["norm_ffn_g"], w["rw_t"], w["rbias"], w["tri"],
      w["bsel"], w["ltri"])


def _chunk_copies(tab_ref, make_copy):
    def body(i, carry):
        for u in range(2):
            j = 2 * i + u
            g = pl.multiple_of(tab_ref[0, 0, j], ROWS)
            make_copy(pl.multiple_of(j * ROWS, ROWS), g).start(priority=u)
        return carry

    lax.fori_loop(0, NCH // 2, body, 0)


def _dispatch_kernel(tab_ref, fill_ref, lpos_ref, xn_ref, xs_ref, loc, zbuf, sem):
    b = pl.program_id(0)
    slot = b % 2
    xb = xn_ref[...].astype(BF16)
    lp = lpos_ref[0]
    for c in range(LS // SUB):
        siota = lax.broadcasted_iota(I32, (SUB, TB), 0) + c * SUB
        p = jnp.where(siota == lp[0:1, :], 1.0, 0.0)
        for k in range(1, TOPK):
            p = p + jnp.where(siota == lp[k:k + 1, :], 1.0, 0.0)
        loc[slot, c * SUB:(c + 1) * SUB, :] = _pack(_dot(p.astype(BF16), xb))

    _chunk_copies(tab_ref, lambda l, g: pltpu.make_async_copy(
        loc.at[slot, pl.ds(l, ROWS)], xs_ref.at[pl.ds(g, ROWS)], sem.at[slot]))

    def drain(s):
        pltpu.make_async_copy(loc.at[s], xs_ref.at[pl.ds(0, LS)], sem.at[s]).wait()

    @pl.when(b > 0)
    def _():
        drain(1 - slot)

    @pl.when(b == NBLK - 1)
    def _():
        drain(slot)
        zbuf[...] = jnp.zeros((ETILE, DP), U32)

        def zero_tile(t):
            return pltpu.make_async_copy(
                zbuf, xs_ref.at[pl.ds(pl.multiple_of(t * ETILE, ETILE), ETILE)], sem.at[0])

        def start(t, carry):
            zero_tile(t).start()
            return carry

        def wait(t, carry):
            zero_tile(t).wait()
            return carry

        lax.fori_loop(fill_ref[0], N_XTILES, start, 0)
        lax.fori_loop(fill_ref[0], N_XTILES, wait, 0)


def _dispatch(tab_d, fill_from, lpos_d, xn):
    return pl.pallas_call(
        _dispatch_kernel,
        grid=(NBLK,),
        in_specs=[
            pl.BlockSpec((1, 1, NCH), lambda b: (b, 0, 0), memory_space=pltpu.SMEM),
            pl.BlockSpec(memory_space=pltpu.SMEM),
            pl.BlockSpec((1, TOPK, TB), lambda b: (b, 0, 0)),
            pl.BlockSpec((TB, D), lambda b: (b, 0)),
        ],
        out_specs=pl.BlockSpec(memory_space=pl.ANY),
        out_shape=jax.ShapeDtypeStruct((XS_ROWS, DP), U32),
        scratch_shapes=[pltpu.VMEM((2, LS, DP), U32), pltpu.VMEM((ETILE, DP), U32),
                        pltpu.SemaphoreType.DMA((2,))],
        compiler_params=pltpu.CompilerParams(
            dimension_semantics=("arbitrary",), vmem_limit_bytes=VMEM_LIMIT),
        name="dispatch",
    )(tab_d, fill_from, lpos_d, xn)


def _expert_kernel(tin_ref, tout_ref, exp_ref, lo_ref, hi_ref, flag_ref, new_ref, xs_ref, wg_ref,
                   wu_ref, wd_ref, ys_ref, wg_b, wu_b, wd_b):
    v = pl.program_id(0)
    flag = flag_ref[v]

    @pl.when(flag == 3)
    def _():
        ys_ref[...] = jnp.zeros((ETILE, DP), U32)

    @pl.when(new_ref[v] == 1)
    def _():
        wg_b[...] = wg_ref[0, 0].astype(BF16)
        wu_b[...] = wu_ref[0, 0].astype(BF16)
        wd_b[...] = wd_ref[0, 0].astype(BF16)

    @pl.when((flag == 1) | (flag == 2))
    def _():
        x = _unpack(xs_ref[...])
        hid = jax.nn.silu(_dot(x, wg_b[...])) * _dot(x, wu_b[...])
        y = _dot(hid.astype(BF16), wd_b[...])
        yp = _pack(y.astype(BF16).astype(F32))
        row = lax.broadcasted_iota(I32, (ETILE, 1), 0)
        mine = (row >= lo_ref[v]) & (row < hi_ref[v])

        @pl.when(flag == 2)
        def _():
            ys_ref[...] = jnp.where(mine, yp, jnp.uint32(0))

        @pl.when(flag == 1)
        def _():
            ys_ref[...] = jnp.where(mine, yp, ys_ref[...])


def _experts(l, meta, xs, wg, wu, wd):
    def wmap(v, ti, to, e, lo, hi, f, n):
        return (l, e[v], 0, 0)

    grid_spec = pltpu.PrefetchScalarGridSpec(
        num_scalar_prefetch=7,
        grid=(N_VISITS,),
        in_specs=[
            pl.BlockSpec((ETILE, DP), lambda v, ti, to, e, lo, hi, f, n: (ti[v], 0)),
            pl.BlockSpec((1, 1, D, ED), wmap),
            pl.BlockSpec((1, 1, D, ED), wmap),
            pl.BlockSpec((1, 1, ED, D), wmap),
        ],
        out_specs=pl.BlockSpec((ETILE, DP), lambda v, ti, to, e, lo, hi, f, n: (to[v], 0)),
        scratch_shapes=[pltpu.VMEM((D, ED), BF16), pltpu.VMEM((D, ED), BF16),
                        pltpu.VMEM((ED, D), BF16)],
    )
    return pl.pallas_call(
        _expert_kernel,
        grid_spec=grid_spec,
        out_shape=jax.ShapeDtypeStruct((NSLOT_MAX, DP), U32),
        compiler_params=pltpu.CompilerParams(
            dimension_semantics=("arbitrary",), vmem_limit_bytes=VMEM_LIMIT),
        name="experts",
    )(*meta, xs, wg, wu, wd)


def _combine_tile(G, r, slot, gf, final):
    nst = TB // G
    xb = r["xn"][...].astype(BF16)
    hid = jax.nn.silu(_dot(xb, r["sg"][...])) * _dot(xb, r["su"][...])
    moe = _dot(hid.astype(BF16), r["sd"][...])
    pltpu.make_async_copy(r["ys"].at[pl.ds(0, LS)], r["loc"].at[slot], r["sem"].at[slot]).wait()
    lp = r["lpos"][...]
    wts = r["wts"][...]
    for c in range(LS // SUB):
        liota = lax.broadcasted_iota(I32, (TB, SUB), 1) + c * SUB
        p = jnp.where(liota == lp[:, 0:1], wts[:, 0:1], 0.0)
        for k in range(1, TOPK):
            p = p + jnp.where(liota == lp[:, k:k + 1], wts[:, k:k + 1], 0.0)
        moe = moe + _dot(p.astype(BF16), _unpack(r["loc"][slot, c * SUB:(c + 1) * SUB, :]))
    xo = r["x_mid"][...].reshape(nst, G, D) + gf[None] * moe.reshape(nst, G, D)
    r["x_out"][...] = xo.reshape(TB, D)
    if final:
        ms = jnp.mean(xo * xo, axis=-1, keepdims=True)
        r["y_out"][...] = (xo * lax.rsqrt(ms + EPS) * r["fin_g"][...]).reshape(TB, D)


def _combine_kernel(final, *refs):
    names = ["tab", "tab_next", "ys", "x_mid", "xn", "lpos", "wts", "mod", "sg", "su", "sd"]
    names += ["fin_g"] if final else []
    names += ["x_out"] + (["y_out"] if final else []) + ["loc", "sem"]
    r = dict(zip(names, refs))
    b = pl.program_id(0)
    slot = b % 2

    def gather(tab_ref, s):
        _chunk_copies(tab_ref, lambda l, g: pltpu.make_async_copy(
            r["ys"].at[pl.ds(g, ROWS)], r["loc"].at[s, pl.ds(l, ROWS)], r["sem"].at[s]))

    @pl.when(b == 0)
    def _():
        gather(r["tab"], 0)

    @pl.when(b + 1 < NBLK)
    def _():
        gather(r["tab_next"], 1 - slot)

    @pl.when(b < PBLK)
    def _():
        _combine_tile(PB, r, slot, r["mod"][0:PB, 5 * D:6 * D], final)

    @pl.when(b >= PBLK)
    def _():
        _combine_tile(SB, r, slot, r["mod"][PB:PB + SB, 5 * D:6 * D], final)


def _combine(tab_c, ys, x_mid, xn, lpos_c, wts, mod_l, w, fin_g):
    final = fin_g is not None
    nb = PB + SB
    in_specs = [
        pl.BlockSpec((1, 1, NCH), lambda b: (b, 0, 0), memory_space=pltpu.SMEM),
        pl.BlockSpec((1, 1, NCH), lambda b: (jnp.minimum(b + 1, NBLK - 1), 0, 0),
                     memory_space=pltpu.SMEM),
        pl.BlockSpec(memory_space=pl.ANY),
        pl.BlockSpec((TB, D), lambda b: (b, 0)),
        pl.BlockSpec((TB, D), lambda b: (b, 0)),
        pl.BlockSpec((TB, TOPK), lambda b: (b, 0)),
        pl.BlockSpec((TB, TOPK), lambda b: (b, 0)),
        _const_spec((nb, MOD_COLS)),
        _const_spec((D, ED)),
        _const_spec((D, ED)),
        _const_spec((ED, D)),
    ]
    args = [tab_c, tab_c, ys, x_mid, xn, lpos_c, wts, mod_l, w["sg"], w["su"], w["sd"]]
    out_shape = [jax.ShapeDtypeStruct((NT, D), F32)]
    out_specs = [pl.BlockSpec((TB, D), lambda b: (b, 0))]
    if final:
        in_specs.append(_const_spec((1, D)))
        args.append(fin_g)
        out_shape.append(jax.ShapeDtypeStruct((NT, D), F32))
        out_specs.append(pl.BlockSpec((TB, D), lambda b: (b, 0)))
    res = pl.pallas_call(
        functools.partial(_combine_kernel, final),
        grid=(NBLK,),
        in_specs=in_specs,
        out_specs=tuple(out_specs),
        out_shape=tuple(out_shape),
        scratch_shapes=[pltpu.VMEM((2, LS, DP), U32), pltpu.SemaphoreType.DMA((2,))],
        compiler_params=pltpu.CompilerParams(
            dimension_semantics=("arbitrary",), vmem_limit_bytes=VMEM_LIMIT),
        name="combine_final" if final else "combine",
    )(*args)
    return res if final else (res[0], None)


def _plan(cnt):
    c = cnt[:, :, :BPT].transpose(0, 2, 1).reshape(NBLK, NE).astype(I32)
    pc = (c + ROWS - 1) // ROWS * ROWS
    lend = jnp.cumsum(pc, axis=1)
    lstart = lend - pc
    etot = jnp.sum(pc, axis=0)
    eend = jnp.cumsum(etot)
    eoff = eend - etot
    gstart = eoff[None, :] + jnp.cumsum(pc, axis=0) - pc
    total = eend[-1]
    eids = jnp.arange(NE, dtype=I32)

    l8 = jnp.arange(NCH, dtype=I32) * ROWS
    valid = l8[None, :] < lend[:, -1:]
    ej = jnp.minimum(jnp.sum((lend[:, None, :] <= l8[None, :, None]).astype(I32), axis=-1), NE - 1)
    hot = ej[..., None] == eids
    row = jnp.sum(jnp.where(hot, (gstart - lstart)[:, None, :], 0), axis=-1) + l8[None, :]
    nfill = jnp.cumsum((~valid).reshape(-1).astype(I32)).reshape(NBLK, NCH) - 1
    ntail = ((total + ETILE - 1) // ETILE * ETILE - total) // ROWS
    spare = NSLOT_MAX + (jnp.arange(NBLK, dtype=I32) % 2)[:, None] * LS + l8[None, :]
    fill = jnp.where(nfill < ntail, total + ROWS * nfill, spare)
    tab_d = jnp.where(valid, row, fill).reshape(NBLK, 1, NCH)
    tab_c = jnp.where(valid, row, 0).reshape(NBLK, 1, NCH)

    def lookup(table, idx):
        return jnp.sum(jnp.where(idx[..., None] == eids, table, 0), axis=-1)

    first_tile = eoff // ETILE
    last_tile = (eend - 1) // ETILE
    nvis = jnp.where(etot > 0, last_tile - first_tile + 1, 0)
    vend = jnp.cumsum(nvis)
    nvalid = vend[-1]
    v = jnp.arange(N_VISITS, dtype=I32)
    vv = jnp.minimum(v, nvalid - 1)
    e = jnp.minimum(jnp.sum((vend[None, :] <= vv[:, None]).astype(I32), axis=1), NE - 1)
    off_e = lookup(eoff, e)
    tile = lookup(first_tile, e) + (vv - lookup(vend - nvis, e))
    base = tile * ETILE
    lo = jnp.clip(off_e - base, 0, ETILE)
    hi = jnp.clip(lookup(eend, e) - base, 0, ETILE)
    used = (total + ETILE - 1) // ETILE
    spare_tile = used + (v - nvalid)
    tout = jnp.where(v < nvalid, tile, jnp.minimum(spare_tile, N_ETILES - 1))
    flag = jnp.where(v < nvalid, jnp.where(off_e <= base, 2, 1),
                     jnp.where(spare_tile < N_ETILES, 3, 0)).astype(I32)
    new = jnp.concatenate([jnp.ones((1,), I32), (e[1:] != e[:-1]).astype(I32)])
    meta = (tile.astype(I32), tout.astype(I32), e, lo.astype(I32), hi.astype(I32), flag, new)
    return tab_d, tab_c, used.astype(I32).reshape(1), meta


def _block_diag(wh):
    out = jnp.zeros((2, 4, HEAD, 4, HEAD), wh.dtype)
    for hh in range(4):
        out = out.at[:, hh, :, hh, :].set(wh.reshape(2, 4, HEAD, HEAD)[:, hh])
    return out.reshape(2, 4 * HEAD, 4 * HEAD)


def _layer_weights(l, p):
    tril = jnp.tril(jnp.ones((CHUNK, CHUNK), bool))
    wt = jnp.where(tril, p["sgu_w"][l], 0.0)
    wpair = wt.reshape(4, 2, CHUNK, CHUNK).transpose(0, 2, 1, 3).reshape(4, CHUNK, 2 * CHUNK)
    bmap = jnp.repeat(p["sgu_b"][l].T, HEAD, axis=1)
    w8 = jnp.where(tril[:DEC_SEQ, :DEC_SEQ], p["sgu_w"][l][:, :DEC_SEQ, :DEC_SEQ], 0.0)
    coef = jnp.repeat(w8.transpose(2, 1, 0), HEAD, axis=2)
    bias8 = jnp.repeat(p["sgu_b"][l][:, :DEC_SEQ].T, HEAD, axis=1)
    gid = jnp.arange(256) // HEAD
    tok = jnp.arange(TILE)
    same_blk = (tok[:, None] // TB) == (tok[None, :] // TB)
    return dict(
        norm_mix_g=p["norm_mix_g"][l][None], norm_ffn_g=p["norm_ffn_g"][l][None],
        w_in=p["w_in"][l].astype(BF16), w_out=p["w_out"][l].astype(BF16),
        conv_w=p["conv_w"][l], conv_b=p["conv_b"][l][None],
        wr=_block_diag(p["gate_r_w"][l]).astype(BF16), wi=_block_diag(p["gate_i_w"][l]).astype(BF16),
        br=p["gate_r_b"][l][None], bi=p["gate_i_b"][l][None], lam=p["lru_lambda"][l][None],
        sgu_g=p["sgu_norm_g"][l][None], wpair=wpair.astype(BF16), bmap=bmap, coef=coef, bias8=bias8,
        out_g=p["out_norm_g"][l][None],
        bd=(gid[:, None] == gid[None, :]).astype(BF16),
        rw_t=p["router_w"][l].T.astype(BF16), rbias=p["router_bias"][l][:, None],
        tri=(same_blk & (tok[:, None] < tok[None, :])).astype(BF16),
        bsel=((tok[:, None] // TB) == jnp.arange(128)[None, :]).astype(BF16),
        ltri=(jnp.arange(NE)[:, None] > jnp.arange(NE)[None, :]).astype(BF16),
        sg=p["shared_w_gate"][l].astype(BF16), su=p["shared_w_up"][l].astype(BF16),
        sd=p["shared_w_down"][l].astype(BF16),
    )


def kernel(x_prompt, x_sample, c_prompt, c_sample, state_lru_h, state_conv, ada_w, ada_b, norm_mix_g, norm_ffn_g, w_in, conv_w, conv_b, gate_r_w, gate_r_b, gate_i_w, gate_i_b, lru_lambda, sgu_norm_g, sgu_w, sgu_b, out_norm_g, w_out, router_w, router_bias, exp_w_gate, exp_w_up, exp_w_down, shared_w_gate, shared_w_up, shared_w_down, final_norm_g):
    p = dict(norm_mix_g=norm_mix_g, norm_ffn_g=norm_ffn_g, w_in=w_in, conv_w=conv_w, conv_b=conv_b,
             gate_r_w=gate_r_w, gate_r_b=gate_r_b, gate_i_w=gate_i_w, gate_i_b=gate_i_b,
             lru_lambda=lru_lambda, sgu_norm_g=sgu_norm_g, sgu_w=sgu_w, sgu_b=sgu_b,
             out_norm_g=out_norm_g, w_out=w_out, router_w=router_w, router_bias=router_bias,
             shared_w_gate=shared_w_gate, shared_w_up=shared_w_up, shared_w_down=shared_w_down)
    x_all = jnp.concatenate([x_prompt.transpose(1, 0, 2).reshape(NP, D),
                             x_sample.transpose(1, 0, 2).reshape(NS, D)], axis=0)
    mod = _modulations(jnp.concatenate([c_prompt, c_sample], axis=0), ada_w, ada_b)

    hp, cp, hs, cs, vs = [], [], [], [], []
    y_all = None
    for l in range(DEPTH):
        w = _layer_weights(l, p)
        w["h0s"] = state_lru_h[l]
        w["convs"] = state_conv[l].transpose(1, 0, 2).reshape((CONV_W - 1) * SB, LW)
        yn, hlp, cvp, hls, cvs, v_s = _mixer(x_all, mod[l], w)
        x_mid, xn, lpos_t, w_t, cnt = _router(x_all, yn, mod[l], w)
        tab_d, tab_c, used, meta = _plan(cnt)
        lpos_d = lpos_t.reshape(N_TILES, TOPK, BPT, TB).transpose(0, 2, 1, 3).reshape(
            NBLK, TOPK, TB)
        xs = _dispatch(tab_d, used, lpos_d, xn)
        ys = _experts(l, meta, xs, exp_w_gate, exp_w_up, exp_w_down)
        lpos_c = lpos_t.transpose(0, 2, 1).reshape(NT, TOPK)
        wts = w_t.transpose(0, 2, 1).reshape(NT, TOPK)
        fin = final_norm_g[None] if l == DEPTH - 1 else None
        x_all, y_all = _combine(tab_c, ys, x_mid, xn, lpos_c, wts, mod[l], w, fin)
        hp.append(hlp)
        cp.append(cvp.reshape(CONV_W - 1, PB, LW).transpose(1, 0, 2))
        hs.append(hls)
        cs.append(cvs.reshape(CONV_W - 1, SB, LW).transpose(1, 0, 2))
        vs.append(v_s.reshape(DEC_SEQ, SB, SW).transpose(1, 0, 2))

    y_prompt = y_all[:NP].reshape(SEQ, PB, D).transpose(1, 0, 2)
    y_sample = y_all[NP:].reshape(DEC_SEQ, SB, D).transpose(1, 0, 2)
    return (y_prompt, y_sample, jnp.stack(hp), jnp.stack(cp), jnp.stack(hs), jnp.stack(cs),
            jnp.stack(vs))
```

```python
import functools

import jax
import jax.numpy as jnp
from jax import lax
from jax.experimental import pallas as pl
from jax.experimental.pallas import tpu as pltpu

F32 = jnp.float32
BF16 = jnp.bfloat16
I32 = jnp.int32
I16 = jnp.int16

D = 1024
DEPTH = 4
PB = 8
SEQ = 2048
SB = 128
DEC_SEQ = 8
NP = PB * SEQ
NS = SB * DEC_SEQ
NT = NP + NS
TILE = 1024
N_TILES = NT // TILE
P_TILES = NP // TILE
LW = 512
SW = 512
HEAD = 64
CHUNK = 128
CONV_W = 4
LRU_C = 8.0
NE = 64
TOPK = 8
ED = 256
ROUTED_SCALE = 2.5
EPS = 1e-6
MOD_COLS = 6 * D
MOD_BLK = 1536
VMEM_LIMIT = 56 * 1024 * 1024

TB = 256
NBLK = NT // TB
PBLK = NP // TB
BPT = TILE // TB
ROWS = 16
LS = 3072
NCH = LS // ROWS
SUB = 512
ETILE = 512
NSLOT_MAX = -(-(NT * TOPK + NBLK * NE * (ROWS - 1)) // ETILE) * ETILE
N_ETILES = NSLOT_MAX // ETILE
N_VISITS = N_ETILES + NE - 1
XS_ROWS = NSLOT_MAX + 2 * LS
N_XTILES = XS_ROWS // ETILE

assert NSLOT_MAX % ETILE == 0 and LS >= TB * TOPK + NE * (ROWS - 1) and LS % SUB == 0


def _dot(a, b):
    return jnp.dot(a, b, preferred_element_type=F32)


def _const_spec(shape):
    nd = len(shape)
    return pl.BlockSpec(shape, lambda *_: (0,) * nd)


def _mod_kernel(c_ref, w_ref, b_ref, o_ref):
    s = jax.nn.silu(c_ref[...]).astype(BF16)
    o_ref[0] = _dot(s, w_ref[0].astype(BF16)) + b_ref[0]


def _modulations(c_all, ada_w, ada_b):
    nb = c_all.shape[0]
    return pl.pallas_call(
        _mod_kernel,
        grid=(DEPTH, MOD_COLS // MOD_BLK),
        in_specs=[
            pl.BlockSpec((nb, D), lambda l, j: (0, 0)),
            pl.BlockSpec((1, D, MOD_BLK), lambda l, j: (l, 0, j)),
            pl.BlockSpec((1, 1, MOD_BLK), lambda l, j: (l, 0, j)),
        ],
        out_specs=pl.BlockSpec((1, nb, MOD_BLK), lambda l, j: (l, 0, j)),
        out_shape=jax.ShapeDtypeStruct((DEPTH, nb, MOD_COLS), F32),
        compiler_params=pltpu.CompilerParams(
            dimension_semantics=("arbitrary", "arbitrary"), vmem_limit_bytes=VMEM_LIMIT),
        name="adaln_mod",
    )(c_all, ada_w, ada_b.reshape(DEPTH, 1, MOD_COLS))


def _sgu_prompt(v, wpair_ref, bmap_ref, vbuf, sbuf):
    for j in range(4):
        vbuf[j] = v[:, 128 * j:128 * (j + 1)]
    lane = lax.broadcasted_iota(I32, (CHUNK, 128), 1)
    left = lane < HEAD
    for b in range(PB):
        for j in range(4):
            vp = vbuf[j, pl.ds(b, CHUNK, stride=PB), :]
            rhs = jnp.concatenate(
                [jnp.where(left, vp, 0.0), jnp.where(left, 0.0, vp)], axis=0).astype(BF16)
            sj = _dot(wpair_ref[j], rhs) + bmap_ref[:, 128 * j:128 * (j + 1)]
            sbuf[j, pl.ds(b, CHUNK, stride=PB), :] = sj
    return jnp.concatenate([sbuf[j] for j in range(4)], axis=1)


def _sgu_sample(v, coef_ref, bias_ref):
    vt = [v[t * SB:(t + 1) * SB] for t in range(DEC_SEQ)]
    rows = []
    for p in range(DEC_SEQ):
        acc = bias_ref[p:p + 1, :] + coef_ref[0, p:p + 1, :] * vt[0]
        for q in range(1, p + 1):
            acc = acc + coef_ref[q, p:p + 1, :] * vt[q]
        rows.append(acc)
    return jnp.concatenate(rows, axis=0)


def _mixer_tile(G, x, sh, sc, h_in, tail_in, sgu_fn, r, hbuf):
    nst = TILE // G
    x3 = x.reshape(nst, G, D)
    ms = jnp.mean(x3 * x3, axis=-1, keepdims=True)
    xn = x3 * lax.rsqrt(ms + EPS) * r["norm_g"][...]
    xn = (xn * (1.0 + sc[None]) + sh[None]).reshape(TILE, D).astype(BF16)

    xa = _dot(xn, r["w_in"][:, 0:LW])
    xp = jnp.concatenate([tail_in, xa], axis=0)
    cw = r["conv_w"]
    xc = r["conv_b"][...] + cw[0:1, :] * xp[0:TILE]
    for k in range(1, CONV_W):
        xc = xc + cw[k:k + 1, :] * xp[k * G:k * G + TILE]
    new_tail = xa[TILE - (CONV_W - 1) * G:]
    xcb = xc.astype(BF16)
    half = LW // 2
    r_pre = jnp.concatenate(
        [_dot(xcb[:, :half], r["wr"][0]), _dot(xcb[:, half:], r["wr"][1])], axis=1) + r["br"][...]
    i_pre = jnp.concatenate(
        [_dot(xcb[:, :half], r["wi"][0]), _dot(xcb[:, half:], r["wi"][1])], axis=1) + r["bi"][...]
    rg = jax.nn.sigmoid(r_pre)
    ig = jax.nn.sigmoid(i_pre)
    log_a = -LRU_C * rg * jax.nn.softplus(-r["lam"][...])
    a = jnp.exp(log_a)
    th = jnp.tanh(log_a)
    u = jnp.sqrt(-2.0 * th / (1.0 - th)) * (ig * xc)
    h = h_in
    for s in range(nst):
        h = a[s * G:(s + 1) * G] * h + u[s * G:(s + 1) * G]
        hbuf[s * G:(s + 1) * G, :] = h
    ya = hbuf[...] * jax.nn.gelu(_dot(xn, r["w_in"][:, LW:2 * LW]))

    ug = jax.nn.gelu(_dot(xn, r["w_in"][:, 2 * LW:2 * LW + SW]))
    vg = jax.nn.gelu(_dot(xn, r["w_in"][:, 2 * LW + SW:]))
    v = vg * lax.rsqrt(jnp.mean(vg * vg, axis=-1, keepdims=True) + EPS) * r["sgu_g"][...]
    yb = ug * sgu_fn(v)

    y = jnp.concatenate([ya, yb], axis=1)
    bd = r["bd"][...]
    outs = []
    for j in range(D // 256):
        ysl = y[:, 256 * j:256 * (j + 1)]
        sq = ysl * ysl
        hi = sq.astype(BF16)
        lo = (sq - hi.astype(F32)).astype(BF16)
        msq = (_dot(hi, bd) + _dot(lo, bd)) * (1.0 / HEAD)
        outs.append(ysl * lax.rsqrt(msq + EPS) * r["out_g"][:, 256 * j:256 * (j + 1)])
    yn = jnp.concatenate(outs, axis=1).astype(BF16)
    return yn, h, new_tail, v


_MIX_IN = ("x", "mod", "norm_g", "w_in", "conv_w", "conv_b", "wr", "wi", "br", "bi", "lam",
           "sgu_g", "wpair", "bmap", "coef", "bias8", "out_g", "bd", "h0s", "convs")
_MIX_OUT = ("y", "hlast_p", "conv_p", "hlast_s", "conv_s", "v_s")
_MIX_SCR = ("hbuf", "vbuf", "sbuf", "h_carry", "tail_carry")


def _mixer_kernel(*refs):
    names = _MIX_IN + _MIX_OUT + _MIX_SCR
    r = dict(zip(names, refs))
    i = pl.program_id(0)

    @pl.when(i == 0)
    def _():
        r["h_carry"][...] = jnp.zeros((PB, LW), F32)
        r["tail_carry"][...] = jnp.zeros(((CONV_W - 1) * PB, LW), F32)

    @pl.when(i < P_TILES)
    def _():
        sh = r["mod"][0:PB, 0:D]
        sc = r["mod"][0:PB, D:2 * D]
        sgu = functools.partial(_sgu_prompt, wpair_ref=r["wpair"], bmap_ref=r["bmap"],
                                vbuf=r["vbuf"], sbuf=r["sbuf"])
        yn, h, tail, _ = _mixer_tile(PB, r["x"][...], sh, sc, r["h_carry"][...],
                                     r["tail_carry"][...], sgu, r, r["hbuf"])
        r["y"][...] = yn
        r["h_carry"][...] = h
        r["tail_carry"][...] = tail
        r["hlast_p"][...] = h
        r["conv_p"][...] = tail

    @pl.when(i == P_TILES)
    def _():
        sh = r["mod"][PB:PB + SB, 0:D]
        sc = r["mod"][PB:PB + SB, D:2 * D]
        sgu = functools.partial(_sgu_sample, coef_ref=r["coef"], bias_ref=r["bias8"])
        yn, h, tail, v = _mixer_tile(SB, r["x"][...], sh, sc, r["h0s"][...],
                                     r["convs"][...], sgu, r, r["hbuf"])
        r["y"][...] = yn
        r["hlast_s"][...] = h
        r["conv_s"][...] = tail
        r["v_s"][...] = v


def _mixer(x_all, mod_l, w):
    nb = PB + SB
    in_specs = [
        pl.BlockSpec((TILE, D), lambda i: (i, 0)),
        _const_spec((nb, MOD_COLS)),
        _const_spec((1, D)),
        _const_spec((D, 2 * LW + 2 * SW)),
        _const_spec((CONV_W, LW)),
        _const_spec((1, LW)),
        _const_spec((2, LW // 2, LW // 2)),
        _const_spec((2, LW // 2, LW // 2)),
        _const_spec((1, LW)),
        _const_spec((1, LW)),
        _const_spec((1, LW)),
        _const_spec((1, SW)),
        _const_spec((4, CHUNK, 2 * CHUNK)),
        _const_spec((CHUNK, SW)),
        _const_spec((DEC_SEQ, DEC_SEQ, SW)),
        _const_spec((DEC_SEQ, SW)),
        _const_spec((1, D)),
        _const_spec((256, 256)),
        _const_spec((SB, LW)),
        _const_spec(((CONV_W - 1) * SB, LW)),
    ]
    out_shape = (
        jax.ShapeDtypeStruct((NT, D), BF16),
        jax.ShapeDtypeStruct((PB, LW), F32),
        jax.ShapeDtypeStruct(((CONV_W - 1) * PB, LW), F32),
        jax.ShapeDtypeStruct((SB, LW), F32),
        jax.ShapeDtypeStruct(((CONV_W - 1) * SB, LW), F32),
        jax.ShapeDtypeStruct((NS, SW), F32),
    )
    out_specs = (
        pl.BlockSpec((TILE, D), lambda i: (i, 0)),
        _const_spec((PB, LW)),
        _const_spec(((CONV_W - 1) * PB, LW)),
        _const_spec((SB, LW)),
        _const_spec(((CONV_W - 1) * SB, LW)),
        _const_spec((NS, SW)),
    )
    scratch = [
        pltpu.VMEM((TILE, LW), F32),
        pltpu.VMEM((4, TILE, 128), F32),
        pltpu.VMEM((4, TILE, 128), F32),
        pltpu.VMEM((PB, LW), F32),
        pltpu.VMEM(((CONV_W - 1) * PB, LW), F32),
    ]
    return pl.pallas_call(
        _mixer_kernel,
        grid=(N_TILES,),
        in_specs=in_specs,
        out_specs=out_specs,
        out_shape=out_shape,
        scratch_shapes=scratch,
        compiler_params=pltpu.CompilerParams(
            dimension_semantics=("arbitrary",), vmem_limit_bytes=VMEM_LIMIT),
        name="mixer",
    )(x_all, mod_l, w["norm_mix_g"], w["w_in"], w["conv_w"], w["conv_b"], w["wr"], w["wi"],
      w["br"], w["bi"], w["lam"], w["sgu_g"], w["wpair"], w["bmap"], w["coef"], w["bias8"],
      w["out_g"], w["bd"], w["h0s"], w["convs"])


def _router_tile(G, r, gm, sc, sh):
    nst = TILE // G
    out = _dot(r["y"][...], r["w_out"][...])
    xm3 = r["x"][...].reshape(nst, G, D) + gm[None] * out.reshape(nst, G, D)
    r["x_mid"][...] = xm3.reshape(TILE, D)
    ms = jnp.mean(xm3 * xm3, axis=-1, keepdims=True)
    xn3 = xm3 * lax.rsqrt(ms + EPS) * r["norm_g"][...]
    xn = (xn3 * (1.0 + sc[None]) + sh[None]).reshape(TILE, D)
    r["xn"][...] = xn

    logits = lax.dot_general(r["rw_t"][...], xn.astype(BF16), (((1,), (1,)), ((), ())),
                             preferred_element_type=F32)
    sig = jax.nn.sigmoid(logits)
    sel = sig + r["rbias"][...]
    eiota = lax.broadcasted_iota(I32, (NE, TILE), 0)
    chosen = jnp.zeros((NE, TILE), F32)
    w_rows, hots = [], []
    for _ in range(TOPK):
        m = jnp.max(sel, axis=0, keepdims=True)
        idx = jnp.min(jnp.where(sel == m, eiota, NE), axis=0, keepdims=True)
        hot = eiota == idx
        hots.append(hot)
        w_rows.append(jnp.sum(jnp.where(hot, sig, 0.0), axis=0, keepdims=True))
        chosen = jnp.where(hot, 1.0, chosen)
        sel = jnp.where(hot, -jnp.inf, sel)
    den = w_rows[0]
    for k in range(1, TOPK):
        den = den + w_rows[k]
    scale = ROUTED_SCALE / den
    r["w_t"][0] = jnp.concatenate([wk * scale for wk in w_rows], axis=0)

    cb = chosen.astype(BF16)
    rank = _dot(cb, r["tri"][...])
    cnt = _dot(cb, r["bsel"][...])
    groups = jnp.floor((cnt + (ROWS - 1.0)) * (1.0 / ROWS))
    start = ROWS * _dot(r["ltri"][...], groups.astype(BF16))
    pos = rank + jnp.concatenate(
        [jnp.broadcast_to(start[:, j:j + 1], (NE, TB)) for j in range(BPT)], axis=1)
    lp_rows = [jnp.sum(jnp.where(hot, pos, 0.0), axis=0, keepdims=True) for hot in hots]
    r["lpos_t"][0] = jnp.concatenate(lp_rows, axis=0).astype(I32)
    r["cnt"][0] = cnt


_RT_IN = ("x", "y", "mod", "w_out", "norm_g", "rw_t", "rbias", "tri", "bsel", "ltri")
_RT_OUT = ("x_mid", "xn", "lpos_t", "w_t", "cnt")


def _router_kernel(*refs):
    r = dict(zip(_RT_IN + _RT_OUT, refs))
    i = pl.program_id(0)

    def mods(lo, n):
        m = r["mod"]
        return m[lo:lo + n, 2 * D:3 * D], m[lo:lo + n, 4 * D:5 * D], m[lo:lo + n, 3 * D:4 * D]

    @pl.when(i < P_TILES)
    def _():
        _router_tile(PB, r, *mods(0, PB))

    @pl.when(i == P_TILES)
    def _():
        _router_tile(SB, r, *mods(PB, SB))


def _router(x_all, y_all, mod_l, w):
    nb = PB + SB
    in_specs = [
        pl.BlockSpec((TILE, D), lambda i: (i, 0)),
        pl.BlockSpec((TILE, D), lambda i: (i, 0)),
        _const_spec((nb, MOD_COLS)),
        _const_spec((D, D)),
        _const_spec((1, D)),
        _const_spec((NE, D)),
        _const_spec((NE, 1)),
        _const_spec((TILE, TILE)),
        _const_spec((TILE, 128)),
        _const_spec((NE, NE)),
    ]
    out_shape = (
        jax.ShapeDtypeStruct((NT, D), F32),
        jax.ShapeDtypeStruct((NT, D), F32),
        jax.ShapeDtypeStruct((N_TILES, TOPK, TILE), I32),
        jax.ShapeDtypeStruct((N_TILES, TOPK, TILE), F32),
        jax.ShapeDtypeStruct((N_TILES, NE, 128), F32),
    )
    out_specs = (
        pl.BlockSpec((TILE, D), lambda i: (i, 0)),
        pl.BlockSpec((TILE, D), lambda i: (i, 0)),
        pl.BlockSpec((1, TOPK, TILE), lambda i: (i, 0, 0)),
        pl.BlockSpec((1, TOPK, TILE), lambda i: (i, 0, 0)),
        pl.BlockSpec((1, NE, 128), lambda i: (i, 0, 0)),
    )
    return pl.pallas_call(
        _router_kernel,
        grid=(N_TILES,),
        in_specs=in_specs,
        out_specs=out_specs,
        out_shape=out_shape,
        compiler_params=pltpu.CompilerParams(
            dimension_semantics=("arbitrary",), vmem_limit_bytes=VMEM_LIMIT),
        name="router",
    )(x_all, y_all, mod_l, w["w_out"], w["norm_ffn_g"], w["rw_t"], w["rbias"], w["tri"],
      w["bsel"], w["ltri"])


def _chunk_copies(tab_ref, make_copy):
    def body(i, carry):
        for u in range(2):
            j = 2 * i + u
            g = pl.multiple_of(tab_ref[0, 0, j], ROWS)
            make_copy(pl.multiple_of(j * ROWS, ROWS), g).start(priority=u)
        return carry

    lax.fori_loop(0, NCH // 2, body, 0)


def _dispatch_kernel(tab_ref, fill_ref, lpos_ref, xn_ref, xs_ref, loc, zbuf, sem):
    b = pl.program_id(0)
    slot = b % 2
    xb = xn_ref[...].astype(BF16)
    lp = lpos_ref[0].astype(I16)
    one = jnp.ones((), BF16)
    for c in range(LS // SUB):
        siota = (lax.broadcasted_iota(I32, (SUB, TB), 0) + c * SUB).astype(I16)
        p = jnp.zeros((SUB, TB), BF16)
        for k in range(TOPK):
            p = jnp.where(siota == lp[k:k + 1, :], one, p)
        loc[slot, c * SUB:(c + 1) * SUB, :] = _dot(p, xb).astype(BF16)

    _chunk_copies(tab_ref, lambda l, g: pltpu.make_async_copy(
        loc.at[slot, pl.ds(l, ROWS)], xs_ref.at[pl.ds(g, ROWS)], sem.at[slot]))

    def drain(s):
        pltpu.make_async_copy(loc.at[s], xs_ref.at[pl.ds(0, LS)], sem.at[s]).wait()

    @pl.when(b > 0)
    def _():
        drain(1 - slot)

    @pl.when(b == NBLK - 1)
    def _():
        drain(slot)
        zbuf[...] = jnp.zeros((ETILE, D), BF16)

        def zero_tile(t):
            return pltpu.make_async_copy(
                zbuf, xs_ref.at[pl.ds(pl.multiple_of(t * ETILE, ETILE), ETILE)], sem.at[0])

        def start(t, carry):
            zero_tile(t).start()
            return carry

        def wait(t, carry):
            zero_tile(t).wait()
            return carry

        lax.fori_loop(fill_ref[0], N_XTILES, start, 0)
        lax.fori_loop(fill_ref[0], N_XTILES, wait, 0)


def _dispatch(tab_d, fill_from, lpos_d, xn):
    return pl.pallas_call(
        _dispatch_kernel,
        grid=(NBLK,),
        in_specs=[
            pl.BlockSpec((1, 1, NCH), lambda b: (b, 0, 0), memory_space=pltpu.SMEM),
            pl.BlockSpec(memory_space=pltpu.SMEM),
            pl.BlockSpec((1, TOPK, TB), lambda b: (b, 0, 0)),
            pl.BlockSpec((TB, D), lambda b: (b, 0)),
        ],
        out_specs=pl.BlockSpec(memory_space=pl.ANY),
        out_shape=jax.ShapeDtypeStruct((XS_ROWS, D), BF16),
        scratch_shapes=[pltpu.VMEM((2, LS, D), BF16), pltpu.VMEM((ETILE, D), BF16),
                        pltpu.SemaphoreType.DMA((2,))],
        compiler_params=pltpu.CompilerParams(
            dimension_semantics=("arbitrary",), vmem_limit_bytes=VMEM_LIMIT),
        name="dispatch",
    )(tab_d, fill_from, lpos_d, xn)


def _expert_kernel(tin_ref, tout_ref, exp_ref, lo_ref, hi_ref, flag_ref, new_ref, xs_ref, wg_ref,
                   wu_ref, wd_ref, ys_ref, wg_b, wu_b, wd_b):
    v = pl.program_id(0)
    flag = flag_ref[v]

    @pl.when(flag >= 2)
    def _():
        ys_ref[...] = jnp.zeros((ETILE, D), BF16)

    @pl.when(new_ref[v] == 1)
    def _():
        wg_b[...] = wg_ref[0, 0].astype(BF16)
        wu_b[...] = wu_ref[0, 0].astype(BF16)
        wd_b[...] = wd_ref[0, 0].astype(BF16)

    @pl.when((flag == 1) | (flag == 2))
    def _():
        x = xs_ref[...]
        hid = jax.nn.silu(_dot(x, wg_b[...])) * _dot(x, wu_b[...])
        y = _dot(hid.astype(BF16), wd_b[...]).astype(BF16)
        row = lax.broadcasted_iota(I32, (ETILE, 1), 0)
        mine = (row >= lo_ref[v]) & (row < hi_ref[v])
        ys_ref[...] = jnp.where(mine, y, ys_ref[...])


def _experts(l, meta, xs, wg, wu, wd):
    def wmap(v, ti, to, e, lo, hi, f, n):
        return (l, e[v], 0, 0)

    grid_spec = pltpu.PrefetchScalarGridSpec(
        num_scalar_prefetch=7,
        grid=(N_VISITS,),
        in_specs=[
            pl.BlockSpec((ETILE, D), lambda v, ti, to, e, lo, hi, f, n: (ti[v], 0)),
            pl.BlockSpec((1, 1, D, ED), wmap),
            pl.BlockSpec((1, 1, D, ED), wmap),
            pl.BlockSpec((1, 1, ED, D), wmap),
        ],
        out_specs=pl.BlockSpec((ETILE, D), lambda v, ti, to, e, lo, hi, f, n: (to[v], 0)),
        scratch_shapes=[pltpu.VMEM((D, ED), BF16), pltpu.VMEM((D, ED), BF16),
                        pltpu.VMEM((ED, D), BF16)],
    )
    return pl.pallas_call(
        _expert_kernel,
        grid_spec=grid_spec,
        out_shape=jax.ShapeDtypeStruct((NSLOT_MAX, D), BF16),
        compiler_params=pltpu.CompilerParams(
            dimension_semantics=("arbitrary",), vmem_limit_bytes=VMEM_LIMIT),
        name="experts",
    )(*meta, xs, wg, wu, wd)


def _combine_tile(G, r, slot, gf, final):
    nst = TB // G
    xb = r["xn"][...].astype(BF16)
    hid = jax.nn.silu(_dot(xb, r["sg"][...])) * _dot(xb, r["su"][...])
    moe = _dot(hid.astype(BF16), r["sd"][...])
    pltpu.make_async_copy(r["ys"].at[pl.ds(0, LS)], r["loc"].at[slot], r["sem"].at[slot]).wait()
    lp = r["lpos"][0].astype(I16)
    wts = r["wts"][0].astype(BF16)
    for c in range(LS // SUB):
        siota = (lax.broadcasted_iota(I32, (SUB, TB), 0) + c * SUB).astype(I16)
        p = jnp.zeros((SUB, TB), BF16)
        for k in range(TOPK):
            p = jnp.where(siota == lp[k:k + 1, :], wts[k:k + 1, :], p)
        moe = moe + lax.dot_general(
            p, r["loc"][slot, c * SUB:(c + 1) * SUB, :], (((0,), (0,)), ((), ())),
            preferred_element_type=F32)
    xo = r["x_mid"][...].reshape(nst, G, D) + gf[None] * moe.reshape(nst, G, D)
    r["x_out"][...] = xo.reshape(TB, D)
    if final:
        ms = jnp.mean(xo * xo, axis=-1, keepdims=True)
        r["y_out"][...] = (xo * lax.rsqrt(ms + EPS) * r["fin_g"][...]).reshape(TB, D)


def _combine_kernel(final, *refs):
    names = ["tab", "tab_next", "ys", "x_mid", "xn", "lpos", "wts", "mod", "sg", "su", "sd"]
    names += ["fin_g"] if final else []
    names += ["x_out"] + (["y_out"] if final else []) + ["loc", "sem"]
    r = dict(zip(names, refs))
    b = pl.program_id(0)
    slot = b % 2

    def gather(tab_ref, s):
        _chunk_copies(tab_ref, lambda l, g: pltpu.make_async_copy(
            r["ys"].at[pl.ds(g, ROWS)], r["loc"].at[s, pl.ds(l, ROWS)], r["sem"].at[s]))

    @pl.when(b == 0)
    def _():
        gather(r["tab"], 0)

    @pl.when(b + 1 < NBLK)
    def _():
        gather(r["tab_next"], 1 - slot)

    @pl.when(b < PBLK)
    def _():
        _combine_tile(PB, r, slot, r["mod"][0:PB, 5 * D:6 * D], final)

    @pl.when(b >= PBLK)
    def _():
        _combine_tile(SB, r, slot, r["mod"][PB:PB + SB, 5 * D:6 * D], final)


def _combine(tab_c, ys, x_mid, xn, lpos_c, wts, mod_l, w, fin_g):
    final = fin_g is not None
    nb = PB + SB
    in_specs = [
        pl.BlockSpec((1, 1, NCH), lambda b: (b, 0, 0), memory_space=pltpu.SMEM),
        pl.BlockSpec((1, 1, NCH), lambda b: (jnp.minimum(b + 1, NBLK - 1), 0, 0),
                     memory_space=pltpu.SMEM),
        pl.BlockSpec(memory_space=pl.ANY),
        pl.BlockSpec((TB, D), lambda b: (b, 0)),
        pl.BlockSpec((TB, D), lambda b: (b, 0)),
        pl.BlockSpec((1, TOPK, TB), lambda b: (b, 0, 0)),
        pl.BlockSpec((1, TOPK, TB), lambda b: (b, 0, 0)),
        _const_spec((nb, MOD_COLS)),
        _const_spec((D, ED)),
        _const_spec((D, ED)),
        _const_spec((ED, D)),
    ]
    args = [tab_c, tab_c, ys, x_mid, xn, lpos_c, wts, mod_l, w["sg"], w["su"], w["sd"]]
    out_shape = [jax.ShapeDtypeStruct((NT, D), F32)]
    out_specs = [pl.BlockSpec((TB, D), lambda b: (b, 0))]
    if final:
        in_specs.append(_const_spec((1, D)))
        args.append(fin_g)
        out_shape.append(jax.ShapeDtypeStruct((NT, D), F32))
        out_specs.append(pl.BlockSpec((TB, D), lambda b: (b, 0)))
    res = pl.pallas_call(
        functools.partial(_combine_kernel, final),
        grid=(NBLK,),
        in_specs=in_specs,
        out_specs=tuple(out_specs),
        out_shape=tuple(out_shape),
        scratch_shapes=[pltpu.VMEM((2, LS, D), BF16), pltpu.SemaphoreType.DMA((2,))],
        compiler_params=pltpu.CompilerParams(
            dimension_semantics=("arbitrary",), vmem_limit_bytes=VMEM_LIMIT),
        name="combine_final" if final else "combine",
    )(*args)
    return res if final else (res[0], None)


def _plan(cnt):
    c = cnt[:, :, :BPT].transpose(0, 2, 1).reshape(NBLK, NE).astype(I32)
    pc = (c + ROWS - 1) // ROWS * ROWS
    lend = jnp.cumsum(pc, axis=1)
    lstart = lend - pc
    etot = jnp.sum(pc, axis=0)
    eend = jnp.cumsum(etot)
    eoff = eend - etot
    gstart = eoff[None, :] + jnp.cumsum(pc, axis=0) - pc
    total = eend[-1]
    eids = jnp.arange(NE, dtype=I32)

    l8 = jnp.arange(NCH, dtype=I32) * ROWS
    valid = l8[None, :] < lend[:, -1:]
    ej = jnp.minimum(jnp.sum((lend[:, None, :] <= l8[None, :, None]).astype(I32), axis=-1), NE - 1)
    hot = ej[..., None] == eids
    row = jnp.sum(jnp.where(hot, (gstart - lstart)[:, None, :], 0), axis=-1) + l8[None, :]
    nfill = jnp.cumsum((~valid).reshape(-1).astype(I32)).reshape(NBLK, NCH) - 1
    ntail = ((total + ETILE - 1) // ETILE * ETILE - total) // ROWS
    spare = NSLOT_MAX + (jnp.arange(NBLK, dtype=I32) % 2)[:, None] * LS + l8[None, :]
    fill = jnp.where(nfill < ntail, total + ROWS * nfill, spare)
    tab_d = jnp.where(valid, row, fill).reshape(NBLK, 1, NCH)
    tab_c = jnp.where(valid, row, 0).reshape(NBLK, 1, NCH)

    def lookup(table, idx):
        return jnp.sum(jnp.where(idx[..., None] == eids, table, 0), axis=-1)

    first_tile = eoff // ETILE
    last_tile = (eend - 1) // ETILE
    nvis = jnp.where(etot > 0, last_tile - first_tile + 1, 0)
    vend = jnp.cumsum(nvis)
    nvalid = vend[-1]
    v = jnp.arange(N_VISITS, dtype=I32)
    vv = jnp.minimum(v, nvalid - 1)
    e = jnp.minimum(jnp.sum((vend[None, :] <= vv[:, None]).astype(I32), axis=1), NE - 1)
    off_e = lookup(eoff, e)
    tile = lookup(first_tile, e) + (vv - lookup(vend - nvis, e))
    base = tile * ETILE
    lo = jnp.clip(off_e - base, 0, ETILE)
    hi = jnp.clip(lookup(eend, e) - base, 0, ETILE)
    used = (total + ETILE - 1) // ETILE
    spare_tile = used + (v - nvalid)
    tout = jnp.where(v < nvalid, tile, jnp.minimum(spare_tile, N_ETILES - 1))
    flag = jnp.where(v < nvalid, jnp.where(off_e <= base, 2, 1),
                     jnp.where(spare_tile < N_ETILES, 3, 0)).astype(I32)
    new = jnp.concatenate([jnp.ones((1,), I32), (e[1:] != e[:-1]).astype(I32)])
    meta = (tile.astype(I32), tout.astype(I32), e, lo.astype(I32), hi.astype(I32), flag, new)
    return tab_d, tab_c, used.astype(I32).reshape(1), meta


def _block_diag(wh):
    out = jnp.zeros((2, 4, HEAD, 4, HEAD), wh.dtype)
    for hh in range(4):
        out = out.at[:, hh, :, hh, :].set(wh.reshape(2, 4, HEAD, HEAD)[:, hh])
    return out.reshape(2, 4 * HEAD, 4 * HEAD)


def _layer_weights(l, p):
    tril = jnp.tril(jnp.ones((CHUNK, CHUNK), bool))
    wt = jnp.where(tril, p["sgu_w"][l], 0.0)
    wpair = wt.reshape(4, 2, CHUNK, CHUNK).transpose(0, 2, 1, 3).reshape(4, CHUNK, 2 * CHUNK)
    bmap = jnp.repeat(p["sgu_b"][l].T, HEAD, axis=1)
    w8 = jnp.where(tril[:DEC_SEQ, :DEC_SEQ], p["sgu_w"][l][:, :DEC_SEQ, :DEC_SEQ], 0.0)
    coef = jnp.repeat(w8.transpose(2, 1, 0), HEAD, axis=2)
    bias8 = jnp.repeat(p["sgu_b"][l][:, :DEC_SEQ].T, HEAD, axis=1)
    gid = jnp.arange(256) // HEAD
    tok = jnp.arange(TILE)
    same_blk = (tok[:, None] // TB) == (tok[None, :] // TB)
    return dict(
        norm_mix_g=p["norm_mix_g"][l][None], norm_ffn_g=p["norm_ffn_g"][l][None],
        w_in=p["w_in"][l].astype(BF16), w_out=p["w_out"][l].astype(BF16),
        conv_w=p["conv_w"][l], conv_b=p["conv_b"][l][None],
        wr=_block_diag(p["gate_r_w"][l]).astype(BF16), wi=_block_diag(p["gate_i_w"][l]).astype(BF16),
        br=p["gate_r_b"][l][None], bi=p["gate_i_b"][l][None], lam=p["lru_lambda"][l][None],
        sgu_g=p["sgu_norm_g"][l][None], wpair=wpair.astype(BF16), bmap=bmap, coef=coef, bias8=bias8,
        out_g=p["out_norm_g"][l][None],
        bd=(gid[:, None] == gid[None, :]).astype(BF16),
        rw_t=p["router_w"][l].T.astype(BF16), rbias=p["router_bias"][l][:, None],
        tri=(same_blk & (tok[:, None] < tok[None, :])).astype(BF16),
        bsel=((tok[:, None] // TB) == jnp.arange(128)[None, :]).astype(BF16),
        ltri=(jnp.arange(NE)[:, None] > jnp.arange(NE)[None, :]).astype(BF16),
        sg=p["shared_w_gate"][l].astype(BF16), su=p["shared_w_up"][l].astype(BF16),
        sd=p["shared_w_down"][l].astype(BF16),
    )


def kernel(x_prompt, x_sample, c_prompt, c_sample, state_lru_h, state_conv, ada_w, ada_b, norm_mix_g, norm_ffn_g, w_in, conv_w, conv_b, gate_r_w, gate_r_b, gate_i_w, gate_i_b, lru_lambda, sgu_norm_g, sgu_w, sgu_b, out_norm_g, w_out, router_w, router_bias, exp_w_gate, exp_w_up, exp_w_down, shared_w_gate, shared_w_up, shared_w_down, final_norm_g):
    p = dict(norm_mix_g=norm_mix_g, norm_ffn_g=norm_ffn_g, w_in=w_in, conv_w=conv_w, conv_b=conv_b,
             gate_r_w=gate_r_w, gate_r_b=gate_r_b, gate_i_w=gate_i_w, gate_i_b=gate_i_b,
             lru_lambda=lru_lambda, sgu_norm_g=sgu_norm_g, sgu_w=sgu_w, sgu_b=sgu_b,
             out_norm_g=out_norm_g, w_out=w_out, router_w=router_w, router_bias=router_bias,
             shared_w_gate=shared_w_gate, shared_w_up=shared_w_up, shared_w_down=shared_w_down)
    x_all = jnp.concatenate([x_prompt.transpose(1, 0, 2).reshape(NP, D),
                             x_sample.transpose(1, 0, 2).reshape(NS, D)], axis=0)
    mod = _modulations(jnp.concatenate([c_prompt, c_sample], axis=0), ada_w, ada_b)

    hp, cp, hs, cs, vs = [], [], [], [], []
    y_all = None
    for l in range(DEPTH):
        w = _layer_weights(l, p)
        w["h0s"] = state_lru_h[l]
        w["convs"] = state_conv[l].transpose(1, 0, 2).reshape((CONV_W - 1) * SB, LW)
        yn, hlp, cvp, hls, cvs, v_s = _mixer(x_all, mod[l], w)
        x_mid, xn, lpos_t, w_t, cnt = _router(x_all, yn, mod[l], w)
        tab_d, tab_c, used, meta = _plan(cnt)
        def per_block(a):
            return a.reshape(N_TILES, TOPK, BPT, TB).transpose(0, 2, 1, 3).reshape(NBLK, TOPK, TB)

        lpos_d = per_block(lpos_t)
        xs = _dispatch(tab_d, used, lpos_d, xn)
        ys = _experts(l, meta, xs, exp_w_gate, exp_w_up, exp_w_down)
        fin = final_norm_g[None] if l == DEPTH - 1 else None
        x_all, y_all = _combine(tab_c, ys, x_mid, xn, lpos_d, per_block(w_t), mod[l], w, fin)
        hp.append(hlp)
        cp.append(cvp.reshape(CONV_W - 1, PB, LW).transpose(1, 0, 2))
        hs.append(hls)
        cs.append(cvs.reshape(CONV_W - 1, SB, LW).transpose(1, 0, 2))
        vs.append(v_s.reshape(DEC_SEQ, SB, SW).transpose(1, 0, 2))

    y_prompt = y_all[:NP].reshape(SEQ, PB, D).transpose(1, 0, 2)
    y_sample = y_all[NP:].reshape(DEC_SEQ, SB, D).transpose(1, 0, 2)
    return (y_prompt, y_sample, jnp.stack(hp), jnp.stack(cp), jnp.stack(hs), jnp.stack(cs),
            jnp.stack(vs))
```

```python
import functools

import jax
import jax.numpy as jnp
from jax import lax
from jax.experimental import pallas as pl
from jax.experimental.pallas import tpu as pltpu

F32 = jnp.float32
BF16 = jnp.bfloat16
I32 = jnp.int32
I16 = jnp.int16

D = 1024
DEPTH = 4
PB = 8
SEQ = 2048
SB = 128
DEC_SEQ = 8
NP = PB * SEQ
NS = SB * DEC_SEQ
NT = NP + NS
TILE = 1024
N_TILES = NT // TILE
P_TILES = NP // TILE
LW = 512
SW = 512
HEAD = 64
CHUNK = 128
CONV_W = 4
LRU_C = 8.0
NE = 64
TOPK = 8
ED = 256
ROUTED_SCALE = 2.5
EPS = 1e-6
MOD_COLS = 6 * D
MOD_BLK = 1536
VMEM_LIMIT = 56 * 1024 * 1024

TB = 256
NBLK = NT // TB
PBLK = NP // TB
BPT = TILE // TB
ROWS = 16
LS = 3072
NCH = LS // ROWS
NCH_MIN = TB * TOPK // ROWS
SUB = 512
LAST = LS // SUB - 1
ETILE = 512
NSLOT_MAX = -(-(NT * TOPK + NBLK * NE * (ROWS - 1)) // ETILE) * ETILE
N_ETILES = NSLOT_MAX // ETILE
N_VISITS = N_ETILES + NE - 1
XS_ROWS = NSLOT_MAX + 2 * LS
N_XTILES = XS_ROWS // ETILE

assert NSLOT_MAX % ETILE == 0 and LS >= TB * TOPK + NE * (ROWS - 1) and LS % SUB == 0


def _dot(a, b):
    return jnp.dot(a, b, preferred_element_type=F32)


def _const_spec(shape):
    nd = len(shape)
    return pl.BlockSpec(shape, lambda *_: (0,) * nd)


def _mod_kernel(c_ref, w_ref, b_ref, o_ref):
    s = jax.nn.silu(c_ref[...]).astype(BF16)
    o_ref[0] = _dot(s, w_ref[0].astype(BF16)) + b_ref[0]


def _modulations(c_all, ada_w, ada_b):
    nb = c_all.shape[0]
    return pl.pallas_call(
        _mod_kernel,
        grid=(DEPTH, MOD_COLS // MOD_BLK),
        in_specs=[
            pl.BlockSpec((nb, D), lambda l, j: (0, 0)),
            pl.BlockSpec((1, D, MOD_BLK), lambda l, j: (l, 0, j)),
            pl.BlockSpec((1, 1, MOD_BLK), lambda l, j: (l, 0, j)),
        ],
        out_specs=pl.BlockSpec((1, nb, MOD_BLK), lambda l, j: (l, 0, j)),
        out_shape=jax.ShapeDtypeStruct((DEPTH, nb, MOD_COLS), F32),
        compiler_params=pltpu.CompilerParams(
            dimension_semantics=("arbitrary", "arbitrary"), vmem_limit_bytes=VMEM_LIMIT),
        name="adaln_mod",
    )(c_all, ada_w, ada_b.reshape(DEPTH, 1, MOD_COLS))


def _sgu_prompt(v, wpair_ref, bmap_ref, vbuf, sbuf):
    for j in range(4):
        vbuf[j] = v[:, 128 * j:128 * (j + 1)]
    lane = lax.broadcasted_iota(I32, (CHUNK, 128), 1)
    left = lane < HEAD
    for b in range(PB):
        for j in range(4):
            vp = vbuf[j, pl.ds(b, CHUNK, stride=PB), :]
            rhs = jnp.concatenate(
                [jnp.where(left, vp, 0.0), jnp.where(left, 0.0, vp)], axis=0).astype(BF16)
            sj = _dot(wpair_ref[j], rhs) + bmap_ref[:, 128 * j:128 * (j + 1)]
            sbuf[j, pl.ds(b, CHUNK, stride=PB), :] = sj
    return jnp.concatenate([sbuf[j] for j in range(4)], axis=1)


def _sgu_sample(v, coef_ref, bias_ref):
    vt = [v[t * SB:(t + 1) * SB] for t in range(DEC_SEQ)]
    rows = []
    for p in range(DEC_SEQ):
        acc = bias_ref[p:p + 1, :] + coef_ref[0, p:p + 1, :] * vt[0]
        for q in range(1, p + 1):
            acc = acc + coef_ref[q, p:p + 1, :] * vt[q]
        rows.append(acc)
    return jnp.concatenate(rows, axis=0)


def _mixer_tile(G, x, sh, sc, h_in, tail_in, sgu_fn, r, hbuf):
    nst = TILE // G
    x3 = x.reshape(nst, G, D)
    ms = jnp.mean(x3 * x3, axis=-1, keepdims=True)
    xn = x3 * lax.rsqrt(ms + EPS) * r["norm_g"][...]
    xn = (xn * (1.0 + sc[None]) + sh[None]).reshape(TILE, D).astype(BF16)

    xa = _dot(xn, r["w_in"][:, 0:LW])
    xp = jnp.concatenate([tail_in, xa], axis=0)
    cw = r["conv_w"]
    xc = r["conv_b"][...] + cw[0:1, :] * xp[0:TILE]
    for k in range(1, CONV_W):
        xc = xc + cw[k:k + 1, :] * xp[k * G:k * G + TILE]
    new_tail = xa[TILE - (CONV_W - 1) * G:]
    xcb = xc.astype(BF16)
    half = LW // 2
    r_pre = jnp.concatenate(
        [_dot(xcb[:, :half], r["wr"][0]), _dot(xcb[:, half:], r["wr"][1])], axis=1) + r["br"][...]
    i_pre = jnp.concatenate(
        [_dot(xcb[:, :half], r["wi"][0]), _dot(xcb[:, half:], r["wi"][1])], axis=1) + r["bi"][...]
    rg = jax.nn.sigmoid(r_pre)
    ig = jax.nn.sigmoid(i_pre)
    log_a = -LRU_C * rg * jax.nn.softplus(-r["lam"][...])
    a = jnp.exp(log_a)
    th = jnp.tanh(log_a)
    num = -2.0 * th
    mult = jnp.where(num > 0.0, num * lax.rsqrt(num * (1.0 - th)), 0.0)
    u = mult * (ig * xc)
    h = h_in
    for s in range(nst):
        h = a[s * G:(s + 1) * G] * h + u[s * G:(s + 1) * G]
        hbuf[s * G:(s + 1) * G, :] = h
    ya = hbuf[...] * jax.nn.gelu(_dot(xn, r["w_in"][:, LW:2 * LW]))

    ug = jax.nn.gelu(_dot(xn, r["w_in"][:, 2 * LW:2 * LW + SW]))
    vg = jax.nn.gelu(_dot(xn, r["w_in"][:, 2 * LW + SW:]))
    v = vg * lax.rsqrt(jnp.mean(vg * vg, axis=-1, keepdims=True) + EPS) * r["sgu_g"][...]
    yb = ug * sgu_fn(v)

    y = jnp.concatenate([ya, yb], axis=1)
    bd = r["bd"][...]
    outs = []
    for j in range(D // 256):
        ysl = y[:, 256 * j:256 * (j + 1)]
        sq = ysl * ysl
        hi = sq.astype(BF16)
        lo = (sq - hi.astype(F32)).astype(BF16)
        msq = (_dot(hi, bd) + _dot(lo, bd)) * (1.0 / HEAD)
        outs.append(ysl * lax.rsqrt(msq + EPS) * r["out_g"][:, 256 * j:256 * (j + 1)])
    yn = jnp.concatenate(outs, axis=1).astype(BF16)
    return yn, h, new_tail, v


_MIX_IN = ("x", "mod", "norm_g", "w_in", "conv_w", "conv_b", "wr", "wi", "br", "bi", "lam",
           "sgu_g", "wpair", "bmap", "coef", "bias8", "out_g", "bd", "h0s", "convs")
_MIX_OUT = ("y", "hlast_p", "conv_p", "hlast_s", "conv_s", "v_s")
_MIX_SCR = ("hbuf", "vbuf", "sbuf", "h_carry", "tail_carry")


def _mixer_kernel(*refs):
    names = _MIX_IN + _MIX_OUT + _MIX_SCR
    r = dict(zip(names, refs))
    i = pl.program_id(0)

    @pl.when(i == 0)
    def _():
        r["h_carry"][...] = jnp.zeros((PB, LW), F32)
        r["tail_carry"][...] = jnp.zeros(((CONV_W - 1) * PB, LW), F32)

    @pl.when(i < P_TILES)
    def _():
        sh = r["mod"][0:PB, 0:D]
        sc = r["mod"][0:PB, D:2 * D]
        sgu = functools.partial(_sgu_prompt, wpair_ref=r["wpair"], bmap_ref=r["bmap"],
                                vbuf=r["vbuf"], sbuf=r["sbuf"])
        yn, h, tail, _ = _mixer_tile(PB, r["x"][...], sh, sc, r["h_carry"][...],
                                     r["tail_carry"][...], sgu, r, r["hbuf"])
        r["y"][...] = yn
        r["h_carry"][...] = h
        r["tail_carry"][...] = tail
        r["hlast_p"][...] = h
        r["conv_p"][...] = tail

    @pl.when(i == P_TILES)
    def _():
        sh = r["mod"][PB:PB + SB, 0:D]
        sc = r["mod"][PB:PB + SB, D:2 * D]
        sgu = functools.partial(_sgu_sample, coef_ref=r["coef"], bias_ref=r["bias8"])
        yn, h, tail, v = _mixer_tile(SB, r["x"][...], sh, sc, r["h0s"][...],
                                     r["convs"][...], sgu, r, r["hbuf"])
        r["y"][...] = yn
        r["hlast_s"][...] = h
        r["conv_s"][...] = tail
        r["v_s"][...] = v


def _mixer(x_all, mod_l, w):
    nb = PB + SB
    in_specs = [
        pl.BlockSpec((TILE, D), lambda i: (i, 0)),
        _const_spec((nb, MOD_COLS)),
        _const_spec((1, D)),
        _const_spec((D, 2 * LW + 2 * SW)),
        _const_spec((CONV_W, LW)),
        _const_spec((1, LW)),
        _const_spec((2, LW // 2, LW // 2)),
        _const_spec((2, LW // 2, LW // 2)),
        _const_spec((1, LW)),
        _const_spec((1, LW)),
        _const_spec((1, LW)),
        _const_spec((1, SW)),
        _const_spec((4, CHUNK, 2 * CHUNK)),
        _const_spec((CHUNK, SW)),
        _const_spec((DEC_SEQ, DEC_SEQ, SW)),
        _const_spec((DEC_SEQ, SW)),
        _const_spec((1, D)),
        _const_spec((256, 256)),
        _const_spec((SB, LW)),
        _const_spec(((CONV_W - 1) * SB, LW)),
    ]
    out_shape = (
        jax.ShapeDtypeStruct((NT, D), BF16),
        jax.ShapeDtypeStruct((PB, LW), F32),
        jax.ShapeDtypeStruct(((CONV_W - 1) * PB, LW), F32),
        jax.ShapeDtypeStruct((SB, LW), F32),
        jax.ShapeDtypeStruct(((CONV_W - 1) * SB, LW), F32),
        jax.ShapeDtypeStruct((NS, SW), F32),
    )
    out_specs = (
        pl.BlockSpec((TILE, D), lambda i: (i, 0)),
        _const_spec((PB, LW)),
        _const_spec(((CONV_W - 1) * PB, LW)),
        _const_spec((SB, LW)),
        _const_spec(((CONV_W - 1) * SB, LW)),
        _const_spec((NS, SW)),
    )
    scratch = [
        pltpu.VMEM((TILE, LW), F32),
        pltpu.VMEM((4, TILE, 128), F32),
        pltpu.VMEM((4, TILE, 128), F32),
        pltpu.VMEM((PB, LW), F32),
        pltpu.VMEM(((CONV_W - 1) * PB, LW), F32),
    ]
    return pl.pallas_call(
        _mixer_kernel,
        grid=(N_TILES,),
        in_specs=in_specs,
        out_specs=out_specs,
        out_shape=out_shape,
        scratch_shapes=scratch,
        compiler_params=pltpu.CompilerParams(
            dimension_semantics=("arbitrary",), vmem_limit_bytes=VMEM_LIMIT),
        name="mixer",
    )(x_all, mod_l, w["norm_mix_g"], w["w_in"], w["conv_w"], w["conv_b"], w["wr"], w["wi"],
      w["br"], w["bi"], w["lam"], w["sgu_g"], w["wpair"], w["bmap"], w["coef"], w["bias8"],
      w["out_g"], w["bd"], w["h0s"], w["convs"])


def _router_tile(G, r, gm, sc, sh):
    nst = TILE // G
    out = _dot(r["y"][...], r["w_out"][...])
    xm3 = r["x"][...].reshape(nst, G, D) + gm[None] * out.reshape(nst, G, D)
    r["x_mid"][...] = xm3.reshape(TILE, D)
    ms = jnp.mean(xm3 * xm3, axis=-1, keepdims=True)
    xn3 = xm3 * lax.rsqrt(ms + EPS) * r["norm_g"][...]
    xn = (xn3 * (1.0 + sc[None]) + sh[None]).reshape(TILE, D)
    r["xn"][...] = xn

    logits = lax.dot_general(r["rw_t"][...], xn.astype(BF16), (((1,), (1,)), ((), ())),
                             preferred_element_type=F32)
    sig = jax.nn.sigmoid(logits)
    sel = sig + r["rbias"][...]
    eiota = lax.broadcasted_iota(I32, (NE, TILE), 0)
    chosen = jnp.zeros((NE, TILE), F32)
    w_rows, hots = [], []
    for _ in range(TOPK):
        m = jnp.max(sel, axis=0, keepdims=True)
        idx = jnp.min(jnp.where(sel == m, eiota, NE), axis=0, keepdims=True)
        hot = eiota == idx
        hots.append(hot)
        w_rows.append(jnp.sum(jnp.where(hot, sig, 0.0), axis=0, keepdims=True))
        chosen = jnp.where(hot, 1.0, chosen)
        sel = jnp.where(hot, -jnp.inf, sel)
    den = w_rows[0]
    for k in range(1, TOPK):
        den = den + w_rows[k]
    scale = ROUTED_SCALE / den
    r["w_t"][0] = jnp.concatenate([wk * scale for wk in w_rows], axis=0)

    cb = chosen.astype(BF16)
    rank = _dot(cb, r["tri"][...])
    cnt = _dot(cb, r["bsel"][...])
    groups = jnp.floor((cnt + (ROWS - 1.0)) * (1.0 / ROWS))
    start = ROWS * _dot(r["ltri"][...], groups.astype(BF16))
    pos = rank + jnp.concatenate(
        [jnp.broadcast_to(start[:, j:j + 1], (NE, TB)) for j in range(BPT)], axis=1)
    lp_rows = [jnp.sum(jnp.where(hot, pos, 0.0), axis=0, keepdims=True) for hot in hots]
    r["lpos_t"][0] = jnp.concatenate(lp_rows, axis=0).astype(I32)
    r["cnt"][0] = cnt


_RT_IN = ("x", "y", "mod", "w_out", "norm_g", "rw_t", "rbias", "tri", "bsel", "ltri")
_RT_OUT = ("x_mid", "xn", "lpos_t", "w_t", "cnt")


def _router_kernel(*refs):
    r = dict(zip(_RT_IN + _RT_OUT, refs))
    i = pl.program_id(0)

    def mods(lo, n):
        m = r["mod"]
        return m[lo:lo + n, 2 * D:3 * D], m[lo:lo + n, 4 * D:5 * D], m[lo:lo + n, 3 * D:4 * D]

    @pl.when(i < P_TILES)
    def _():
        _router_tile(PB, r, *mods(0, PB))

    @pl.when(i == P_TILES)
    def _():
        _router_tile(SB, r, *mods(PB, SB))


def _router(x_all, y_all, mod_l, w):
    nb = PB + SB
    in_specs = [
        pl.BlockSpec((TILE, D), lambda i: (i, 0)),
        pl.BlockSpec((TILE, D), lambda i: (i, 0)),
        _const_spec((nb, MOD_COLS)),
        _const_spec((D, D)),
        _const_spec((1, D)),
        _const_spec((NE, D)),
        _const_spec((NE, 1)),
        _const_spec((TILE, TILE)),
        _const_spec((TILE, 128)),
        _const_spec((NE, NE)),
    ]
    out_shape = (
        jax.ShapeDtypeStruct((NT, D), F32),
        jax.ShapeDtypeStruct((NT, D), F32),
        jax.ShapeDtypeStruct((N_TILES, TOPK, TILE), I32),
        jax.ShapeDtypeStruct((N_TILES, TOPK, TILE), F32),
        jax.ShapeDtypeStruct((N_TILES, NE, 128), F32),
    )
    out_specs = (
        pl.BlockSpec((TILE, D), lambda i: (i, 0)),
        pl.BlockSpec((TILE, D), lambda i: (i, 0)),
        pl.BlockSpec((1, TOPK, TILE), lambda i: (i, 0, 0)),
        pl.BlockSpec((1, TOPK, TILE), lambda i: (i, 0, 0)),
        pl.BlockSpec((1, NE, 128), lambda i: (i, 0, 0)),
    )
    return pl.pallas_call(
        _router_kernel,
        grid=(N_TILES,),
        in_specs=in_specs,
        out_specs=out_specs,
        out_shape=out_shape,
        compiler_params=pltpu.CompilerParams(
            dimension_semantics=("arbitrary",), vmem_limit_bytes=VMEM_LIMIT),
        name="router",
    )(x_all, y_all, mod_l, w["w_out"], w["norm_ffn_g"], w["rw_t"], w["rbias"], w["tri"],
      w["bsel"], w["ltri"])


def _extra_pairs(nrows):
    return (nrows - NCH_MIN * ROWS + 2 * ROWS - 1) // (2 * ROWS)


def _chunk_copies(tab_ref, nrows, make_copy):
    def body(i, carry):
        for u in range(2):
            j = 2 * i + u
            g = pl.multiple_of(tab_ref[0, 0, j], ROWS)
            make_copy(pl.multiple_of(j * ROWS, ROWS), g).start(priority=u)
        return carry

    lax.fori_loop(0, NCH_MIN // 2 + _extra_pairs(nrows), body, 0)


def _chunk_waits(nrows, whole_copy, chunk_copy):
    whole_copy.wait()

    def body(i, carry):
        chunk_copy.wait()
        chunk_copy.wait()
        return carry

    lax.fori_loop(0, _extra_pairs(nrows), body, 0)


def _dispatch_kernel(tab_ref, fill_ref, nrows_ref, lpos_ref, xn_ref, xs_ref, loc, zbuf, sem):
    b = pl.program_id(0)
    slot = b % 2
    xb = xn_ref[...].astype(BF16)
    lp = lpos_ref[0].astype(I16)
    one = jnp.ones((), BF16)

    def sort_rows(c):
        siota = (lax.broadcasted_iota(I32, (SUB, TB), 0) + c * SUB).astype(I16)
        p = jnp.zeros((SUB, TB), BF16)
        for k in range(TOPK):
            p = jnp.where(siota == lp[k:k + 1, :], one, p)
        loc[slot, c * SUB:(c + 1) * SUB, :] = _dot(p, xb).astype(BF16)

    for c in range(LAST):
        sort_rows(c)
    tail_used = nrows_ref[b] > LAST * SUB

    @pl.when(tail_used)
    def _():
        sort_rows(LAST)

    @pl.when(jnp.logical_not(tail_used))
    def _():
        loc[slot, LAST * SUB:, :] = jnp.zeros((SUB, D), BF16)

    _chunk_copies(tab_ref, nrows_ref[b], lambda l, g: pltpu.make_async_copy(
        loc.at[slot, pl.ds(l, ROWS)], xs_ref.at[pl.ds(g, ROWS)], sem.at[slot]))

    def drain(s, nrows):
        _chunk_waits(
            nrows,
            pltpu.make_async_copy(loc.at[s, pl.ds(0, NCH_MIN * ROWS)],
                                  xs_ref.at[pl.ds(0, NCH_MIN * ROWS)], sem.at[s]),
            pltpu.make_async_copy(loc.at[s, pl.ds(0, ROWS)], xs_ref.at[pl.ds(0, ROWS)], sem.at[s]))

    @pl.when(b > 0)
    def _():
        drain(1 - slot, nrows_ref[jnp.maximum(b - 1, 0)])

    @pl.when(b == NBLK - 1)
    def _():
        drain(slot, nrows_ref[b])
        zbuf[...] = jnp.zeros((ETILE, D), BF16)

        def zero_chunk(j):
            return pltpu.make_async_copy(
                zbuf.at[pl.ds(0, ROWS)],
                xs_ref.at[pl.ds(pl.multiple_of(fill_ref[1] + j * ROWS, ROWS), ROWS)], sem.at[1])

        def start_chunk(j, carry):
            zero_chunk(j).start()
            return carry

        def wait_chunk(j, carry):
            zero_chunk(j).wait()
            return carry

        ntail = (fill_ref[0] * ETILE - fill_ref[1]) // ROWS
        lax.fori_loop(0, ntail, start_chunk, 0)
        lax.fori_loop(0, ntail, wait_chunk, 0)

        def zero_tile(t):
            return pltpu.make_async_copy(
                zbuf, xs_ref.at[pl.ds(pl.multiple_of(t * ETILE, ETILE), ETILE)], sem.at[0])

        def start(t, carry):
            zero_tile(t).start()
            return carry

        def wait(t, carry):
            zero_tile(t).wait()
            return carry

        lax.fori_loop(fill_ref[0], N_XTILES, start, 0)
        lax.fori_loop(fill_ref[0], N_XTILES, wait, 0)


def _dispatch(tab_d, fill_from, nrows, lpos_d, xn):
    return pl.pallas_call(
        _dispatch_kernel,
        grid=(NBLK,),
        in_specs=[
            pl.BlockSpec((1, 1, NCH), lambda b: (b, 0, 0), memory_space=pltpu.SMEM),
            pl.BlockSpec(memory_space=pltpu.SMEM),
            pl.BlockSpec(memory_space=pltpu.SMEM),
            pl.BlockSpec((1, TOPK, TB), lambda b: (b, 0, 0)),
            pl.BlockSpec((TB, D), lambda b: (b, 0)),
        ],
        out_specs=pl.BlockSpec(memory_space=pl.ANY),
        out_shape=jax.ShapeDtypeStruct((XS_ROWS, D), BF16),
        scratch_shapes=[pltpu.VMEM((2, LS, D), BF16), pltpu.VMEM((ETILE, D), BF16),
                        pltpu.SemaphoreType.DMA((2,))],
        compiler_params=pltpu.CompilerParams(
            dimension_semantics=("arbitrary",), vmem_limit_bytes=VMEM_LIMIT),
        name="dispatch",
    )(tab_d, fill_from, nrows, lpos_d, xn)


def _expert_kernel(tin_ref, tout_ref, exp_ref, lo_ref, hi_ref, flag_ref, new_ref, xs_ref, wg_ref,
                   wu_ref, wd_ref, ys_ref, wg_b, wu_b, wd_b):
    v = pl.program_id(0)
    flag = flag_ref[v]

    @pl.when(flag >= 2)
    def _():
        ys_ref[...] = jnp.zeros((ETILE, D), BF16)

    @pl.when(new_ref[v] == 1)
    def _():
        wg_b[...] = wg_ref[0, 0].astype(BF16)
        wu_b[...] = wu_ref[0, 0].astype(BF16)
        wd_b[...] = wd_ref[0, 0].astype(BF16)

    @pl.when((flag == 1) | (flag == 2))
    def _():
        x = xs_ref[...]
        hid = jax.nn.silu(_dot(x, wg_b[...])) * _dot(x, wu_b[...])
        y = _dot(hid.astype(BF16), wd_b[...]).astype(BF16)
        row = lax.broadcasted_iota(I32, (ETILE, 1), 0)
        mine = (row >= lo_ref[v]) & (row < hi_ref[v])
        ys_ref[...] = jnp.where(mine, y, ys_ref[...])


def _experts(l, meta, xs, wg, wu, wd):
    def wmap(v, ti, to, e, lo, hi, f, n):
        return (l, e[v], 0, 0)

    grid_spec = pltpu.PrefetchScalarGridSpec(
        num_scalar_prefetch=7,
        grid=(N_VISITS,),
        in_specs=[
            pl.BlockSpec((ETILE, D), lambda v, ti, to, e, lo, hi, f, n: (ti[v], 0)),
            pl.BlockSpec((1, 1, D, ED), wmap),
            pl.BlockSpec((1, 1, D, ED), wmap),
            pl.BlockSpec((1, 1, ED, D), wmap),
        ],
        out_specs=pl.BlockSpec((ETILE, D), lambda v, ti, to, e, lo, hi, f, n: (to[v], 0)),
        scratch_shapes=[pltpu.VMEM((D, ED), BF16), pltpu.VMEM((D, ED), BF16),
                        pltpu.VMEM((ED, D), BF16)],
    )
    return pl.pallas_call(
        _expert_kernel,
        grid_spec=grid_spec,
        out_shape=jax.ShapeDtypeStruct((NSLOT_MAX, D), BF16),
        compiler_params=pltpu.CompilerParams(
            dimension_semantics=("arbitrary",), vmem_limit_bytes=VMEM_LIMIT),
        name="experts",
    )(*meta, xs, wg, wu, wd)


def _combine_tile(G, r, slot, gf, final):
    nst = TB // G
    xb = r["xn"][...].astype(BF16)
    hid = jax.nn.silu(_dot(xb, r["sg"][...])) * _dot(xb, r["su"][...])
    moe = _dot(hid.astype(BF16), r["sd"][...])
    _chunk_waits(
        r["nrows"][pl.program_id(0)],
        pltpu.make_async_copy(r["ys"].at[pl.ds(0, NCH_MIN * ROWS)],
                              r["loc"].at[slot, pl.ds(0, NCH_MIN * ROWS)], r["sem"].at[slot]),
        pltpu.make_async_copy(r["ys"].at[pl.ds(0, ROWS)], r["loc"].at[slot, pl.ds(0, ROWS)],
                              r["sem"].at[slot]))
    lp = r["lpos"][0].astype(I16)
    wts = r["wts"][0].astype(BF16)

    def unsort_rows(c):
        siota = (lax.broadcasted_iota(I32, (SUB, TB), 0) + c * SUB).astype(I16)
        p = jnp.zeros((SUB, TB), BF16)
        for k in range(TOPK):
            p = jnp.where(siota == lp[k:k + 1, :], wts[k:k + 1, :], p)
        return lax.dot_general(
            p, r["loc"][slot, c * SUB:(c + 1) * SUB, :], (((0,), (0,)), ((), ())),
            preferred_element_type=F32)

    def finish(moe):
        xo = r["x_mid"][...].reshape(nst, G, D) + gf[None] * moe.reshape(nst, G, D)
        r["x_out"][...] = xo.reshape(TB, D)
        if final:
            ms = jnp.mean(xo * xo, axis=-1, keepdims=True)
            r["y_out"][...] = (xo * lax.rsqrt(ms + EPS) * r["fin_g"][...]).reshape(TB, D)

    for c in range(LAST):
        moe = moe + unsort_rows(c)
    tail_used = r["nrows"][pl.program_id(0)] > LAST * SUB

    @pl.when(tail_used)
    def _():
        finish(moe + unsort_rows(LAST))

    @pl.when(jnp.logical_not(tail_used))
    def _():
        finish(moe)


def _combine_kernel(final, *refs):
    names = ["tab", "tab_next", "nrows", "ys", "x_mid", "xn", "lpos", "wts", "mod", "sg", "su", "sd"]
    names += ["fin_g"] if final else []
    names += ["x_out"] + (["y_out"] if final else []) + ["loc", "sem"]
    r = dict(zip(names, refs))
    b = pl.program_id(0)
    slot = b % 2

    def gather(tab_ref, s, nrows):
        r["loc"][s, NCH_MIN * ROWS:, :] = jnp.zeros((LS - NCH_MIN * ROWS, D), BF16)
        _chunk_copies(tab_ref, nrows, lambda l, g: pltpu.make_async_copy(
            r["ys"].at[pl.ds(g, ROWS)], r["loc"].at[s, pl.ds(l, ROWS)], r["sem"].at[s]))

    @pl.when(b == 0)
    def _():
        gather(r["tab"], 0, r["nrows"][0])

    @pl.when(b + 1 < NBLK)
    def _():
        gather(r["tab_next"], 1 - slot, r["nrows"][jnp.minimum(b + 1, NBLK - 1)])

    @pl.when(b < PBLK)
    def _():
        _combine_tile(PB, r, slot, r["mod"][0:PB, 5 * D:6 * D], final)

    @pl.when(b >= PBLK)
    def _():
        _combine_tile(SB, r, slot, r["mod"][PB:PB + SB, 5 * D:6 * D], final)


def _combine(tab_c, nrows, ys, x_mid, xn, lpos_c, wts, mod_l, w, fin_g):
    final = fin_g is not None
    nb = PB + SB
    in_specs = [
        pl.BlockSpec((1, 1, NCH), lambda b: (b, 0, 0), memory_space=pltpu.SMEM),
        pl.BlockSpec((1, 1, NCH), lambda b: (jnp.minimum(b + 1, NBLK - 1), 0, 0),
                     memory_space=pltpu.SMEM),
        pl.BlockSpec(memory_space=pltpu.SMEM),
        pl.BlockSpec(memory_space=pl.ANY),
        pl.BlockSpec((TB, D), lambda b: (b, 0)),
        pl.BlockSpec((TB, D), lambda b: (b, 0)),
        pl.BlockSpec((1, TOPK, TB), lambda b: (b, 0, 0)),
        pl.BlockSpec((1, TOPK, TB), lambda b: (b, 0, 0)),
        _const_spec((nb, MOD_COLS)),
        _const_spec((D, ED)),
        _const_spec((D, ED)),
        _const_spec((ED, D)),
    ]
    args = [tab_c, tab_c, nrows, ys, x_mid, xn, lpos_c, wts, mod_l, w["sg"], w["su"], w["sd"]]
    out_shape = [jax.ShapeDtypeStruct((NT, D), F32)]
    out_specs = [pl.BlockSpec((TB, D), lambda b: (b, 0))]
    if final:
        in_specs.append(_const_spec((1, D)))
        args.append(fin_g)
        out_shape.append(jax.ShapeDtypeStruct((NT, D), F32))
        out_specs.append(pl.BlockSpec((TB, D), lambda b: (b, 0)))
    res = pl.pallas_call(
        functools.partial(_combine_kernel, final),
        grid=(NBLK,),
        in_specs=in_specs,
        out_specs=tuple(out_specs),
        out_shape=tuple(out_shape),
        scratch_shapes=[pltpu.VMEM((2, LS, D), BF16), pltpu.SemaphoreType.DMA((2,))],
        compiler_params=pltpu.CompilerParams(
            dimension_semantics=("arbitrary",), vmem_limit_bytes=VMEM_LIMIT),
        name="combine_final" if final else "combine",
    )(*args)
    return res if final else (res[0], None)


def _plan(cnt):
    c = cnt[:, :, :BPT].transpose(0, 2, 1).reshape(NBLK, NE).astype(I32)
    pc = (c + ROWS - 1) // ROWS * ROWS
    lend = jnp.cumsum(pc, axis=1)
    lstart = lend - pc
    etot = jnp.sum(pc, axis=0)
    eend = jnp.cumsum(etot)
    eoff = eend - etot
    gstart = eoff[None, :] + jnp.cumsum(pc, axis=0) - pc
    total = eend[-1]
    eids = jnp.arange(NE, dtype=I32)

    l8 = jnp.arange(NCH, dtype=I32) * ROWS
    valid = l8[None, :] < lend[:, -1:]
    ej = jnp.minimum(jnp.sum((lend[:, None, :] <= l8[None, :, None]).astype(I32), axis=-1), NE - 1)
    hot = ej[..., None] == eids
    row = jnp.sum(jnp.where(hot, (gstart - lstart)[:, None, :], 0), axis=-1) + l8[None, :]
    spare = NSLOT_MAX + (jnp.arange(NBLK, dtype=I32) % 2)[:, None] * LS + l8[None, :]
    tab_d = jnp.where(valid, row, spare).reshape(NBLK, 1, NCH)
    tab_c = jnp.where(valid, row, 0).reshape(NBLK, 1, NCH)

    def lookup(table, idx):
        return jnp.sum(jnp.where(idx[..., None] == eids, table, 0), axis=-1)

    first_tile = eoff // ETILE
    last_tile = (eend - 1) // ETILE
    nvis = jnp.where(etot > 0, last_tile - first_tile + 1, 0)
    vend = jnp.cumsum(nvis)
    nvalid = vend[-1]
    v = jnp.arange(N_VISITS, dtype=I32)
    vv = jnp.minimum(v, nvalid - 1)
    e = jnp.minimum(jnp.sum((vend[None, :] <= vv[:, None]).astype(I32), axis=1), NE - 1)
    off_e = lookup(eoff, e)
    tile = lookup(first_tile, e) + (vv - lookup(vend - nvis, e))
    base = tile * ETILE
    lo = jnp.clip(off_e - base, 0, ETILE)
    hi = jnp.clip(lookup(eend, e) - base, 0, ETILE)
    used = (total + ETILE - 1) // ETILE
    spare_tile = used + (v - nvalid)
    tout = jnp.where(v < nvalid, tile, jnp.minimum(spare_tile, N_ETILES - 1))
    flag = jnp.where(v < nvalid, jnp.where(off_e <= base, 2, 1),
                     jnp.where(spare_tile < N_ETILES, 3, 0)).astype(I32)
    new = jnp.concatenate([jnp.ones((1,), I32), (e[1:] != e[:-1]).astype(I32)])
    meta = (tile.astype(I32), tout.astype(I32), e, lo.astype(I32), hi.astype(I32), flag, new)
    return tab_d, tab_c, jnp.stack([used, total]).astype(I32), lend[:, -1], meta


def _block_diag(wh):
    out = jnp.zeros((2, 4, HEAD, 4, HEAD), wh.dtype)
    for hh in range(4):
        out = out.at[:, hh, :, hh, :].set(wh.reshape(2, 4, HEAD, HEAD)[:, hh])
    return out.reshape(2, 4 * HEAD, 4 * HEAD)


def _layer_weights(l, p):
    tril = jnp.tril(jnp.ones((CHUNK, CHUNK), bool))
    wt = jnp.where(tril, p["sgu_w"][l], 0.0)
    wpair = wt.reshape(4, 2, CHUNK, CHUNK).transpose(0, 2, 1, 3).reshape(4, CHUNK, 2 * CHUNK)
    bmap = jnp.repeat(p["sgu_b"][l].T, HEAD, axis=1)
    w8 = jnp.where(tril[:DEC_SEQ, :DEC_SEQ], p["sgu_w"][l][:, :DEC_SEQ, :DEC_SEQ], 0.0)
    coef = jnp.repeat(w8.transpose(2, 1, 0), HEAD, axis=2)
    bias8 = jnp.repeat(p["sgu_b"][l][:, :DEC_SEQ].T, HEAD, axis=1)
    gid = jnp.arange(256) // HEAD
    tok = jnp.arange(TILE)
    same_blk = (tok[:, None] // TB) == (tok[None, :] // TB)
    return dict(
        norm_mix_g=p["norm_mix_g"][l][None], norm_ffn_g=p["norm_ffn_g"][l][None],
        w_in=p["w_in"][l].astype(BF16), w_out=p["w_out"][l].astype(BF16),
        conv_w=p["conv_w"][l], conv_b=p["conv_b"][l][None],
        wr=_block_diag(p["gate_r_w"][l]).astype(BF16), wi=_block_diag(p["gate_i_w"][l]).astype(BF16),
        br=p["gate_r_b"][l][None], bi=p["gate_i_b"][l][None], lam=p["lru_lambda"][l][None],
        sgu_g=p["sgu_norm_g"][l][None], wpair=wpair.astype(BF16), bmap=bmap, coef=coef, bias8=bias8,
        out_g=p["out_norm_g"][l][None],
        bd=(gid[:, None] == gid[None, :]).astype(BF16),
        rw_t=p["router_w"][l].T.astype(BF16), rbias=p["router_bias"][l][:, None],
        tri=(same_blk & (tok[:, None] < tok[None, :])).astype(BF16),
        bsel=((tok[:, None] // TB) == jnp.arange(128)[None, :]).astype(BF16),
        ltri=(jnp.arange(NE)[:, None] > jnp.arange(NE)[None, :]).astype(BF16),
        sg=p["shared_w_gate"][l].astype(BF16), su=p["shared_w_up"][l].astype(BF16),
        sd=p["shared_w_down"][l].astype(BF16),
    )


def kernel(x_prompt, x_sample, c_prompt, c_sample, state_lru_h, state_conv, ada_w, ada_b, norm_mix_g, norm_ffn_g, w_in, conv_w, conv_b, gate_r_w, gate_r_b, gate_i_w, gate_i_b, lru_lambda, sgu_norm_g, sgu_w, sgu_b, out_norm_g, w_out, router_w, router_bias, exp_w_gate, exp_w_up, exp_w_down, shared_w_gate, shared_w_up, shared_w_down, final_norm_g):
    p = dict(norm_mix_g=norm_mix_g, norm_ffn_g=norm_ffn_g, w_in=w_in, conv_w=conv_w, conv_b=conv_b,
             gate_r_w=gate_r_w, gate_r_b=gate_r_b, gate_i_w=gate_i_w, gate_i_b=gate_i_b,
             lru_lambda=lru_lambda, sgu_norm_g=sgu_norm_g, sgu_w=sgu_w, sgu_b=sgu_b,
             out_norm_g=out_norm_g, w_out=w_out, router_w=router_w, router_bias=router_bias,
             shared_w_gate=shared_w_gate, shared_w_up=shared_w_up, shared_w_down=shared_w_down)
    x_all = jnp.concatenate([x_prompt.transpose(1, 0, 2).reshape(NP, D),
                             x_sample.transpose(1, 0, 2).reshape(NS, D)], axis=0)
    mod = _modulations(jnp.concatenate([c_prompt, c_sample], axis=0), ada_w, ada_b)

    hp, cp, hs, cs, vs = [], [], [], [], []
    y_all = None
    for l in range(DEPTH):
        w = _layer_weights(l, p)
        w["h0s"] = state_lru_h[l]
        w["convs"] = state_conv[l].transpose(1, 0, 2).reshape((CONV_W - 1) * SB, LW)
        yn, hlp, cvp, hls, cvs, v_s = _mixer(x_all, mod[l], w)
        x_mid, xn, lpos_t, w_t, cnt = _router(x_all, yn, mod[l], w)
        tab_d, tab_c, used, nrows, meta = _plan(cnt)
        def per_block(a):
            return a.reshape(N_TILES, TOPK, BPT, TB).transpose(0, 2, 1, 3).reshape(NBLK, TOPK, TB)

        lpos_d = per_block(lpos_t)
        xs = _dispatch(tab_d, used, nrows, lpos_d, xn)
        ys = _experts(l, meta, xs, exp_w_gate, exp_w_up, exp_w_down)
        fin = final_norm_g[None] if l == DEPTH - 1 else None
        x_all, y_all = _combine(tab_c, nrows, ys, x_mid, xn, lpos_d, per_block(w_t), mod[l], w, fin)
        hp.append(hlp)
        cp.append(cvp.reshape(CONV_W - 1, PB, LW).transpose(1, 0, 2))
        hs.append(hls)
        cs.append(cvs.reshape(CONV_W - 1, SB, LW).transpose(1, 0, 2))
        vs.append(v_s.reshape(DEC_SEQ, SB, SW).transpose(1, 0, 2))

    y_prompt = y_all[:NP].reshape(SEQ, PB, D).transpose(1, 0, 2)
    y_sample = y_all[NP:].reshape(DEC_SEQ, SB, D).transpose(1, 0, 2)
    return (y_prompt, y_sample, jnp.stack(hp), jnp.stack(cp), jnp.stack(hs), jnp.stack(cs),
            jnp.stack(vs))
```

```python
import functools

import jax
import jax.numpy as jnp
from jax import lax
from jax.experimental import pallas as pl
from jax.experimental.pallas import tpu as pltpu

F32 = jnp.float32
BF16 = jnp.bfloat16
I32 = jnp.int32
I16 = jnp.int16

D = 1024
DEPTH = 4
PB = 8
SEQ = 2048
SB = 128
DEC_SEQ = 8
NP = PB * SEQ
NS = SB * DEC_SEQ
NT = NP + NS
TILE = 1024
N_TILES = NT // TILE
P_TILES = NP // TILE
LW = 512
SW = 512
HEAD = 64
CHUNK = 128
CONV_W = 4
LRU_C = 8.0
NE = 64
TOPK = 8
ED = 256
ROUTED_SCALE = 2.5
EPS = 1e-6
MOD_COLS = 6 * D
MOD_BLK = 1536
VMEM_LIMIT = 56 * 1024 * 1024

TB = 256
NBLK = NT // TB
PBLK = NP // TB
BPT = TILE // TB
ROWS = 16
LS = 3072
NCH = LS // ROWS
NCH_MIN = TB * TOPK // ROWS
SUB = 512
LAST = LS // SUB - 1
ETILE = 512
NSLOT_MAX = -(-(NT * TOPK + NBLK * NE * (ROWS - 1)) // ETILE) * ETILE
N_ETILES = NSLOT_MAX // ETILE
N_VISITS = N_ETILES + NE - 1
XS_ROWS = NSLOT_MAX + 2 * LS
N_XTILES = XS_ROWS // ETILE

assert NSLOT_MAX % ETILE == 0 and LS >= TB * TOPK + NE * (ROWS - 1) and LS % SUB == 0


def _dot(a, b):
    return jnp.dot(a, b, preferred_element_type=F32)


def _const_spec(shape):
    nd = len(shape)
    return pl.BlockSpec(shape, lambda *_: (0,) * nd)


def _mod_kernel(c_ref, w_ref, b_ref, o_ref):
    s = jax.nn.silu(c_ref[...]).astype(BF16)
    o_ref[0] = _dot(s, w_ref[0].astype(BF16)) + b_ref[0]


def _modulations(c_all, ada_w, ada_b):
    nb = c_all.shape[0]
    return pl.pallas_call(
        _mod_kernel,
        grid=(DEPTH, MOD_COLS // MOD_BLK),
        in_specs=[
            pl.BlockSpec((nb, D), lambda l, j: (0, 0)),
            pl.BlockSpec((1, D, MOD_BLK), lambda l, j: (l, 0, j)),
            pl.BlockSpec((1, 1, MOD_BLK), lambda l, j: (l, 0, j)),
        ],
        out_specs=pl.BlockSpec((1, nb, MOD_BLK), lambda l, j: (l, 0, j)),
        out_shape=jax.ShapeDtypeStruct((DEPTH, nb, MOD_COLS), F32),
        compiler_params=pltpu.CompilerParams(
            dimension_semantics=("arbitrary", "arbitrary"), vmem_limit_bytes=VMEM_LIMIT),
        name="adaln_mod",
    )(c_all, ada_w, ada_b.reshape(DEPTH, 1, MOD_COLS))


def _sgu_prompt(v, wpair_ref, bmap_ref, vbuf, sbuf):
    for j in range(4):
        vbuf[j] = v[:, 128 * j:128 * (j + 1)]
    lane = lax.broadcasted_iota(I32, (CHUNK, 128), 1)
    left = lane < HEAD
    for b in range(PB):
        for j in range(4):
            vp = vbuf[j, pl.ds(b, CHUNK, stride=PB), :]
            rhs = jnp.concatenate(
                [jnp.where(left, vp, 0.0), jnp.where(left, 0.0, vp)], axis=0).astype(BF16)
            sj = _dot(wpair_ref[j], rhs) + bmap_ref[:, 128 * j:128 * (j + 1)]
            sbuf[j, pl.ds(b, CHUNK, stride=PB), :] = sj
    return jnp.concatenate([sbuf[j] for j in range(4)], axis=1)


def _sgu_sample(v, coef_ref, bias_ref):
    vt = [v[t * SB:(t + 1) * SB] for t in range(DEC_SEQ)]
    rows = []
    for p in range(DEC_SEQ):
        acc = bias_ref[p:p + 1, :] + coef_ref[0, p:p + 1, :] * vt[0]
        for q in range(1, p + 1):
            acc = acc + coef_ref[q, p:p + 1, :] * vt[q]
        rows.append(acc)
    return jnp.concatenate(rows, axis=0)


def _mixer_tile(G, x, sh, sc, h_in, tail_in, sgu_fn, r, hbuf):
    nst = TILE // G
    x3 = x.reshape(nst, G, D)
    ms = jnp.mean(x3 * x3, axis=-1, keepdims=True)
    xn = x3 * lax.rsqrt(ms + EPS) * r["norm_g"][...]
    xn = (xn * (1.0 + sc[None]) + sh[None]).reshape(TILE, D).astype(BF16)

    xa = _dot(xn, r["w_in"][:, 0:LW])
    xp = jnp.concatenate([tail_in, xa], axis=0)
    cw = r["conv_w"]
    xc = r["conv_b"][...] + cw[0:1, :] * xp[0:TILE]
    for k in range(1, CONV_W):
        xc = xc + cw[k:k + 1, :] * xp[k * G:k * G + TILE]
    new_tail = xa[TILE - (CONV_W - 1) * G:]
    xcb = xc.astype(BF16)
    half = LW // 2
    r_pre = jnp.concatenate(
        [_dot(xcb[:, :half], r["wr"][0]), _dot(xcb[:, half:], r["wr"][1])], axis=1) + r["br"][...]
    i_pre = jnp.concatenate(
        [_dot(xcb[:, :half], r["wi"][0]), _dot(xcb[:, half:], r["wi"][1])], axis=1) + r["bi"][...]
    rg = jax.nn.sigmoid(r_pre)
    ig = jax.nn.sigmoid(i_pre)
    log_a = -LRU_C * rg * jax.nn.softplus(-r["lam"][...])
    a = jnp.exp(log_a)
    th = jnp.tanh(log_a)
    num = -2.0 * th
    mult = jnp.where(num > 0.0, num * lax.rsqrt(num * (1.0 - th)), 0.0)
    u = mult * (ig * xc)
    h = h_in
    for s in range(nst):
        h = a[s * G:(s + 1) * G] * h + u[s * G:(s + 1) * G]
        hbuf[s * G:(s + 1) * G, :] = h
    ya = hbuf[...] * jax.nn.gelu(_dot(xn, r["w_in"][:, LW:2 * LW]))

    ug = jax.nn.gelu(_dot(xn, r["w_in"][:, 2 * LW:2 * LW + SW]))
    vg = jax.nn.gelu(_dot(xn, r["w_in"][:, 2 * LW + SW:]))
    v = vg * lax.rsqrt(jnp.mean(vg * vg, axis=-1, keepdims=True) + EPS) * r["sgu_g"][...]
    yb = ug * sgu_fn(v)

    y = jnp.concatenate([ya, yb], axis=1)
    bd = r["bd"][...]
    outs = []
    for j in range(D // 256):
        ysl = y[:, 256 * j:256 * (j + 1)]
        sq = ysl * ysl
        hi = sq.astype(BF16)
        lo = (sq - hi.astype(F32)).astype(BF16)
        msq = (_dot(hi, bd) + _dot(lo, bd)) * (1.0 / HEAD)
        outs.append(ysl * lax.rsqrt(msq + EPS) * r["out_g"][:, 256 * j:256 * (j + 1)])
    yn = jnp.concatenate(outs, axis=1).astype(BF16)
    return yn, h, new_tail, v


_MIX_IN = ("x", "mod", "norm_g", "w_in", "conv_w", "conv_b", "wr", "wi", "br", "bi", "lam",
           "sgu_g", "wpair", "bmap", "coef", "bias8", "out_g", "bd", "h0s", "convs")
_MIX_OUT = ("y", "hlast_p", "conv_p", "hlast_s", "conv_s", "v_s")
_MIX_SCR = ("hbuf", "vbuf", "sbuf", "h_carry", "tail_carry")


def _mixer_kernel(*refs):
    names = _MIX_IN + _MIX_OUT + _MIX_SCR
    r = dict(zip(names, refs))
    i = pl.program_id(0)

    @pl.when(i == 0)
    def _():
        r["h_carry"][...] = jnp.zeros((PB, LW), F32)
        r["tail_carry"][...] = jnp.zeros(((CONV_W - 1) * PB, LW), F32)

    @pl.when(i < P_TILES)
    def _():
        sh = r["mod"][0:PB, 0:D]
        sc = r["mod"][0:PB, D:2 * D]
        sgu = functools.partial(_sgu_prompt, wpair_ref=r["wpair"], bmap_ref=r["bmap"],
                                vbuf=r["vbuf"], sbuf=r["sbuf"])
        yn, h, tail, _ = _mixer_tile(PB, r["x"][...], sh, sc, r["h_carry"][...],
                                     r["tail_carry"][...], sgu, r, r["hbuf"])
        r["y"][...] = yn
        r["h_carry"][...] = h
        r["tail_carry"][...] = tail
        r["hlast_p"][...] = h
        r["conv_p"][...] = tail

    @pl.when(i == P_TILES)
    def _():
        sh = r["mod"][PB:PB + SB, 0:D]
        sc = r["mod"][PB:PB + SB, D:2 * D]
        sgu = functools.partial(_sgu_sample, coef_ref=r["coef"], bias_ref=r["bias8"])
        yn, h, tail, v = _mixer_tile(SB, r["x"][...], sh, sc, r["h0s"][...],
                                     r["convs"][...], sgu, r, r["hbuf"])
        r["y"][...] = yn
        r["hlast_s"][...] = h
        r["conv_s"][...] = tail
        r["v_s"][...] = v


def _mixer(x_all, mod_l, w):
    nb = PB + SB
    in_specs = [
        pl.BlockSpec((TILE, D), lambda i: (i, 0)),
        _const_spec((nb, MOD_COLS)),
        _const_spec((1, D)),
        _const_spec((D, 2 * LW + 2 * SW)),
        _const_spec((CONV_W, LW)),
        _const_spec((1, LW)),
        _const_spec((2, LW // 2, LW // 2)),
        _const_spec((2, LW // 2, LW // 2)),
        _const_spec((1, LW)),
        _const_spec((1, LW)),
        _const_spec((1, LW)),
        _const_spec((1, SW)),
        _const_spec((4, CHUNK, 2 * CHUNK)),
        _const_spec((CHUNK, SW)),
        _const_spec((DEC_SEQ, DEC_SEQ, SW)),
        _const_spec((DEC_SEQ, SW)),
        _const_spec((1, D)),
        _const_spec((256, 256)),
        _const_spec((SB, LW)),
        _const_spec(((CONV_W - 1) * SB, LW)),
    ]
    out_shape = (
        jax.ShapeDtypeStruct((NT, D), BF16),
        jax.ShapeDtypeStruct((PB, LW), F32),
        jax.ShapeDtypeStruct(((CONV_W - 1) * PB, LW), F32),
        jax.ShapeDtypeStruct((SB, LW), F32),
        jax.ShapeDtypeStruct(((CONV_W - 1) * SB, LW), F32),
        jax.ShapeDtypeStruct((NS, SW), F32),
    )
    out_specs = (
        pl.BlockSpec((TILE, D), lambda i: (i, 0)),
        _const_spec((PB, LW)),
        _const_spec(((CONV_W - 1) * PB, LW)),
        _const_spec((SB, LW)),
        _const_spec(((CONV_W - 1) * SB, LW)),
        _const_spec((NS, SW)),
    )
    scratch = [
        pltpu.VMEM((TILE, LW), F32),
        pltpu.VMEM((4, TILE, 128), F32),
        pltpu.VMEM((4, TILE, 128), F32),
        pltpu.VMEM((PB, LW), F32),
        pltpu.VMEM(((CONV_W - 1) * PB, LW), F32),
    ]
    return pl.pallas_call(
        _mixer_kernel,
        grid=(N_TILES,),
        in_specs=in_specs,
        out_specs=out_specs,
        out_shape=out_shape,
        scratch_shapes=scratch,
        compiler_params=pltpu.CompilerParams(
            dimension_semantics=("arbitrary",), vmem_limit_bytes=VMEM_LIMIT),
        name="mixer",
    )(x_all, mod_l, w["norm_mix_g"], w["w_in"], w["conv_w"], w["conv_b"], w["wr"], w["wi"],
      w["br"], w["bi"], w["lam"], w["sgu_g"], w["wpair"], w["bmap"], w["coef"], w["bias8"],
      w["out_g"], w["bd"], w["h0s"], w["convs"])


def _router_tile(G, r, gm, sc, sh):
    nst = TILE // G
    out = _dot(r["y"][...], r["w_out"][...])
    xm3 = r["x"][...].reshape(nst, G, D) + gm[None] * out.reshape(nst, G, D)
    r["x_mid"][...] = xm3.reshape(TILE, D)
    ms = jnp.mean(xm3 * xm3, axis=-1, keepdims=True)
    xn3 = xm3 * lax.rsqrt(ms + EPS) * r["norm_g"][...]
    xn = (xn3 * (1.0 + sc[None]) + sh[None]).reshape(TILE, D).astype(BF16)
    r["xn"][...] = xn

    logits = lax.dot_general(r["rw_t"][...], xn, (((1,), (1,)), ((), ())),
                             preferred_element_type=F32)
    sig = jax.nn.sigmoid(logits)
    sel = sig + r["rbias"][...]
    eiota = lax.broadcasted_iota(I32, (NE, TILE), 0)
    chosen = jnp.zeros((NE, TILE), F32)
    w_rows, hots = [], []
    for _ in range(TOPK):
        m = jnp.max(sel, axis=0, keepdims=True)
        idx = jnp.min(jnp.where(sel == m, eiota, NE), axis=0, keepdims=True)
        hot = eiota == idx
        hots.append(hot)
        w_rows.append(jnp.sum(jnp.where(hot, sig, 0.0), axis=0, keepdims=True))
        chosen = jnp.where(hot, 1.0, chosen)
        sel = jnp.where(hot, -jnp.inf, sel)
    den = w_rows[0]
    for k in range(1, TOPK):
        den = den + w_rows[k]
    scale = ROUTED_SCALE / den
    r["w_t"][0] = jnp.concatenate([wk * scale for wk in w_rows], axis=0)

    cb = chosen.astype(BF16)
    rank = _dot(cb, r["tri"][...])
    cnt = _dot(cb, r["bsel"][...])
    groups = jnp.floor((cnt + (ROWS - 1.0)) * (1.0 / ROWS))
    start = ROWS * _dot(r["ltri"][...], groups.astype(BF16))
    pos = rank + jnp.concatenate(
        [jnp.broadcast_to(start[:, j:j + 1], (NE, TB)) for j in range(BPT)], axis=1)
    lp_rows = [jnp.sum(jnp.where(hot, pos, 0.0), axis=0, keepdims=True) for hot in hots]
    r["lpos_t"][0] = jnp.concatenate(lp_rows, axis=0).astype(I32)
    r["cnt"][0] = cnt


_RT_IN = ("x", "y", "mod", "w_out", "norm_g", "rw_t", "rbias", "tri", "bsel", "ltri")
_RT_OUT = ("x_mid", "xn", "lpos_t", "w_t", "cnt")


def _router_kernel(*refs):
    r = dict(zip(_RT_IN + _RT_OUT, refs))
    i = pl.program_id(0)

    def mods(lo, n):
        m = r["mod"]
        return m[lo:lo + n, 2 * D:3 * D], m[lo:lo + n, 4 * D:5 * D], m[lo:lo + n, 3 * D:4 * D]

    @pl.when(i < P_TILES)
    def _():
        _router_tile(PB, r, *mods(0, PB))

    @pl.when(i == P_TILES)
    def _():
        _router_tile(SB, r, *mods(PB, SB))


def _router(x_all, y_all, mod_l, w):
    nb = PB + SB
    in_specs = [
        pl.BlockSpec((TILE, D), lambda i: (i, 0)),
        pl.BlockSpec((TILE, D), lambda i: (i, 0)),
        _const_spec((nb, MOD_COLS)),
        _const_spec((D, D)),
        _const_spec((1, D)),
        _const_spec((NE, D)),
        _const_spec((NE, 1)),
        _const_spec((TILE, TILE)),
        _const_spec((TILE, 128)),
        _const_spec((NE, NE)),
    ]
    out_shape = (
        jax.ShapeDtypeStruct((NT, D), F32),
        jax.ShapeDtypeStruct((NT, D), BF16),
        jax.ShapeDtypeStruct((N_TILES, TOPK, TILE), I32),
        jax.ShapeDtypeStruct((N_TILES, TOPK, TILE), F32),
        jax.ShapeDtypeStruct((N_TILES, NE, 128), F32),
    )
    out_specs = (
        pl.BlockSpec((TILE, D), lambda i: (i, 0)),
        pl.BlockSpec((TILE, D), lambda i: (i, 0)),
        pl.BlockSpec((1, TOPK, TILE), lambda i: (i, 0, 0)),
        pl.BlockSpec((1, TOPK, TILE), lambda i: (i, 0, 0)),
        pl.BlockSpec((1, NE, 128), lambda i: (i, 0, 0)),
    )
    return pl.pallas_call(
        _router_kernel,
        grid=(N_TILES,),
        in_specs=in_specs,
        out_specs=out_specs,
        out_shape=out_shape,
        compiler_params=pltpu.CompilerParams(
            dimension_semantics=("arbitrary",), vmem_limit_bytes=VMEM_LIMIT),
        name="router",
    )(x_all, y_all, mod_l, w["w_out"], w["norm_ffn_g"], w["rw_t"], w["rbias"], w["tri"],
      w["bsel"], w["ltri"])


def _extra_pairs(nrows):
    return (nrows - NCH_MIN * ROWS + 2 * ROWS - 1) // (2 * ROWS)


def _chunk_copies(tab_ref, nrows, make_copy):
    def body(i, carry):
        for u in range(2):
            j = 2 * i + u
            g = pl.multiple_of(tab_ref[0, 0, j], ROWS)
            make_copy(pl.multiple_of(j * ROWS, ROWS), g).start(priority=u)
        return carry

    lax.fori_loop(0, NCH_MIN // 2 + _extra_pairs(nrows), body, 0)


def _chunk_waits(nrows, whole_copy, chunk_copy):
    whole_copy.wait()

    def body(i, carry):
        chunk_copy.wait()
        chunk_copy.wait()
        return carry

    lax.fori_loop(0, _extra_pairs(nrows), body, 0)


def _dispatch_kernel(tab_ref, fill_ref, nrows_ref, lpos_ref, xn_ref, xs_ref, loc, zbuf, sem):
    b = pl.program_id(0)
    slot = b % 2
    xb = xn_ref[...]
    lp = lpos_ref[0].astype(I16)
    one = jnp.ones((), BF16)

    def sort_rows(c):
        siota = (lax.broadcasted_iota(I32, (SUB, TB), 0) + c * SUB).astype(I16)
        p = jnp.zeros((SUB, TB), BF16)
        for k in range(TOPK):
            p = jnp.where(siota == lp[k:k + 1, :], one, p)
        loc[slot, c * SUB:(c + 1) * SUB, :] = _dot(p, xb).astype(BF16)

    for c in range(LAST):
        sort_rows(c)
    tail_used = nrows_ref[b] > LAST * SUB

    @pl.when(tail_used)
    def _():
        sort_rows(LAST)

    @pl.when(jnp.logical_not(tail_used))
    def _():
        loc[slot, LAST * SUB:, :] = jnp.zeros((SUB, D), BF16)

    _chunk_copies(tab_ref, nrows_ref[b], lambda l, g: pltpu.make_async_copy(
        loc.at[slot, pl.ds(l, ROWS)], xs_ref.at[pl.ds(g, ROWS)], sem.at[slot]))

    def drain(s, nrows):
        _chunk_waits(
            nrows,
            pltpu.make_async_copy(loc.at[s, pl.ds(0, NCH_MIN * ROWS)],
                                  xs_ref.at[pl.ds(0, NCH_MIN * ROWS)], sem.at[s]),
            pltpu.make_async_copy(loc.at[s, pl.ds(0, ROWS)], xs_ref.at[pl.ds(0, ROWS)], sem.at[s]))

    @pl.when(b > 0)
    def _():
        drain(1 - slot, nrows_ref[jnp.maximum(b - 1, 0)])

    @pl.when(b == NBLK - 1)
    def _():
        drain(slot, nrows_ref[b])
        zbuf[...] = jnp.zeros((ETILE, D), BF16)

        def zero_chunk(j):
            return pltpu.make_async_copy(
                zbuf.at[pl.ds(0, ROWS)],
                xs_ref.at[pl.ds(pl.multiple_of(fill_ref[1] + j * ROWS, ROWS), ROWS)], sem.at[1])

        def start_chunk(j, carry):
            zero_chunk(j).start()
            return carry

        def wait_chunk(j, carry):
            zero_chunk(j).wait()
            return carry

        ntail = (fill_ref[0] * ETILE - fill_ref[1]) // ROWS
        lax.fori_loop(0, ntail, start_chunk, 0)
        lax.fori_loop(0, ntail, wait_chunk, 0)

        def zero_tile(t):
            return pltpu.make_async_copy(
                zbuf, xs_ref.at[pl.ds(pl.multiple_of(t * ETILE, ETILE), ETILE)], sem.at[0])

        def start(t, carry):
            zero_tile(t).start()
            return carry

        def wait(t, carry):
            zero_tile(t).wait()
            return carry

        lax.fori_loop(fill_ref[0], N_XTILES, start, 0)
        lax.fori_loop(fill_ref[0], N_XTILES, wait, 0)


def _dispatch(tab_d, fill_from, nrows, lpos_d, xn):
    return pl.pallas_call(
        _dispatch_kernel,
        grid=(NBLK,),
        in_specs=[
            pl.BlockSpec((1, 1, NCH), lambda b: (b, 0, 0), memory_space=pltpu.SMEM),
            pl.BlockSpec(memory_space=pltpu.SMEM),
            pl.BlockSpec(memory_space=pltpu.SMEM),
            pl.BlockSpec((1, TOPK, TB), lambda b: (b, 0, 0)),
            pl.BlockSpec((TB, D), lambda b: (b, 0)),
        ],
        out_specs=pl.BlockSpec(memory_space=pl.ANY),
        out_shape=jax.ShapeDtypeStruct((XS_ROWS, D), BF16),
        scratch_shapes=[pltpu.VMEM((2, LS, D), BF16), pltpu.VMEM((ETILE, D), BF16),
                        pltpu.SemaphoreType.DMA((2,))],
        compiler_params=pltpu.CompilerParams(
            dimension_semantics=("arbitrary",), vmem_limit_bytes=VMEM_LIMIT),
        name="dispatch",
    )(tab_d, fill_from, nrows, lpos_d, xn)


def _expert_kernel(l, tin_ref, tout_ref, exp_ref, lo_ref, hi_ref, flag_ref, new_ref, wslot_ref,
                   next_ref, xs_ref, wg_hbm, wu_hbm, wd_hbm, ys_ref, wg_f, wu_f, wd_f, wg_b, wu_b,
                   wd_b, wsem):
    v = pl.program_id(0)
    flag = flag_ref[v]

    def weight_copies(e, p):
        return (pltpu.make_async_copy(wg_hbm.at[l, e], wg_f.at[p], wsem.at[p]),
                pltpu.make_async_copy(wu_hbm.at[l, e], wu_f.at[p], wsem.at[p]),
                pltpu.make_async_copy(wd_hbm.at[l, e], wd_f.at[p], wsem.at[p]))

    @pl.when(v == 0)
    def _():
        for c in weight_copies(exp_ref[0], 0):
            c.start()

    @pl.when(flag >= 2)
    def _():
        ys_ref[...] = jnp.zeros((ETILE, D), BF16)

    @pl.when(new_ref[v] == 1)
    def _():
        p = wslot_ref[v]
        for c in weight_copies(exp_ref[v], p):
            c.wait()

        @pl.when(next_ref[v] >= 0)
        def _():
            for c in weight_copies(next_ref[v], 1 - p):
                c.start()

        wg_b[...] = wg_f[p].astype(BF16)
        wu_b[...] = wu_f[p].astype(BF16)
        wd_b[...] = wd_f[p].astype(BF16)

    @pl.when((flag == 1) | (flag == 2))
    def _():
        x = xs_ref[...]
        hid = jax.nn.silu(_dot(x, wg_b[...])) * _dot(x, wu_b[...])
        y = _dot(hid.astype(BF16), wd_b[...]).astype(BF16)
        row = lax.broadcasted_iota(I32, (ETILE, 1), 0)
        mine = (row >= lo_ref[v]) & (row < hi_ref[v])
        ys_ref[...] = jnp.where(mine, y, ys_ref[...])


def _experts(l, meta, xs, wg, wu, wd):
    nmeta = len(meta)
    grid_spec = pltpu.PrefetchScalarGridSpec(
        num_scalar_prefetch=nmeta,
        grid=(N_VISITS,),
        in_specs=[
            pl.BlockSpec((ETILE, D), lambda v, ti, *_: (ti[v], 0)),
            pl.BlockSpec(memory_space=pl.ANY),
            pl.BlockSpec(memory_space=pl.ANY),
            pl.BlockSpec(memory_space=pl.ANY),
        ],
        out_specs=pl.BlockSpec((ETILE, D), lambda v, ti, to, *_: (to[v], 0)),
        scratch_shapes=[pltpu.VMEM((2, D, ED), F32), pltpu.VMEM((2, D, ED), F32),
                        pltpu.VMEM((2, ED, D), F32), pltpu.VMEM((D, ED), BF16),
                        pltpu.VMEM((D, ED), BF16), pltpu.VMEM((ED, D), BF16),
                        pltpu.SemaphoreType.DMA((2,))],
    )
    return pl.pallas_call(
        functools.partial(_expert_kernel, l),
        grid_spec=grid_spec,
        out_shape=jax.ShapeDtypeStruct((NSLOT_MAX, D), BF16),
        compiler_params=pltpu.CompilerParams(
            dimension_semantics=("arbitrary",), vmem_limit_bytes=VMEM_LIMIT),
        name="experts",
    )(*meta, xs, wg, wu, wd)


def _combine_tile(G, r, slot, gf, final):
    nst = TB // G
    xb = r["xn"][...]
    hid = jax.nn.silu(_dot(xb, r["sg"][...])) * _dot(xb, r["su"][...])
    moe = _dot(hid.astype(BF16), r["sd"][...])
    _chunk_waits(
        r["nrows"][pl.program_id(0)],
        pltpu.make_async_copy(r["ys"].at[pl.ds(0, NCH_MIN * ROWS)],
                              r["loc"].at[slot, pl.ds(0, NCH_MIN * ROWS)], r["sem"].at[slot]),
        pltpu.make_async_copy(r["ys"].at[pl.ds(0, ROWS)], r["loc"].at[slot, pl.ds(0, ROWS)],
                              r["sem"].at[slot]))
    lp = r["lpos"][0].astype(I16)
    wts = r["wts"][0].astype(BF16)

    def unsort_rows(c):
        siota = (lax.broadcasted_iota(I32, (SUB, TB), 0) + c * SUB).astype(I16)
        p = jnp.zeros((SUB, TB), BF16)
        for k in range(TOPK):
            p = jnp.where(siota == lp[k:k + 1, :], wts[k:k + 1, :], p)
        return lax.dot_general(
            p, r["loc"][slot, c * SUB:(c + 1) * SUB, :], (((0,), (0,)), ((), ())),
            preferred_element_type=F32)

    def finish(moe):
        xo = r["x_mid"][...].reshape(nst, G, D) + gf[None] * moe.reshape(nst, G, D)
        r["x_out"][...] = xo.reshape(TB, D)
        if final:
            ms = jnp.mean(xo * xo, axis=-1, keepdims=True)
            r["y_out"][...] = (xo * lax.rsqrt(ms + EPS) * r["fin_g"][...]).reshape(TB, D)

    for c in range(LAST):
        moe = moe + unsort_rows(c)
    tail_used = r["nrows"][pl.program_id(0)] > LAST * SUB

    @pl.when(tail_used)
    def _():
        finish(moe + unsort_rows(LAST))

    @pl.when(jnp.logical_not(tail_used))
    def _():
        finish(moe)


def _combine_kernel(final, *refs):
    names = ["tab", "tab_next", "nrows", "ys", "x_mid", "xn", "lpos", "wts", "mod", "sg", "su", "sd"]
    names += ["fin_g"] if final else []
    names += ["x_out"] + (["y_out"] if final else []) + ["loc", "sem"]
    r = dict(zip(names, refs))
    b = pl.program_id(0)
    slot = b % 2

    def gather(tab_ref, s, nrows):
        r["loc"][s, NCH_MIN * ROWS:, :] = jnp.zeros((LS - NCH_MIN * ROWS, D), BF16)
        _chunk_copies(tab_ref, nrows, lambda l, g: pltpu.make_async_copy(
            r["ys"].at[pl.ds(g, ROWS)], r["loc"].at[s, pl.ds(l, ROWS)], r["sem"].at[s]))

    @pl.when(b == 0)
    def _():
        gather(r["tab"], 0, r["nrows"][0])

    @pl.when(b + 1 < NBLK)
    def _():
        gather(r["tab_next"], 1 - slot, r["nrows"][jnp.minimum(b + 1, NBLK - 1)])

    @pl.when(b < PBLK)
    def _():
        _combine_tile(PB, r, slot, r["mod"][0:PB, 5 * D:6 * D], final)

    @pl.when(b >= PBLK)
    def _():
        _combine_tile(SB, r, slot, r["mod"][PB:PB + SB, 5 * D:6 * D], final)


def _combine(tab_c, nrows, ys, x_mid, xn, lpos_c, wts, mod_l, w, fin_g):
    final = fin_g is not None
    nb = PB + SB
    in_specs = [
        pl.BlockSpec((1, 1, NCH), lambda b: (b, 0, 0), memory_space=pltpu.SMEM),
        pl.BlockSpec((1, 1, NCH), lambda b: (jnp.minimum(b + 1, NBLK - 1), 0, 0),
                     memory_space=pltpu.SMEM),
        pl.BlockSpec(memory_space=pltpu.SMEM),
        pl.BlockSpec(memory_space=pl.ANY),
        pl.BlockSpec((TB, D), lambda b: (b, 0)),
        pl.BlockSpec((TB, D), lambda b: (b, 0)),
        pl.BlockSpec((1, TOPK, TB), lambda b: (b, 0, 0)),
        pl.BlockSpec((1, TOPK, TB), lambda b: (b, 0, 0)),
        _const_spec((nb, MOD_COLS)),
        _const_spec((D, ED)),
        _const_spec((D, ED)),
        _const_spec((ED, D)),
    ]
    args = [tab_c, tab_c, nrows, ys, x_mid, xn, lpos_c, wts, mod_l, w["sg"], w["su"], w["sd"]]
    out_shape = [jax.ShapeDtypeStruct((NT, D), F32)]
    out_specs = [pl.BlockSpec((TB, D), lambda b: (b, 0))]
    if final:
        in_specs.append(_const_spec((1, D)))
        args.append(fin_g)
        out_shape.append(jax.ShapeDtypeStruct((NT, D), F32))
        out_specs.append(pl.BlockSpec((TB, D), lambda b: (b, 0)))
    res = pl.pallas_call(
        functools.partial(_combine_kernel, final),
        grid=(NBLK,),
        in_specs=in_specs,
        out_specs=tuple(out_specs),
        out_shape=tuple(out_shape),
        scratch_shapes=[pltpu.VMEM((2, LS, D), BF16), pltpu.SemaphoreType.DMA((2,))],
        compiler_params=pltpu.CompilerParams(
            dimension_semantics=("arbitrary",), vmem_limit_bytes=VMEM_LIMIT),
        name="combine_final" if final else "combine",
    )(*args)
    return res if final else (res[0], None)


def _plan(cnt):
    c = cnt[:, :, :BPT].transpose(0, 2, 1).reshape(NBLK, NE).astype(I32)
    pc = (c + ROWS - 1) // ROWS * ROWS
    lend = jnp.cumsum(pc, axis=1)
    lstart = lend - pc
    etot = jnp.sum(pc, axis=0)
    eend = jnp.cumsum(etot)
    eoff = eend - etot
    gstart = eoff[None, :] + jnp.cumsum(pc, axis=0) - pc
    total = eend[-1]
    eids = jnp.arange(NE, dtype=I32)

    l8 = jnp.arange(NCH, dtype=I32) * ROWS
    valid = l8[None, :] < lend[:, -1:]
    ej = jnp.minimum(jnp.sum((lend[:, None, :] <= l8[None, :, None]).astype(I32), axis=-1), NE - 1)
    hot = ej[..., None] == eids
    row = jnp.sum(jnp.where(hot, (gstart - lstart)[:, None, :], 0), axis=-1) + l8[None, :]
    spare = NSLOT_MAX + (jnp.arange(NBLK, dtype=I32) % 2)[:, None] * LS + l8[None, :]
    tab_d = jnp.where(valid, row, spare).reshape(NBLK, 1, NCH)
    tab_c = jnp.where(valid, row, 0).reshape(NBLK, 1, NCH)

    def lookup(table, idx):
        return jnp.sum(jnp.where(idx[..., None] == eids, table, 0), axis=-1)

    first_tile = eoff // ETILE
    last_tile = (eend - 1) // ETILE
    nvis = jnp.where(etot > 0, last_tile - first_tile + 1, 0)
    vend = jnp.cumsum(nvis)
    nvalid = vend[-1]
    v = jnp.arange(N_VISITS, dtype=I32)
    vv = jnp.minimum(v, nvalid - 1)
    e = jnp.minimum(jnp.sum((vend[None, :] <= vv[:, None]).astype(I32), axis=1), NE - 1)
    off_e = lookup(eoff, e)
    tile = lookup(first_tile, e) + (vv - lookup(vend - nvis, e))
    base = tile * ETILE
    lo = jnp.clip(off_e - base, 0, ETILE)
    hi = jnp.clip(lookup(eend, e) - base, 0, ETILE)
    used = (total + ETILE - 1) // ETILE
    spare_tile = used + (v - nvalid)
    tout = jnp.where(v < nvalid, tile, jnp.minimum(spare_tile, N_ETILES - 1))
    flag = jnp.where(v < nvalid, jnp.where(off_e <= base, 2, 1),
                     jnp.where(spare_tile < N_ETILES, 3, 0)).astype(I32)
    new = jnp.concatenate([jnp.ones((1,), I32), (e[1:] != e[:-1]).astype(I32)])
    live = etot > 0
    wslot = lookup((jnp.cumsum(live.astype(I32)) - 1) % 2, e)
    later = live[None, :] & (eids[None, :] > eids[:, None])
    nxt = jnp.min(jnp.where(later, eids[None, :], NE), axis=1)
    nxt = lookup(jnp.where(nxt < NE, nxt, -1), e)
    meta = (tile.astype(I32), tout.astype(I32), e, lo.astype(I32), hi.astype(I32), flag, new,
            wslot.astype(I32), nxt.astype(I32))
    return tab_d, tab_c, jnp.stack([used, total]).astype(I32), lend[:, -1], meta


def _block_diag(wh):
    out = jnp.zeros((2, 4, HEAD, 4, HEAD), wh.dtype)
    for hh in range(4):
        out = out.at[:, hh, :, hh, :].set(wh.reshape(2, 4, HEAD, HEAD)[:, hh])
    return out.reshape(2, 4 * HEAD, 4 * HEAD)


def _layer_weights(l, p):
    tril = jnp.tril(jnp.ones((CHUNK, CHUNK), bool))
    wt = jnp.where(tril, p["sgu_w"][l], 0.0)
    wpair = wt.reshape(4, 2, CHUNK, CHUNK).transpose(0, 2, 1, 3).reshape(4, CHUNK, 2 * CHUNK)
    bmap = jnp.repeat(p["sgu_b"][l].T, HEAD, axis=1)
    w8 = jnp.where(tril[:DEC_SEQ, :DEC_SEQ], p["sgu_w"][l][:, :DEC_SEQ, :DEC_SEQ], 0.0)
    coef = jnp.repeat(w8.transpose(2, 1, 0), HEAD, axis=2)
    bias8 = jnp.repeat(p["sgu_b"][l][:, :DEC_SEQ].T, HEAD, axis=1)
    gid = jnp.arange(256) // HEAD
    tok = jnp.arange(TILE)
    same_blk = (tok[:, None] // TB) == (tok[None, :] // TB)
    return dict(
        norm_mix_g=p["norm_mix_g"][l][None], norm_ffn_g=p["norm_ffn_g"][l][None],
        w_in=p["w_in"][l].astype(BF16), w_out=p["w_out"][l].astype(BF16),
        conv_w=p["conv_w"][l], conv_b=p["conv_b"][l][None],
        wr=_block_diag(p["gate_r_w"][l]).astype(BF16), wi=_block_diag(p["gate_i_w"][l]).astype(BF16),
        br=p["gate_r_b"][l][None], bi=p["gate_i_b"][l][None], lam=p["lru_lambda"][l][None],
        sgu_g=p["sgu_norm_g"][l][None], wpair=wpair.astype(BF16), bmap=bmap, coef=coef, bias8=bias8,
        out_g=p["out_norm_g"][l][None],
        bd=(gid[:, None] == gid[None, :]).astype(BF16),
        rw_t=p["router_w"][l].T.astype(BF16), rbias=p["router_bias"][l][:, None],
        tri=(same_blk & (tok[:, None] < tok[None, :])).astype(BF16),
        bsel=((tok[:, None] // TB) == jnp.arange(128)[None, :]).astype(BF16),
        ltri=(jnp.arange(NE)[:, None] > jnp.arange(NE)[None, :]).astype(BF16),
        sg=p["shared_w_gate"][l].astype(BF16), su=p["shared_w_up"][l].astype(BF16),
        sd=p["shared_w_down"][l].astype(BF16),
    )


def kernel(x_prompt, x_sample, c_prompt, c_sample, state_lru_h, state_conv, ada_w, ada_b, norm_mix_g, norm_ffn_g, w_in, conv_w, conv_b, gate_r_w, gate_r_b, gate_i_w, gate_i_b, lru_lambda, sgu_norm_g, sgu_w, sgu_b, out_norm_g, w_out, router_w, router_bias, exp_w_gate, exp_w_up, exp_w_down, shared_w_gate, shared_w_up, shared_w_down, final_norm_g):
    p = dict(norm_mix_g=norm_mix_g, norm_ffn_g=norm_ffn_g, w_in=w_in, conv_w=conv_w, conv_b=conv_b,
             gate_r_w=gate_r_w, gate_r_b=gate_r_b, gate_i_w=gate_i_w, gate_i_b=gate_i_b,
             lru_lambda=lru_lambda, sgu_norm_g=sgu_norm_g, sgu_w=sgu_w, sgu_b=sgu_b,
             out_norm_g=out_norm_g, w_out=w_out, router_w=router_w, router_bias=router_bias,
             shared_w_gate=shared_w_gate, shared_w_up=shared_w_up, shared_w_down=shared_w_down)
    x_all = jnp.concatenate([x_prompt.transpose(1, 0, 2).reshape(NP, D),
                             x_sample.transpose(1, 0, 2).reshape(NS, D)], axis=0)
    mod = _modulations(jnp.concatenate([c_prompt, c_sample], axis=0), ada_w, ada_b)

    hp, cp, hs, cs, vs = [], [], [], [], []
    y_all = None
    for l in range(DEPTH):
        w = _layer_weights(l, p)
        w["h0s"] = state_lru_h[l]
        w["convs"] = state_conv[l].transpose(1, 0, 2).reshape((CONV_W - 1) * SB, LW)
        yn, hlp, cvp, hls, cvs, v_s = _mixer(x_all, mod[l], w)
        x_mid, xn, lpos_t, w_t, cnt = _router(x_all, yn, mod[l], w)
        tab_d, tab_c, used, nrows, meta = _plan(cnt)
        def per_block(a):
            return a.reshape(N_TILES, TOPK, BPT, TB).transpose(0, 2, 1, 3).reshape(NBLK, TOPK, TB)

        lpos_d = per_block(lpos_t)
        xs = _dispatch(tab_d, used, nrows, lpos_d, xn)
        ys = _experts(l, meta, xs, exp_w_gate, exp_w_up, exp_w_down)
        fin = final_norm_g[None] if l == DEPTH - 1 else None
        x_all, y_all = _combine(tab_c, nrows, ys, x_mid, xn, lpos_d, per_block(w_t), mod[l], w, fin)
        hp.append(hlp)
        cp.append(cvp.reshape(CONV_W - 1, PB, LW).transpose(1, 0, 2))
        hs.append(hls)
        cs.append(cvs.reshape(CONV_W - 1, SB, LW).transpose(1, 0, 2))
        vs.append(v_s.reshape(DEC_SEQ, SB, SW).transpose(1, 0, 2))

    y_prompt = y_all[:NP].reshape(SEQ, PB, D).transpose(1, 0, 2)
    y_sample = y_all[NP:].reshape(DEC_SEQ, SB, D).transpose(1, 0, 2)
    return (y_prompt, y_sample, jnp.stack(hp), jnp.stack(cp), jnp.stack(hs), jnp.stack(cs),
            jnp.stack(vs))
```

```python
import functools

import jax
import jax.numpy as jnp
from jax import lax
from jax.experimental import pallas as pl
from jax.experimental.pallas import tpu as pltpu

F32 = jnp.float32
BF16 = jnp.bfloat16
I32 = jnp.int32
I16 = jnp.int16

D = 1024
DEPTH = 4
PB = 8
SEQ = 2048
SB = 128
DEC_SEQ = 8
NP = PB * SEQ
NS = SB * DEC_SEQ
NT = NP + NS
TILE = 1024
N_TILES = NT // TILE
P_TILES = NP // TILE
LW = 512
SW = 512
HEAD = 64
CHUNK = 128
CONV_W = 4
LRU_C = 8.0
NE = 64
TOPK = 8
ED = 256
ROUTED_SCALE = 2.5
EPS = 1e-6
MOD_COLS = 6 * D
MOD_BLK = 1536
VMEM_LIMIT = 56 * 1024 * 1024

TB = 256
NBLK = NT // TB
PBLK = NP // TB
BPT = TILE // TB
ROWS = 16
LS = 3072
NCH = LS // ROWS
NCH_MIN = TB * TOPK // ROWS
SUB = 512
LAST = LS // SUB - 1
ETILE = 512
NSLOT_MAX = -(-(NT * TOPK + NBLK * NE * (ROWS - 1) + NE * (ETILE - ROWS)) // ETILE) * ETILE
N_ETILES = NSLOT_MAX // ETILE
TAB_TWO, TAB_THREE, TAB_ONE = 0, NE, 2 * NE
TAB_COUNTS = 2 * NE + NCH
TABW = TAB_COUNTS + 8
COPY_CLASSES = ((TAB_TWO, 2), (TAB_THREE, 3), (TAB_ONE, 1))

assert NSLOT_MAX % ETILE == 0 and LS >= TB * TOPK + NE * (ROWS - 1) and LS % SUB == 0


def _dot(a, b):
    return jnp.dot(a, b, preferred_element_type=F32)


def _const_spec(shape):
    nd = len(shape)
    return pl.BlockSpec(shape, lambda *_: (0,) * nd)


def _mod_kernel(c_ref, w_ref, b_ref, o_ref):
    s = jax.nn.silu(c_ref[...]).astype(BF16)
    o_ref[0] = _dot(s, w_ref[0].astype(BF16)) + b_ref[0]


def _modulations(c_all, ada_w, ada_b):
    nb = c_all.shape[0]
    return pl.pallas_call(
        _mod_kernel,
        grid=(DEPTH, MOD_COLS // MOD_BLK),
        in_specs=[
            pl.BlockSpec((nb, D), lambda l, j: (0, 0)),
            pl.BlockSpec((1, D, MOD_BLK), lambda l, j: (l, 0, j)),
            pl.BlockSpec((1, 1, MOD_BLK), lambda l, j: (l, 0, j)),
        ],
        out_specs=pl.BlockSpec((1, nb, MOD_BLK), lambda l, j: (l, 0, j)),
        out_shape=jax.ShapeDtypeStruct((DEPTH, nb, MOD_COLS), F32),
        compiler_params=pltpu.CompilerParams(
            dimension_semantics=("arbitrary", "arbitrary"), vmem_limit_bytes=VMEM_LIMIT),
        name="adaln_mod",
    )(c_all, ada_w, ada_b.reshape(DEPTH, 1, MOD_COLS))


def _sgu_prompt(v, wpair_ref, bmap_ref, vbuf, sbuf):
    for j in range(4):
        vbuf[j] = v[:, 128 * j:128 * (j + 1)]
    lane = lax.broadcasted_iota(I32, (CHUNK, 128), 1)
    left = lane < HEAD
    for b in range(PB):
        for j in range(4):
            vp = vbuf[j, pl.ds(b, CHUNK, stride=PB), :]
            rhs = jnp.concatenate(
                [jnp.where(left, vp, 0.0), jnp.where(left, 0.0, vp)], axis=0).astype(BF16)
            sj = _dot(wpair_ref[j], rhs) + bmap_ref[:, 128 * j:128 * (j + 1)]
            sbuf[j, pl.ds(b, CHUNK, stride=PB), :] = sj
    return jnp.concatenate([sbuf[j] for j in range(4)], axis=1)


def _sgu_sample(v, coef_ref, bias_ref):
    vt = [v[t * SB:(t + 1) * SB] for t in range(DEC_SEQ)]
    rows = []
    for p in range(DEC_SEQ):
        acc = bias_ref[p:p + 1, :] + coef_ref[0, p:p + 1, :] * vt[0]
        for q in range(1, p + 1):
            acc = acc + coef_ref[q, p:p + 1, :] * vt[q]
        rows.append(acc)
    return jnp.concatenate(rows, axis=0)


def _mixer_tile(G, x, sh, sc, h_in, tail_in, sgu_fn, r, hbuf):
    nst = TILE // G
    x3 = x.reshape(nst, G, D)
    ms = jnp.mean(x3 * x3, axis=-1, keepdims=True)
    xn = x3 * lax.rsqrt(ms + EPS) * r["norm_g"][...]
    xn = (xn * (1.0 + sc[None]) + sh[None]).reshape(TILE, D).astype(BF16)

    xa = _dot(xn, r["w_in"][:, 0:LW])
    xp = jnp.concatenate([tail_in, xa], axis=0)
    cw = r["conv_w"]
    xc = r["conv_b"][...] + cw[0:1, :] * xp[0:TILE]
    for k in range(1, CONV_W):
        xc = xc + cw[k:k + 1, :] * xp[k * G:k * G + TILE]
    new_tail = xa[TILE - (CONV_W - 1) * G:]
    xcb = xc.astype(BF16)
    half = LW // 2
    r_pre = jnp.concatenate(
        [_dot(xcb[:, :half], r["wr"][0]), _dot(xcb[:, half:], r["wr"][1])], axis=1) + r["br"][...]
    i_pre = jnp.concatenate(
        [_dot(xcb[:, :half], r["wi"][0]), _dot(xcb[:, half:], r["wi"][1])], axis=1) + r["bi"][...]
    rg = jax.nn.sigmoid(r_pre)
    ig = jax.nn.sigmoid(i_pre)
    log_a = -LRU_C * rg * jax.nn.softplus(-r["lam"][...])
    a = jnp.exp(log_a)
    th = jnp.tanh(log_a)
    num = -2.0 * th
    mult = jnp.where(num > 0.0, num * lax.rsqrt(num * (1.0 - th)), 0.0)
    u = mult * (ig * xc)
    h = h_in
    for s in range(nst):
        h = a[s * G:(s + 1) * G] * h + u[s * G:(s + 1) * G]
        hbuf[s * G:(s + 1) * G, :] = h
    ya = hbuf[...] * jax.nn.gelu(_dot(xn, r["w_in"][:, LW:2 * LW]))

    ug = jax.nn.gelu(_dot(xn, r["w_in"][:, 2 * LW:2 * LW + SW]))
    vg = jax.nn.gelu(_dot(xn, r["w_in"][:, 2 * LW + SW:]))
    v = vg * lax.rsqrt(jnp.mean(vg * vg, axis=-1, keepdims=True) + EPS) * r["sgu_g"][...]
    yb = ug * sgu_fn(v)

    y = jnp.concatenate([ya, yb], axis=1)
    bd = r["bd"][...]
    outs = []
    for j in range(D // 256):
        ysl = y[:, 256 * j:256 * (j + 1)]
        sq = ysl * ysl
        hi = sq.astype(BF16)
        lo = (sq - hi.astype(F32)).astype(BF16)
        msq = (_dot(hi, bd) + _dot(lo, bd)) * (1.0 / HEAD)
        outs.append(ysl * lax.rsqrt(msq + EPS) * r["out_g"][:, 256 * j:256 * (j + 1)])
    yn = jnp.concatenate(outs, axis=1).astype(BF16)
    return yn, h, new_tail, v


_MIX_IN = ("x", "mod", "norm_g", "w_in", "conv_w", "conv_b", "wr", "wi", "br", "bi", "lam",
           "sgu_g", "wpair", "bmap", "coef", "bias8", "out_g", "bd", "h0s", "convs")
_MIX_OUT = ("y", "hlast_p", "conv_p", "hlast_s", "conv_s", "v_s")
_MIX_SCR = ("hbuf", "vbuf", "sbuf", "h_carry", "tail_carry")


def _mixer_kernel(*refs):
    names = _MIX_IN + _MIX_OUT + _MIX_SCR
    r = dict(zip(names, refs))
    i = pl.program_id(0)

    @pl.when(i == 0)
    def _():
        r["h_carry"][...] = jnp.zeros((PB, LW), F32)
        r["tail_carry"][...] = jnp.zeros(((CONV_W - 1) * PB, LW), F32)

    @pl.when(i < P_TILES)
    def _():
        sh = r["mod"][0:PB, 0:D]
        sc = r["mod"][0:PB, D:2 * D]
        sgu = functools.partial(_sgu_prompt, wpair_ref=r["wpair"], bmap_ref=r["bmap"],
                                vbuf=r["vbuf"], sbuf=r["sbuf"])
        yn, h, tail, _ = _mixer_tile(PB, r["x"][...], sh, sc, r["h_carry"][...],
                                     r["tail_carry"][...], sgu, r, r["hbuf"])
        r["y"][...] = yn
        r["h_carry"][...] = h
        r["tail_carry"][...] = tail
        r["hlast_p"][...] = h
        r["conv_p"][...] = tail

    @pl.when(i == P_TILES)
    def _():
        sh = r["mod"][PB:PB + SB, 0:D]
        sc = r["mod"][PB:PB + SB, D:2 * D]
        sgu = functools.partial(_sgu_sample, coef_ref=r["coef"], bias_ref=r["bias8"])
        yn, h, tail, v = _mixer_tile(SB, r["x"][...], sh, sc, r["h0s"][...],
                                     r["convs"][...], sgu, r, r["hbuf"])
        r["y"][...] = yn
        r["hlast_s"][...] = h
        r["conv_s"][...] = tail
        r["v_s"][...] = v


def _mixer(x_all, mod_l, w):
    nb = PB + SB
    in_specs = [
        pl.BlockSpec((TILE, D), lambda i: (i, 0)),
        _const_spec((nb, MOD_COLS)),
        _const_spec((1, D)),
        _const_spec((D, 2 * LW + 2 * SW)),
        _const_spec((CONV_W, LW)),
        _const_spec((1, LW)),
        _const_spec((2, LW // 2, LW // 2)),
        _const_spec((2, LW // 2, LW // 2)),
        _const_spec((1, LW)),
        _const_spec((1, LW)),
        _const_spec((1, LW)),
        _const_spec((1, SW)),
        _const_spec((4, CHUNK, 2 * CHUNK)),
        _const_spec((CHUNK, SW)),
        _const_spec((DEC_SEQ, DEC_SEQ, SW)),
        _const_spec((DEC_SEQ, SW)),
        _const_spec((1, D)),
        _const_spec((256, 256)),
        _const_spec((SB, LW)),
        _const_spec(((CONV_W - 1) * SB, LW)),
    ]
    out_shape = (
        jax.ShapeDtypeStruct((NT, D), BF16),
        jax.ShapeDtypeStruct((PB, LW), F32),
        jax.ShapeDtypeStruct(((CONV_W - 1) * PB, LW), F32),
        jax.ShapeDtypeStruct((SB, LW), F32),
        jax.ShapeDtypeStruct(((CONV_W - 1) * SB, LW), F32),
        jax.ShapeDtypeStruct((NS, SW), F32),
    )
    out_specs = (
        pl.BlockSpec((TILE, D), lambda i: (i, 0)),
        _const_spec((PB, LW)),
        _const_spec(((CONV_W - 1) * PB, LW)),
        _const_spec((SB, LW)),
        _const_spec(((CONV_W - 1) * SB, LW)),
        _const_spec((NS, SW)),
    )
    scratch = [
        pltpu.VMEM((TILE, LW), F32),
        pltpu.VMEM((4, TILE, 128), F32),
        pltpu.VMEM((4, TILE, 128), F32),
        pltpu.VMEM((PB, LW), F32),
        pltpu.VMEM(((CONV_W - 1) * PB, LW), F32),
    ]
    return pl.pallas_call(
        _mixer_kernel,
        grid=(N_TILES,),
        in_specs=in_specs,
        out_specs=out_specs,
        out_shape=out_shape,
        scratch_shapes=scratch,
        compiler_params=pltpu.CompilerParams(
            dimension_semantics=("arbitrary",), vmem_limit_bytes=VMEM_LIMIT),
        name="mixer",
    )(x_all, mod_l, w["norm_mix_g"], w["w_in"], w["conv_w"], w["conv_b"], w["wr"], w["wi"],
      w["br"], w["bi"], w["lam"], w["sgu_g"], w["wpair"], w["bmap"], w["coef"], w["bias8"],
      w["out_g"], w["bd"], w["h0s"], w["convs"])


def _router_tile(G, r, gm, sc, sh):
    nst = TILE // G
    out = _dot(r["y"][...], r["w_out"][...])
    xm3 = r["x"][...].reshape(nst, G, D) + gm[None] * out.reshape(nst, G, D)
    r["x_mid"][...] = xm3.reshape(TILE, D)
    ms = jnp.mean(xm3 * xm3, axis=-1, keepdims=True)
    xn3 = xm3 * lax.rsqrt(ms + EPS) * r["norm_g"][...]
    xn = (xn3 * (1.0 + sc[None]) + sh[None]).reshape(TILE, D).astype(BF16)
    r["xn"][...] = xn

    logits = lax.dot_general(r["rw_t"][...], xn, (((1,), (1,)), ((), ())),
                             preferred_element_type=F32)
    sig = jax.nn.sigmoid(logits)
    sel = sig + r["rbias"][...]
    eiota = lax.broadcasted_iota(I32, (NE, TILE), 0)
    chosen = jnp.zeros((NE, TILE), F32)
    w_rows, hots = [], []
    for _ in range(TOPK):
        m = jnp.max(sel, axis=0, keepdims=True)
        idx = jnp.min(jnp.where(sel == m, eiota, NE), axis=0, keepdims=True)
        hot = eiota == idx
        hots.append(hot)
        w_rows.append(jnp.sum(jnp.where(hot, sig, 0.0), axis=0, keepdims=True))
        chosen = jnp.where(hot, 1.0, chosen)
        sel = jnp.where(hot, -jnp.inf, sel)
    den = w_rows[0]
    for k in range(1, TOPK):
        den = den + w_rows[k]
    scale = ROUTED_SCALE / den
    r["w_t"][0] = jnp.concatenate([wk * scale for wk in w_rows], axis=0)

    cb = chosen.astype(BF16)
    rank = _dot(cb, r["tri"][...])
    cnt = _dot(cb, r["bsel"][...])
    groups = jnp.floor((cnt + (ROWS - 1.0)) * (1.0 / ROWS))
    start = ROWS * _dot(r["ltri"][...], groups.astype(BF16))
    pos = rank + jnp.concatenate(
        [jnp.broadcast_to(start[:, j:j + 1], (NE, TB)) for j in range(BPT)], axis=1)
    lp_rows = [jnp.sum(jnp.where(hot, pos, 0.0), axis=0, keepdims=True) for hot in hots]
    r["lpos_t"][0] = jnp.concatenate(lp_rows, axis=0).astype(I32)
    r["cnt"][0] = cnt


_RT_IN = ("x", "y", "mod", "w_out", "norm_g", "rw_t", "rbias", "tri", "bsel", "ltri")
_RT_OUT = ("x_mid", "xn", "lpos_t", "w_t", "cnt")


def _router_kernel(*refs):
    r = dict(zip(_RT_IN + _RT_OUT, refs))
    i = pl.program_id(0)

    def mods(lo, n):
        m = r["mod"]
        return m[lo:lo + n, 2 * D:3 * D], m[lo:lo + n, 4 * D:5 * D], m[lo:lo + n, 3 * D:4 * D]

    @pl.when(i < P_TILES)
    def _():
        _router_tile(PB, r, *mods(0, PB))

    @pl.when(i == P_TILES)
    def _():
        _router_tile(SB, r, *mods(PB, SB))


def _router(x_all, y_all, mod_l, w):
    nb = PB + SB
    in_specs = [
        pl.BlockSpec((TILE, D), lambda i: (i, 0)),
        pl.BlockSpec((TILE, D), lambda i: (i, 0)),
        _const_spec((nb, MOD_COLS)),
        _const_spec((D, D)),
        _const_spec((1, D)),
        _const_spec((NE, D)),
        _const_spec((NE, 1)),
        _const_spec((TILE, TILE)),
        _const_spec((TILE, 128)),
        _const_spec((NE, NE)),
    ]
    out_shape = (
        jax.ShapeDtypeStruct((NT, D), F32),
        jax.ShapeDtypeStruct((NT, D), BF16),
        jax.ShapeDtypeStruct((N_TILES, TOPK, TILE), I32),
        jax.ShapeDtypeStruct((N_TILES, TOPK, TILE), F32),
        jax.ShapeDtypeStruct((N_TILES, NE, 128), F32),
    )
    out_specs = (
        pl.BlockSpec((TILE, D), lambda i: (i, 0)),
        pl.BlockSpec((TILE, D), lambda i: (i, 0)),
        pl.BlockSpec((1, TOPK, TILE), lambda i: (i, 0, 0)),
        pl.BlockSpec((1, TOPK, TILE), lambda i: (i, 0, 0)),
        pl.BlockSpec((1, NE, 128), lambda i: (i, 0, 0)),
    )
    return pl.pallas_call(
        _router_kernel,
        grid=(N_TILES,),
        in_specs=in_specs,
        out_specs=out_specs,
        out_shape=out_shape,
        compiler_params=pltpu.CompilerParams(
            dimension_semantics=("arbitrary",), vmem_limit_bytes=VMEM_LIMIT),
        name="router",
    )(x_all, y_all, mod_l, w["w_out"], w["norm_ffn_g"], w["rw_t"], w["rbias"], w["tri"],
      w["bsel"], w["ltri"])


def _group_copies(tab_ref, make_copy):
    for cls, (base, nch) in enumerate(COPY_CLASSES):
        def body(i, carry, base=base, nch=nch, cls=cls):
            p = tab_ref[0, 0, base + i]
            l = pl.multiple_of((p & 255) * ROWS, ROWS)
            g = pl.multiple_of((p >> 8) * ROWS, ROWS)
            make_copy(l, g, nch * ROWS).start(priority=cls % 2)
            return carry

        lax.fori_loop(0, tab_ref[0, 0, TAB_COUNTS + cls], body, 0)


def _chunk_waits(nrows, whole_copy, chunk_copy):
    whole_copy.wait()

    def body(i, carry):
        chunk_copy.wait()
        return carry

    lax.fori_loop(0, (nrows - NCH_MIN * ROWS) // ROWS, body, 0)


def _dispatch_kernel(tab_ref, fill_ref, nrows_ref, lpos_ref, xn_ref, xs_ref, loc, zbuf, sem):
    b = pl.program_id(0)
    slot = b % 2
    xb = xn_ref[...]
    lp = lpos_ref[0].astype(I16)
    one = jnp.ones((), BF16)

    def sort_rows(c):
        siota = (lax.broadcasted_iota(I32, (SUB, TB), 0) + c * SUB).astype(I16)
        p = jnp.zeros((SUB, TB), BF16)
        for k in range(TOPK):
            p = jnp.where(siota == lp[k:k + 1, :], one, p)
        loc[slot, c * SUB:(c + 1) * SUB, :] = _dot(p, xb).astype(BF16)

    for c in range(LAST):
        sort_rows(c)
    tail_used = nrows_ref[b] > LAST * SUB

    @pl.when(tail_used)
    def _():
        sort_rows(LAST)

    @pl.when(jnp.logical_not(tail_used))
    def _():
        loc[slot, LAST * SUB:, :] = jnp.zeros((SUB, D), BF16)

    _group_copies(tab_ref, lambda l, g, n: pltpu.make_async_copy(
        loc.at[slot, pl.ds(l, n)], xs_ref.at[pl.ds(g, n)], sem.at[slot]))

    def drain(s, nrows):
        _chunk_waits(
            nrows,
            pltpu.make_async_copy(loc.at[s, pl.ds(0, NCH_MIN * ROWS)],
                                  xs_ref.at[pl.ds(0, NCH_MIN * ROWS)], sem.at[s]),
            pltpu.make_async_copy(loc.at[s, pl.ds(0, ROWS)], xs_ref.at[pl.ds(0, ROWS)], sem.at[s]))

    @pl.when(b > 0)
    def _():
        drain(1 - slot, nrows_ref[jnp.maximum(b - 1, 0)])

    @pl.when(b == NBLK - 1)
    def _():
        drain(slot, nrows_ref[b])
        zbuf[...] = jnp.zeros((ETILE, D), BF16)

        def gap_chunk(e, j):
            row = pl.multiple_of(fill_ref[1 + e] + j * ROWS, ROWS)
            return pltpu.make_async_copy(
                zbuf.at[pl.ds(0, ROWS)], xs_ref.at[pl.ds(row, ROWS)], sem.at[1])

        def start_gap(e, carry):
            lax.fori_loop(0, fill_ref[1 + NE + e],
                          lambda j, c: (gap_chunk(e, j).start(), c)[1], 0)
            return carry

        def wait_gap(e, carry):
            lax.fori_loop(0, fill_ref[1 + NE + e],
                          lambda j, c: (gap_chunk(e, j).wait(), c)[1], 0)
            return carry

        lax.fori_loop(0, NE, start_gap, 0)
        lax.fori_loop(0, NE, wait_gap, 0)

        def zero_tile(t):
            return pltpu.make_async_copy(
                zbuf, xs_ref.at[pl.ds(pl.multiple_of(t * ETILE, ETILE), ETILE)], sem.at[0])

        def start(t, carry):
            zero_tile(t).start()
            return carry

        def wait(t, carry):
            zero_tile(t).wait()
            return carry

        lax.fori_loop(fill_ref[0], N_ETILES, start, 0)
        lax.fori_loop(fill_ref[0], N_ETILES, wait, 0)


def _dispatch(tab, fill_from, nrows, lpos_d, xn):
    return pl.pallas_call(
        _dispatch_kernel,
        grid=(NBLK,),
        in_specs=[
            pl.BlockSpec((1, 1, TABW), lambda b: (b, 0, 0), memory_space=pltpu.SMEM),
            pl.BlockSpec(memory_space=pltpu.SMEM),
            pl.BlockSpec(memory_space=pltpu.SMEM),
            pl.BlockSpec((1, TOPK, TB), lambda b: (b, 0, 0)),
            pl.BlockSpec((TB, D), lambda b: (b, 0)),
        ],
        out_specs=pl.BlockSpec(memory_space=pl.ANY),
        out_shape=jax.ShapeDtypeStruct((NSLOT_MAX, D), BF16),
        scratch_shapes=[pltpu.VMEM((2, LS, D), BF16), pltpu.VMEM((ETILE, D), BF16),
                        pltpu.SemaphoreType.DMA((2,))],
        compiler_params=pltpu.CompilerParams(
            dimension_semantics=("arbitrary",), vmem_limit_bytes=VMEM_LIMIT),
        name="dispatch",
    )(tab, fill_from, nrows, lpos_d, xn)


def _expert_kernel(l, tin_ref, exp_ref, flag_ref, new_ref, wslot_ref, next_ref, xs_ref, wg_hbm,
                   wu_hbm, wd_hbm, ys_ref, wg_f, wu_f, wd_f, wg_b, wu_b, wd_b, wsem):
    v = pl.program_id(0)
    flag = flag_ref[v]

    def weight_copies(e, p):
        return (pltpu.make_async_copy(wg_hbm.at[l, e], wg_f.at[p], wsem.at[p]),
                pltpu.make_async_copy(wu_hbm.at[l, e], wu_f.at[p], wsem.at[p]),
                pltpu.make_async_copy(wd_hbm.at[l, e], wd_f.at[p], wsem.at[p]))

    @pl.when(v == 0)
    def _():
        for c in weight_copies(exp_ref[0], 0):
            c.start()

    @pl.when(flag == 0)
    def _():
        ys_ref[...] = jnp.zeros((ETILE, D), BF16)

    @pl.when(new_ref[v] == 1)
    def _():
        p = wslot_ref[v]
        for c in weight_copies(exp_ref[v], p):
            c.wait()

        @pl.when(next_ref[v] >= 0)
        def _():
            for c in weight_copies(next_ref[v], 1 - p):
                c.start()

        wg_b[...] = wg_f[p].astype(BF16)
        wu_b[...] = wu_f[p].astype(BF16)
        wd_b[...] = wd_f[p].astype(BF16)

    @pl.when(flag == 1)
    def _():
        x = xs_ref[...]
        hid = jax.nn.silu(_dot(x, wg_b[...])) * _dot(x, wu_b[...])
        ys_ref[...] = _dot(hid.astype(BF16), wd_b[...]).astype(BF16)


def _experts(l, meta, xs, wg, wu, wd):
    nmeta = len(meta)
    grid_spec = pltpu.PrefetchScalarGridSpec(
        num_scalar_prefetch=nmeta,
        grid=(N_ETILES,),
        in_specs=[
            pl.BlockSpec((ETILE, D), lambda v, ti, *_: (ti[v], 0)),
            pl.BlockSpec(memory_space=pl.ANY),
            pl.BlockSpec(memory_space=pl.ANY),
            pl.BlockSpec(memory_space=pl.ANY),
        ],
        out_specs=pl.BlockSpec((ETILE, D), lambda v, *_: (v, 0)),
        scratch_shapes=[pltpu.VMEM((2, D, ED), F32), pltpu.VMEM((2, D, ED), F32),
                        pltpu.VMEM((2, ED, D), F32), pltpu.VMEM((D, ED), BF16),
                        pltpu.VMEM((D, ED), BF16), pltpu.VMEM((ED, D), BF16),
                        pltpu.SemaphoreType.DMA((2,))],
    )
    return pl.pallas_call(
        functools.partial(_expert_kernel, l),
        grid_spec=grid_spec,
        out_shape=jax.ShapeDtypeStruct((NSLOT_MAX, D), BF16),
        compiler_params=pltpu.CompilerParams(
            dimension_semantics=("arbitrary",), vmem_limit_bytes=VMEM_LIMIT),
        name="experts",
    )(*meta, xs, wg, wu, wd)


def _combine_tile(G, r, slot, gf, final):
    nst = TB // G
    xb = r["xn"][...]
    hid = jax.nn.silu(_dot(xb, r["sg"][...])) * _dot(xb, r["su"][...])
    moe = _dot(hid.astype(BF16), r["sd"][...])
    _chunk_waits(
        r["nrows"][pl.program_id(0)],
        pltpu.make_async_copy(r["ys"].at[pl.ds(0, NCH_MIN * ROWS)],
                              r["loc"].at[slot, pl.ds(0, NCH_MIN * ROWS)], r["sem"].at[slot]),
        pltpu.make_async_copy(r["ys"].at[pl.ds(0, ROWS)], r["loc"].at[slot, pl.ds(0, ROWS)],
                              r["sem"].at[slot]))
    lp = r["lpos"][0].astype(I16)
    wts = r["wts"][0].astype(BF16)

    def unsort_rows(c):
        siota = (lax.broadcasted_iota(I32, (SUB, TB), 0) + c * SUB).astype(I16)
        p = jnp.zeros((SUB, TB), BF16)
        for k in range(TOPK):
            p = jnp.where(siota == lp[k:k + 1, :], wts[k:k + 1, :], p)
        return lax.dot_general(
            p, r["loc"][slot, c * SUB:(c + 1) * SUB, :], (((0,), (0,)), ((), ())),
            preferred_element_type=F32)

    def finish(moe):
        xo = r["x_mid"][...].reshape(nst, G, D) + gf[None] * moe.reshape(nst, G, D)
        r["x_out"][...] = xo.reshape(TB, D)
        if final:
            ms = jnp.mean(xo * xo, axis=-1, keepdims=True)
            r["y_out"][...] = (xo * lax.rsqrt(ms + EPS) * r["fin_g"][...]).reshape(TB, D)

    for c in range(LAST):
        moe = moe + unsort_rows(c)
    tail_used = r["nrows"][pl.program_id(0)] > LAST * SUB

    @pl.when(tail_used)
    def _():
        finish(moe + unsort_rows(LAST))

    @pl.when(jnp.logical_not(tail_used))
    def _():
        finish(moe)


def _combine_kernel(final, *refs):
    names = ["tab", "tab_next", "nrows", "ys", "x_mid", "xn", "lpos", "wts", "mod", "sg", "su", "sd"]
    names += ["fin_g"] if final else []
    names += ["x_out"] + (["y_out"] if final else []) + ["loc", "sem"]
    r = dict(zip(names, refs))
    b = pl.program_id(0)
    slot = b % 2

    def gather(tab_ref, s):
        r["loc"][s, NCH_MIN * ROWS:, :] = jnp.zeros((LS - NCH_MIN * ROWS, D), BF16)
        _group_copies(tab_ref, lambda l, g, n: pltpu.make_async_copy(
            r["ys"].at[pl.ds(g, n)], r["loc"].at[s, pl.ds(l, n)], r["sem"].at[s]))

    @pl.when(b == 0)
    def _():
        gather(r["tab"], 0)

    @pl.when(b + 1 < NBLK)
    def _():
        gather(r["tab_next"], 1 - slot)

    @pl.when(b < PBLK)
    def _():
        _combine_tile(PB, r, slot, r["mod"][0:PB, 5 * D:6 * D], final)

    @pl.when(b >= PBLK)
    def _():
        _combine_tile(SB, r, slot, r["mod"][PB:PB + SB, 5 * D:6 * D], final)


def _combine(tab, nrows, ys, x_mid, xn, lpos_c, wts, mod_l, w, fin_g):
    final = fin_g is not None
    nb = PB + SB
    in_specs = [
        pl.BlockSpec((1, 1, TABW), lambda b: (b, 0, 0), memory_space=pltpu.SMEM),
        pl.BlockSpec((1, 1, TABW), lambda b: (jnp.minimum(b + 1, NBLK - 1), 0, 0),
                     memory_space=pltpu.SMEM),
        pl.BlockSpec(memory_space=pltpu.SMEM),
        pl.BlockSpec(memory_space=pl.ANY),
        pl.BlockSpec((TB, D), lambda b: (b, 0)),
        pl.BlockSpec((TB, D), lambda b: (b, 0)),
        pl.BlockSpec((1, TOPK, TB), lambda b: (b, 0, 0)),
        pl.BlockSpec((1, TOPK, TB), lambda b: (b, 0, 0)),
        _const_spec((nb, MOD_COLS)),
        _const_spec((D, ED)),
        _const_spec((D, ED)),
        _const_spec((ED, D)),
    ]
    args = [tab, tab, nrows, ys, x_mid, xn, lpos_c, wts, mod_l, w["sg"], w["su"], w["sd"]]
    out_shape = [jax.ShapeDtypeStruct((NT, D), F32)]
    out_specs = [pl.BlockSpec((TB, D), lambda b: (b, 0))]
    if final:
        in_specs.append(_const_spec((1, D)))
        args.append(fin_g)
        out_shape.append(jax.ShapeDtypeStruct((NT, D), F32))
        out_specs.append(pl.BlockSpec((TB, D), lambda b: (b, 0)))
    res = pl.pallas_call(
        functools.partial(_combine_kernel, final),
        grid=(NBLK,),
        in_specs=in_specs,
        out_specs=tuple(out_specs),
        out_shape=tuple(out_shape),
        scratch_shapes=[pltpu.VMEM((2, LS, D), BF16), pltpu.SemaphoreType.DMA((2,))],
        compiler_params=pltpu.CompilerParams(
            dimension_semantics=("arbitrary",), vmem_limit_bytes=VMEM_LIMIT),
        name="combine_final" if final else "combine",
    )(*args)
    return res if final else (res[0], None)


def _plan(cnt):
    c = cnt[:, :, :BPT].transpose(0, 2, 1).reshape(NBLK, NE).astype(I32)
    pc = (c + ROWS - 1) // ROWS * ROWS
    lend = jnp.cumsum(pc, axis=1)
    lstart = lend - pc
    etot = jnp.sum(pc, axis=0)
    epad = (etot + ETILE - 1) // ETILE * ETILE
    eend = jnp.cumsum(epad)
    eoff = eend - epad
    gstart = eoff[None, :] + jnp.cumsum(pc, axis=0) - pc
    eids = jnp.arange(NE, dtype=I32)

    def compact(mask, vals, width):
        pos = jnp.cumsum(mask.astype(I32), axis=1) - 1
        sel = mask[:, None, :] & (pos[:, None, :] == jnp.arange(width, dtype=I32)[None, :, None])
        return jnp.sum(jnp.where(sel, vals[:, None, :], 0), axis=-1), jnp.sum(mask.astype(I32), axis=1)

    nchunk = pc // ROWS
    packed = (gstart // ROWS) * 256 + lstart // ROWS
    two, n_two = compact(nchunk == 2, packed, NE)
    three, n_three = compact(nchunk == 3, packed, NE)
    lj = jnp.arange(NCH, dtype=I32)
    l8 = lj * ROWS
    ej = jnp.minimum(jnp.sum((lend[:, None, :] <= l8[None, :, None]).astype(I32), axis=-1), NE - 1)
    hot = ej[..., None] == eids
    row = jnp.sum(jnp.where(hot, (gstart - lstart)[:, None, :], 0), axis=-1) + l8[None, :]
    single = (nchunk != 2) & (nchunk != 3)
    is_one = (l8[None, :] < lend[:, -1:]) & jnp.any(hot & single[:, None, :], axis=-1)
    one, n_one = compact(is_one, (row // ROWS) * 256 + lj[None, :], NCH)
    counts = jnp.stack([n_two, n_three, n_one] + [jnp.zeros_like(n_two)] * (TABW - TAB_COUNTS - 3), axis=1)
    tab = jnp.concatenate([two, three, one, counts], axis=1).astype(I32).reshape(NBLK, 1, TABW)

    def lookup(table, idx):
        return jnp.sum(jnp.where(idx[..., None] == eids, table, 0), axis=-1)

    used = eend[-1] // ETILE
    v = jnp.arange(N_ETILES, dtype=I32)
    tin = jnp.minimum(v, used - 1)
    e = jnp.minimum(jnp.sum((eend[None, :] <= (tin * ETILE)[:, None]).astype(I32), axis=1), NE - 1)
    flag = (v < used).astype(I32)
    new = jnp.concatenate([jnp.ones((1,), I32), (e[1:] != e[:-1]).astype(I32)])
    live = etot > 0
    wslot = lookup((jnp.cumsum(live.astype(I32)) - 1) % 2, e)
    later = live[None, :] & (eids[None, :] > eids[:, None])
    nxt = jnp.min(jnp.where(later, eids[None, :], NE), axis=1)
    nxt = lookup(jnp.where(nxt < NE, nxt, -1), e)
    meta = (tin, e, flag, new, wslot.astype(I32), nxt.astype(I32))
    fill = jnp.concatenate([used[None], eoff + etot, (epad - etot) // ROWS]).astype(I32)
    return tab, fill, lend[:, -1], meta


def _block_diag(wh):
    out = jnp.zeros((2, 4, HEAD, 4, HEAD), wh.dtype)
    for hh in range(4):
        out = out.at[:, hh, :, hh, :].set(wh.reshape(2, 4, HEAD, HEAD)[:, hh])
    return out.reshape(2, 4 * HEAD, 4 * HEAD)


def _layer_weights(l, p):
    tril = jnp.tril(jnp.ones((CHUNK, CHUNK), bool))
    wt = jnp.where(tril, p["sgu_w"][l], 0.0)
    wpair = wt.reshape(4, 2, CHUNK, CHUNK).transpose(0, 2, 1, 3).reshape(4, CHUNK, 2 * CHUNK)
    bmap = jnp.repeat(p["sgu_b"][l].T, HEAD, axis=1)
    w8 = jnp.where(tril[:DEC_SEQ, :DEC_SEQ], p["sgu_w"][l][:, :DEC_SEQ, :DEC_SEQ], 0.0)
    coef = jnp.repeat(w8.transpose(2, 1, 0), HEAD, axis=2)
    bias8 = jnp.repeat(p["sgu_b"][l][:, :DEC_SEQ].T, HEAD, axis=1)
    gid = jnp.arange(256) // HEAD
    tok = jnp.arange(TILE)
    same_blk = (tok[:, None] // TB) == (tok[None, :] // TB)
    return dict(
        norm_mix_g=p["norm_mix_g"][l][None], norm_ffn_g=p["norm_ffn_g"][l][None],
        w_in=p["w_in"][l].astype(BF16), w_out=p["w_out"][l].astype(BF16),
        conv_w=p["conv_w"][l], conv_b=p["conv_b"][l][None],
        wr=_block_diag(p["gate_r_w"][l]).astype(BF16), wi=_block_diag(p["gate_i_w"][l]).astype(BF16),
        br=p["gate_r_b"][l][None], bi=p["gate_i_b"][l][None], lam=p["lru_lambda"][l][None],
        sgu_g=p["sgu_norm_g"][l][None], wpair=wpair.astype(BF16), bmap=bmap, coef=coef, bias8=bias8,
        out_g=p["out_norm_g"][l][None],
        bd=(gid[:, None] == gid[None, :]).astype(BF16),
        rw_t=p["router_w"][l].T.astype(BF16), rbias=p["router_bias"][l][:, None],
        tri=(same_blk & (tok[:, None] < tok[None, :])).astype(BF16),
        bsel=((tok[:, None] // TB) == jnp.arange(128)[None, :]).astype(BF16),
        ltri=(jnp.arange(NE)[:, None] > jnp.arange(NE)[None, :]).astype(BF16),
        sg=p["shared_w_gate"][l].astype(BF16), su=p["shared_w_up"][l].astype(BF16),
        sd=p["shared_w_down"][l].astype(BF16),
    )


def kernel(x_prompt, x_sample, c_prompt, c_sample, state_lru_h, state_conv, ada_w, ada_b, norm_mix_g, norm_ffn_g, w_in, conv_w, conv_b, gate_r_w, gate_r_b, gate_i_w, gate_i_b, lru_lambda, sgu_norm_g, sgu_w, sgu_b, out_norm_g, w_out, router_w, router_bias, exp_w_gate, exp_w_up, exp_w_down, shared_w_gate, shared_w_up, shared_w_down, final_norm_g):
    p = dict(norm_mix_g=norm_mix_g, norm_ffn_g=norm_ffn_g, w_in=w_in, conv_w=conv_w, conv_b=conv_b,
             gate_r_w=gate_r_w, gate_r_b=gate_r_b, gate_i_w=gate_i_w, gate_i_b=gate_i_b,
             lru_lambda=lru_lambda, sgu_norm_g=sgu_norm_g, sgu_w=sgu_w, sgu_b=sgu_b,
             out_norm_g=out_norm_g, w_out=w_out, router_w=router_w, router_bias=router_bias,
             shared_w_gate=shared_w_gate, shared_w_up=shared_w_up, shared_w_down=shared_w_down)
    x_all = jnp.concatenate([x_prompt.transpose(1, 0, 2).reshape(NP, D),
                             x_sample.transpose(1, 0, 2).reshape(NS, D)], axis=0)
    mod = _modulations(jnp.concatenate([c_prompt, c_sample], axis=0), ada_w, ada_b)

    hp, cp, hs, cs, vs = [], [], [], [], []
    y_all = None
    for l in range(DEPTH):
        w = _layer_weights(l, p)
        w["h0s"] = state_lru_h[l]
        w["convs"] = state_conv[l].transpose(1, 0, 2).reshape((CONV_W - 1) * SB, LW)
        yn, hlp, cvp, hls, cvs, v_s = _mixer(x_all, mod[l], w)
        x_mid, xn, lpos_t, w_t, cnt = _router(x_all, yn, mod[l], w)
        tab, used, nrows, meta = _plan(cnt)
        def per_block(a):
            return a.reshape(N_TILES, TOPK, BPT, TB).transpose(0, 2, 1, 3).reshape(NBLK, TOPK, TB)

        lpos_d = per_block(lpos_t)
        xs = _dispatch(tab, used, nrows, lpos_d, xn)
        ys = _experts(l, meta, xs, exp_w_gate, exp_w_up, exp_w_down)
        fin = final_norm_g[None] if l == DEPTH - 1 else None
        x_all, y_all = _combine(tab, nrows, ys, x_mid, xn, lpos_d, per_block(w_t), mod[l], w, fin)
        hp.append(hlp)
        cp.append(cvp.reshape(CONV_W - 1, PB, LW).transpose(1, 0, 2))
        hs.append(hls)
        cs.append(cvs.reshape(CONV_W - 1, SB, LW).transpose(1, 0, 2))
        vs.append(v_s.reshape(DEC_SEQ, SB, SW).transpose(1, 0, 2))

    y_prompt = y_all[:NP].reshape(SEQ, PB, D).transpose(1, 0, 2)
    y_sample = y_all[NP:].reshape(DEC_SEQ, SB, D).transpose(1, 0, 2)
    return (y_prompt, y_sample, jnp.stack(hp), jnp.stack(cp), jnp.stack(hs), jnp.stack(cs),
            jnp.stack(vs))
```

```python
import functools

import jax
import jax.numpy as jnp
from jax import lax
from jax.experimental import pallas as pl
from jax.experimental.pallas import tpu as pltpu

F32 = jnp.float32
BF16 = jnp.bfloat16
I32 = jnp.int32
I16 = jnp.int16

D = 1024
DEPTH = 4
PB = 8
SEQ = 2048
SB = 128
DEC_SEQ = 8
NP = PB * SEQ
NS = SB * DEC_SEQ
NT = NP + NS
TILE = 1024
N_TILES = NT // TILE
P_TILES = NP // TILE
LW = 512
SW = 512
HEAD = 64
CHUNK = 128
CONV_W = 4
LRU_C = 8.0
NE = 64
TOPK = 8
ED = 256
ROUTED_SCALE = 2.5
EPS = 1e-6
MOD_COLS = 6 * D
MOD_BLK = 1536
VMEM_LIMIT = 56 * 1024 * 1024

TB = 256
NBLK = NT // TB
PBLK = NP // TB
BPT = TILE // TB
ROWS = 16
LS = 3072
NCH = LS // ROWS
NCH_MIN = TB * TOPK // ROWS
SUB = 512
LAST = LS // SUB - 1
ETILE = 512
NSLOT_MAX = -(-(NT * TOPK + NBLK * NE * (ROWS - 1) + NE * (ETILE - ROWS)) // ETILE) * ETILE
N_ETILES = NSLOT_MAX // ETILE
TAB_TWO, TAB_THREE, TAB_ONE = 0, NE, 2 * NE
TAB_COUNTS = 2 * NE + NCH
TABW = TAB_COUNTS + 8
COPY_CLASSES = ((TAB_TWO, 2), (TAB_THREE, 3), (TAB_ONE, 1))

assert NSLOT_MAX % ETILE == 0 and LS >= TB * TOPK + NE * (ROWS - 1) and LS % SUB == 0


def _dot(a, b):
    return jnp.dot(a, b, preferred_element_type=F32)


def _const_spec(shape):
    nd = len(shape)
    return pl.BlockSpec(shape, lambda *_: (0,) * nd)


def _mod_kernel(c_ref, w_ref, b_ref, o_ref):
    s = jax.nn.silu(c_ref[...]).astype(BF16)
    o_ref[0] = _dot(s, w_ref[0].astype(BF16)) + b_ref[0]


def _modulations(c_all, ada_w, ada_b):
    nb = c_all.shape[0]
    return pl.pallas_call(
        _mod_kernel,
        grid=(DEPTH, MOD_COLS // MOD_BLK),
        in_specs=[
            pl.BlockSpec((nb, D), lambda l, j: (0, 0)),
            pl.BlockSpec((1, D, MOD_BLK), lambda l, j: (l, 0, j)),
            pl.BlockSpec((1, 1, MOD_BLK), lambda l, j: (l, 0, j)),
        ],
        out_specs=pl.BlockSpec((1, nb, MOD_BLK), lambda l, j: (l, 0, j)),
        out_shape=jax.ShapeDtypeStruct((DEPTH, nb, MOD_COLS), F32),
        compiler_params=pltpu.CompilerParams(
            dimension_semantics=("arbitrary", "arbitrary"), vmem_limit_bytes=VMEM_LIMIT),
        name="adaln_mod",
    )(c_all, ada_w, ada_b.reshape(DEPTH, 1, MOD_COLS))


def _sgu_prompt(v, wpair_ref, bmap_ref, vbuf, sbuf):
    for j in range(4):
        vbuf[j] = v[:, 128 * j:128 * (j + 1)]
    lane = lax.broadcasted_iota(I32, (CHUNK, 128), 1)
    left = lane < HEAD
    for b in range(PB):
        for j in range(4):
            vp = vbuf[j, pl.ds(b, CHUNK, stride=PB), :]
            rhs = jnp.concatenate(
                [jnp.where(left, vp, 0.0), jnp.where(left, 0.0, vp)], axis=0).astype(BF16)
            sj = _dot(wpair_ref[j], rhs) + bmap_ref[:, 128 * j:128 * (j + 1)]
            sbuf[j, pl.ds(b, CHUNK, stride=PB), :] = sj
    return jnp.concatenate([sbuf[j] for j in range(4)], axis=1)


def _sgu_sample(v, coef_ref, bias_ref):
    vt = [v[t * SB:(t + 1) * SB] for t in range(DEC_SEQ)]
    rows = []
    for p in range(DEC_SEQ):
        acc = bias_ref[p:p + 1, :] + coef_ref[0, p:p + 1, :] * vt[0]
        for q in range(1, p + 1):
            acc = acc + coef_ref[q, p:p + 1, :] * vt[q]
        rows.append(acc)
    return jnp.concatenate(rows, axis=0)


def _mixer_tile(G, x, sh, sc, h_in, tail_in, sgu_fn, r, hbuf):
    nst = TILE // G
    x3 = x.reshape(nst, G, D)
    ms = jnp.mean(x3 * x3, axis=-1, keepdims=True)
    xn = x3 * lax.rsqrt(ms + EPS) * r["norm_g"][...]
    xn = (xn * (1.0 + sc[None]) + sh[None]).reshape(TILE, D).astype(BF16)

    xa = _dot(xn, r["w_in"][:, 0:LW])
    xp = jnp.concatenate([tail_in, xa], axis=0)
    cw = r["conv_w"]
    xc = r["conv_b"][...] + cw[0:1, :] * xp[0:TILE]
    for k in range(1, CONV_W):
        xc = xc + cw[k:k + 1, :] * xp[k * G:k * G + TILE]
    new_tail = xa[TILE - (CONV_W - 1) * G:]
    xcb = xc.astype(BF16)
    half = LW // 2
    r_pre = jnp.concatenate(
        [_dot(xcb[:, :half], r["wr"][0]), _dot(xcb[:, half:], r["wr"][1])], axis=1) + r["br"][...]
    i_pre = jnp.concatenate(
        [_dot(xcb[:, :half], r["wi"][0]), _dot(xcb[:, half:], r["wi"][1])], axis=1) + r["bi"][...]
    rg = jax.nn.sigmoid(r_pre)
    ig = jax.nn.sigmoid(i_pre)
    log_a = -LRU_C * rg * jax.nn.softplus(-r["lam"][...])
    a = jnp.exp(log_a)
    th = jnp.tanh(log_a)
    num = -2.0 * th
    mult = jnp.where(num > 0.0, num * lax.rsqrt(num * (1.0 - th)), 0.0)
    u = mult * (ig * xc)
    h = h_in
    for s in range(nst):
        h = a[s * G:(s + 1) * G] * h + u[s * G:(s + 1) * G]
        hbuf[s * G:(s + 1) * G, :] = h
    ya = hbuf[...] * jax.nn.gelu(_dot(xn, r["w_in"][:, LW:2 * LW]))

    ug = jax.nn.gelu(_dot(xn, r["w_in"][:, 2 * LW:2 * LW + SW]))
    vg = jax.nn.gelu(_dot(xn, r["w_in"][:, 2 * LW + SW:]))
    v = vg * lax.rsqrt(jnp.mean(vg * vg, axis=-1, keepdims=True) + EPS) * r["sgu_g"][...]
    yb = ug * sgu_fn(v)

    y = jnp.concatenate([ya, yb], axis=1)
    bd = r["bd"][...]
    outs = []
    for j in range(D // 256):
        ysl = y[:, 256 * j:256 * (j + 1)]
        sq = ysl * ysl
        hi = sq.astype(BF16)
        lo = (sq - hi.astype(F32)).astype(BF16)
        msq = (_dot(hi, bd) + _dot(lo, bd)) * (1.0 / HEAD)
        outs.append(ysl * lax.rsqrt(msq + EPS) * r["out_g"][:, 256 * j:256 * (j + 1)])
    yn = jnp.concatenate(outs, axis=1).astype(BF16)
    return yn, h, new_tail, v


_MIX_IN = ("x", "mod", "norm_g", "w_in", "conv_w", "conv_b", "wr", "wi", "br", "bi", "lam",
           "sgu_g", "wpair", "bmap", "coef", "bias8", "out_g", "bd", "h0s", "convs")
_MIX_OUT = ("y", "hlast_p", "conv_p", "hlast_s", "conv_s", "v_s")
_MIX_SCR = ("hbuf", "vbuf", "sbuf", "h_carry", "tail_carry")


def _mixer_kernel(*refs):
    names = _MIX_IN + _MIX_OUT + _MIX_SCR
    r = dict(zip(names, refs))
    i = pl.program_id(0)

    @pl.when(i == 0)
    def _():
        r["h_carry"][...] = jnp.zeros((PB, LW), F32)
        r["tail_carry"][...] = jnp.zeros(((CONV_W - 1) * PB, LW), F32)

    @pl.when(i < P_TILES)
    def _():
        sh = r["mod"][0:PB, 0:D]
        sc = r["mod"][0:PB, D:2 * D]
        sgu = functools.partial(_sgu_prompt, wpair_ref=r["wpair"], bmap_ref=r["bmap"],
                                vbuf=r["vbuf"], sbuf=r["sbuf"])
        yn, h, tail, _ = _mixer_tile(PB, r["x"][...], sh, sc, r["h_carry"][...],
                                     r["tail_carry"][...], sgu, r, r["hbuf"])
        r["y"][...] = yn
        r["h_carry"][...] = h
        r["tail_carry"][...] = tail
        r["hlast_p"][...] = h
        r["conv_p"][...] = tail

    @pl.when(i == P_TILES)
    def _():
        sh = r["mod"][PB:PB + SB, 0:D]
        sc = r["mod"][PB:PB + SB, D:2 * D]
        sgu = functools.partial(_sgu_sample, coef_ref=r["coef"], bias_ref=r["bias8"])
        yn, h, tail, v = _mixer_tile(SB, r["x"][...], sh, sc, r["h0s"][...],
                                     r["convs"][...], sgu, r, r["hbuf"])
        r["y"][...] = yn
        r["hlast_s"][...] = h
        r["conv_s"][...] = tail
        r["v_s"][...] = v


def _mixer(x_all, mod_l, w):
    nb = PB + SB
    in_specs = [
        pl.BlockSpec((TILE, D), lambda i: (i, 0)),
        _const_spec((nb, MOD_COLS)),
        _const_spec((1, D)),
        _const_spec((D, 2 * LW + 2 * SW)),
        _const_spec((CONV_W, LW)),
        _const_spec((1, LW)),
        _const_spec((2, LW // 2, LW // 2)),
        _const_spec((2, LW // 2, LW // 2)),
        _const_spec((1, LW)),
        _const_spec((1, LW)),
        _const_spec((1, LW)),
        _const_spec((1, SW)),
        _const_spec((4, CHUNK, 2 * CHUNK)),
        _const_spec((CHUNK, SW)),
        _const_spec((DEC_SEQ, DEC_SEQ, SW)),
        _const_spec((DEC_SEQ, SW)),
        _const_spec((1, D)),
        _const_spec((256, 256)),
        _const_spec((SB, LW)),
        _const_spec(((CONV_W - 1) * SB, LW)),
    ]
    out_shape = (
        jax.ShapeDtypeStruct((NT, D), BF16),
        jax.ShapeDtypeStruct((PB, LW), F32),
        jax.ShapeDtypeStruct(((CONV_W - 1) * PB, LW), F32),
        jax.ShapeDtypeStruct((SB, LW), F32),
        jax.ShapeDtypeStruct(((CONV_W - 1) * SB, LW), F32),
        jax.ShapeDtypeStruct((NS, SW), F32),
    )
    out_specs = (
        pl.BlockSpec((TILE, D), lambda i: (i, 0)),
        _const_spec((PB, LW)),
        _const_spec(((CONV_W - 1) * PB, LW)),
        _const_spec((SB, LW)),
        _const_spec(((CONV_W - 1) * SB, LW)),
        _const_spec((NS, SW)),
    )
    scratch = [
        pltpu.VMEM((TILE, LW), F32),
        pltpu.VMEM((4, TILE, 128), F32),
        pltpu.VMEM((4, TILE, 128), F32),
        pltpu.VMEM((PB, LW), F32),
        pltpu.VMEM(((CONV_W - 1) * PB, LW), F32),
    ]
    return pl.pallas_call(
        _mixer_kernel,
        grid=(N_TILES,),
        in_specs=in_specs,
        out_specs=out_specs,
        out_shape=out_shape,
        scratch_shapes=scratch,
        compiler_params=pltpu.CompilerParams(
            dimension_semantics=("arbitrary",), vmem_limit_bytes=VMEM_LIMIT),
        name="mixer",
    )(x_all, mod_l, w["norm_mix_g"], w["w_in"], w["conv_w"], w["conv_b"], w["wr"], w["wi"],
      w["br"], w["bi"], w["lam"], w["sgu_g"], w["wpair"], w["bmap"], w["coef"], w["bias8"],
      w["out_g"], w["bd"], w["h0s"], w["convs"])


def _router_tile(G, r, gm, sc, sh):
    nst = TILE // G
    out = _dot(r["y"][...], r["w_out"][...])
    xm3 = r["x"][...].reshape(nst, G, D) + gm[None] * out.reshape(nst, G, D)
    r["x_mid"][...] = xm3.reshape(TILE, D)
    ms = jnp.mean(xm3 * xm3, axis=-1, keepdims=True)
    xn3 = xm3 * lax.rsqrt(ms + EPS) * r["norm_g"][...]
    xn = (xn3 * (1.0 + sc[None]) + sh[None]).reshape(TILE, D).astype(BF16)
    r["xn"][...] = xn

    logits = lax.dot_general(r["rw_t"][...], xn, (((1,), (1,)), ((), ())),
                             preferred_element_type=F32)
    sig = jax.nn.sigmoid(logits)
    sel = sig + r["rbias"][...]
    eiota = lax.broadcasted_iota(I32, (NE, TILE), 0)
    chosen = jnp.zeros((NE, TILE), F32)
    w_rows, hots = [], []
    for _ in range(TOPK):
        m = jnp.max(sel, axis=0, keepdims=True)
        idx = jnp.min(jnp.where(sel == m, eiota, NE), axis=0, keepdims=True)
        hot = eiota == idx
        hots.append(hot)
        w_rows.append(jnp.sum(jnp.where(hot, sig, 0.0), axis=0, keepdims=True))
        chosen = jnp.where(hot, 1.0, chosen)
        sel = jnp.where(hot, -jnp.inf, sel)
    den = w_rows[0]
    for k in range(1, TOPK):
        den = den + w_rows[k]
    scale = ROUTED_SCALE / den
    r["w_t"][0] = jnp.concatenate([wk * scale for wk in w_rows], axis=0)

    cb = chosen.astype(BF16)
    rank = _dot(cb, r["tri"][...])
    cnt = _dot(cb, r["bsel"][...])
    groups = jnp.floor((cnt + (ROWS - 1.0)) * (1.0 / ROWS))
    start = ROWS * _dot(r["ltri"][...], groups.astype(BF16))
    pos = rank + jnp.concatenate(
        [jnp.broadcast_to(start[:, j:j + 1], (NE, TB)) for j in range(BPT)], axis=1)
    lp_rows = [jnp.sum(jnp.where(hot, pos, 0.0), axis=0, keepdims=True) for hot in hots]
    r["lpos_t"][0] = jnp.concatenate(lp_rows, axis=0).astype(I32)
    r["cnt"][0] = cnt


_RT_IN = ("x", "y", "mod", "w_out", "norm_g", "rw_t", "rbias", "tri", "bsel", "ltri")
_RT_OUT = ("x_mid", "xn", "lpos_t", "w_t", "cnt")


def _router_kernel(*refs):
    r = dict(zip(_RT_IN + _RT_OUT, refs))
    i = pl.program_id(0)

    def mods(lo, n):
        m = r["mod"]
        return m[lo:lo + n, 2 * D:3 * D], m[lo:lo + n, 4 * D:5 * D], m[lo:lo + n, 3 * D:4 * D]

    @pl.when(i < P_TILES)
    def _():
        _router_tile(PB, r, *mods(0, PB))

    @pl.when(i == P_TILES)
    def _():
        _router_tile(SB, r, *mods(PB, SB))


def _router(x_all, y_all, mod_l, w):
    nb = PB + SB
    in_specs = [
        pl.BlockSpec((TILE, D), lambda i: (i, 0)),
        pl.BlockSpec((TILE, D), lambda i: (i, 0)),
        _const_spec((nb, MOD_COLS)),
        _const_spec((D, D)),
        _const_spec((1, D)),
        _const_spec((NE, D)),
        _const_spec((NE, 1)),
        _const_spec((TILE, TILE)),
        _const_spec((TILE, 128)),
        _const_spec((NE, NE)),
    ]
    out_shape = (
        jax.ShapeDtypeStruct((NT, D), F32),
        jax.ShapeDtypeStruct((NT, D), BF16),
        jax.ShapeDtypeStruct((N_TILES, TOPK, TILE), I32),
        jax.ShapeDtypeStruct((N_TILES, TOPK, TILE), F32),
        jax.ShapeDtypeStruct((N_TILES, NE, 128), F32),
    )
    out_specs = (
        pl.BlockSpec((TILE, D), lambda i: (i, 0)),
        pl.BlockSpec((TILE, D), lambda i: (i, 0)),
        pl.BlockSpec((1, TOPK, TILE), lambda i: (i, 0, 0)),
        pl.BlockSpec((1, TOPK, TILE), lambda i: (i, 0, 0)),
        pl.BlockSpec((1, NE, 128), lambda i: (i, 0, 0)),
    )
    return pl.pallas_call(
        _router_kernel,
        grid=(N_TILES,),
        in_specs=in_specs,
        out_specs=out_specs,
        out_shape=out_shape,
        compiler_params=pltpu.CompilerParams(
            dimension_semantics=("arbitrary",), vmem_limit_bytes=VMEM_LIMIT),
        name="router",
    )(x_all, y_all, mod_l, w["w_out"], w["norm_ffn_g"], w["rw_t"], w["rbias"], w["tri"],
      w["bsel"], w["ltri"])


def _group_copies(tab_ref, make_copy):
    for cls, (base, nch) in enumerate(COPY_CLASSES):
        def body(i, carry, base=base, nch=nch, cls=cls):
            p = tab_ref[0, 0, base + i]
            l = pl.multiple_of((p & 255) * ROWS, ROWS)
            g = pl.multiple_of((p >> 8) * ROWS, ROWS)
            make_copy(l, g, nch * ROWS).start(priority=cls % 2)
            return carry

        lax.fori_loop(0, tab_ref[0, 0, TAB_COUNTS + cls], body, 0)


def _chunk_waits(nrows, whole_copy, chunk_copy):
    whole_copy.wait()

    def body(i, carry):
        chunk_copy.wait()
        return carry

    lax.fori_loop(0, (nrows - NCH_MIN * ROWS) // ROWS, body, 0)


def _dispatch_kernel(tab_ref, fill_ref, nrows_ref, lpos_ref, xn_ref, xs_ref, loc, zbuf, sem):
    b = pl.program_id(0)
    slot = b % 2
    xb = xn_ref[...]
    lp = lpos_ref[0].astype(I16)
    one = jnp.ones((), BF16)

    def sort_rows(c):
        siota = (lax.broadcasted_iota(I32, (SUB, TB), 0) + c * SUB).astype(I16)
        p = jnp.zeros((SUB, TB), BF16)
        for k in range(TOPK):
            p = jnp.where(siota == lp[k:k + 1, :], one, p)
        loc[slot, c * SUB:(c + 1) * SUB, :] = _dot(p, xb).astype(BF16)

    for c in range(LAST):
        sort_rows(c)
    tail_used = nrows_ref[b] > LAST * SUB

    @pl.when(tail_used)
    def _():
        sort_rows(LAST)

    @pl.when(jnp.logical_not(tail_used))
    def _():
        loc[slot, LAST * SUB:, :] = jnp.zeros((SUB, D), BF16)

    _group_copies(tab_ref, lambda l, g, n: pltpu.make_async_copy(
        loc.at[slot, pl.ds(l, n)], xs_ref.at[pl.ds(g, n)], sem.at[slot]))

    def drain(s, nrows):
        _chunk_waits(
            nrows,
            pltpu.make_async_copy(loc.at[s, pl.ds(0, NCH_MIN * ROWS)],
                                  xs_ref.at[pl.ds(0, NCH_MIN * ROWS)], sem.at[s]),
            pltpu.make_async_copy(loc.at[s, pl.ds(0, ROWS)], xs_ref.at[pl.ds(0, ROWS)], sem.at[s]))

    @pl.when(b > 0)
    def _():
        drain(1 - slot, nrows_ref[jnp.maximum(b - 1, 0)])

    @pl.when(b == NBLK - 1)
    def _():
        drain(slot, nrows_ref[b])
        zbuf[...] = jnp.zeros((ETILE, D), BF16)

        def gap_chunk(e, j):
            row = pl.multiple_of(fill_ref[1 + e] + j * ROWS, ROWS)
            return pltpu.make_async_copy(
                zbuf.at[pl.ds(0, ROWS)], xs_ref.at[pl.ds(row, ROWS)], sem.at[1])

        def start_gap(e, carry):
            lax.fori_loop(0, fill_ref[1 + NE + e],
                          lambda j, c: (gap_chunk(e, j).start(), c)[1], 0)
            return carry

        def wait_gap(e, carry):
            lax.fori_loop(0, fill_ref[1 + NE + e],
                          lambda j, c: (gap_chunk(e, j).wait(), c)[1], 0)
            return carry

        lax.fori_loop(0, NE, start_gap, 0)
        lax.fori_loop(0, NE, wait_gap, 0)

        def zero_tile(t):
            return pltpu.make_async_copy(
                zbuf, xs_ref.at[pl.ds(pl.multiple_of(t * ETILE, ETILE), ETILE)], sem.at[0])

        def start(t, carry):
            zero_tile(t).start()
            return carry

        def wait(t, carry):
            zero_tile(t).wait()
            return carry

        lax.fori_loop(fill_ref[0], N_ETILES, start, 0)
        lax.fori_loop(fill_ref[0], N_ETILES, wait, 0)


def _dispatch(tab, fill_from, nrows, lpos_d, xn):
    return pl.pallas_call(
        _dispatch_kernel,
        grid=(NBLK,),
        in_specs=[
            pl.BlockSpec((1, 1, TABW), lambda b: (b, 0, 0), memory_space=pltpu.SMEM),
            pl.BlockSpec(memory_space=pltpu.SMEM),
            pl.BlockSpec(memory_space=pltpu.SMEM),
            pl.BlockSpec((1, TOPK, TB), lambda b: (b, 0, 0)),
            pl.BlockSpec((TB, D), lambda b: (b, 0)),
        ],
        out_specs=pl.BlockSpec(memory_space=pl.ANY),
        out_shape=jax.ShapeDtypeStruct((NSLOT_MAX, D), BF16),
        scratch_shapes=[pltpu.VMEM((2, LS, D), BF16), pltpu.VMEM((ETILE, D), BF16),
                        pltpu.SemaphoreType.DMA((2,))],
        compiler_params=pltpu.CompilerParams(
            dimension_semantics=("arbitrary",), vmem_limit_bytes=VMEM_LIMIT),
        name="dispatch",
    )(tab, fill_from, nrows, lpos_d, xn)


def _expert_kernel(l, tin_ref, exp_ref, flag_ref, new_ref, wslot_ref, next_ref, xs_ref, wg_hbm,
                   wu_hbm, wd_hbm, ys_ref, wg_f, wu_f, wd_f, wg_b, wu_b, wd_b, wsem):
    v = pl.program_id(0)
    flag = flag_ref[v]

    def weight_copies(e, p):
        return (pltpu.make_async_copy(wg_hbm.at[l, e], wg_f.at[p], wsem.at[p]),
                pltpu.make_async_copy(wu_hbm.at[l, e], wu_f.at[p], wsem.at[p]),
                pltpu.make_async_copy(wd_hbm.at[l, e], wd_f.at[p], wsem.at[p]))

    @pl.when(v == 0)
    def _():
        for c in weight_copies(exp_ref[0], 0):
            c.start()

    @pl.when(new_ref[v] == 1)
    def _():
        p = wslot_ref[v]
        for c in weight_copies(exp_ref[v], p):
            c.wait()

        @pl.when(next_ref[v] >= 0)
        def _():
            for c in weight_copies(next_ref[v], 1 - p):
                c.start()

        wg_b[...] = wg_f[p].astype(BF16)
        wu_b[...] = wu_f[p].astype(BF16)
        wd_b[...] = wd_f[p].astype(BF16)

    @pl.when(flag == 1)
    def _():
        x = xs_ref[...]
        hid = jax.nn.silu(_dot(x, wg_b[...])) * _dot(x, wu_b[...])
        ys_ref[...] = _dot(hid.astype(BF16), wd_b[...]).astype(BF16)


def _experts(l, meta, xs, wg, wu, wd):
    nmeta = len(meta)
    grid_spec = pltpu.PrefetchScalarGridSpec(
        num_scalar_prefetch=nmeta,
        grid=(N_ETILES,),
        in_specs=[
            pl.BlockSpec((ETILE, D), lambda v, ti, *_: (ti[v], 0)),
            pl.BlockSpec(memory_space=pl.ANY),
            pl.BlockSpec(memory_space=pl.ANY),
            pl.BlockSpec(memory_space=pl.ANY),
        ],
        out_specs=pl.BlockSpec((ETILE, D), lambda v, ti, *_: (ti[v], 0)),
        scratch_shapes=[pltpu.VMEM((2, D, ED), F32), pltpu.VMEM((2, D, ED), F32),
                        pltpu.VMEM((2, ED, D), F32), pltpu.VMEM((D, ED), BF16),
                        pltpu.VMEM((D, ED), BF16), pltpu.VMEM((ED, D), BF16),
                        pltpu.SemaphoreType.DMA((2,))],
    )
    return pl.pallas_call(
        functools.partial(_expert_kernel, l),
        grid_spec=grid_spec,
        out_shape=jax.ShapeDtypeStruct((NSLOT_MAX, D), BF16),
        input_output_aliases={nmeta: 0},
        compiler_params=pltpu.CompilerParams(
            dimension_semantics=("arbitrary",), vmem_limit_bytes=VMEM_LIMIT),
        name="experts",
    )(*meta, xs, wg, wu, wd)


def _combine_tile(G, r, slot, gf, final):
    nst = TB // G
    xb = r["xn"][...]
    hid = jax.nn.silu(_dot(xb, r["sg"][...])) * _dot(xb, r["su"][...])
    moe = _dot(hid.astype(BF16), r["sd"][...])
    _chunk_waits(
        r["nrows"][pl.program_id(0)],
        pltpu.make_async_copy(r["ys"].at[pl.ds(0, NCH_MIN * ROWS)],
                              r["loc"].at[slot, pl.ds(0, NCH_MIN * ROWS)], r["sem"].at[slot]),
        pltpu.make_async_copy(r["ys"].at[pl.ds(0, ROWS)], r["loc"].at[slot, pl.ds(0, ROWS)],
                              r["sem"].at[slot]))
    lp = r["lpos"][0].astype(I16)
    wts = r["wts"][0].astype(BF16)

    def unsort_rows(c):
        siota = (lax.broadcasted_iota(I32, (SUB, TB), 0) + c * SUB).astype(I16)
        p = jnp.zeros((SUB, TB), BF16)
        for k in range(TOPK):
            p = jnp.where(siota == lp[k:k + 1, :], wts[k:k + 1, :], p)
        return lax.dot_general(
            p, r["loc"][slot, c * SUB:(c + 1) * SUB, :], (((0,), (0,)), ((), ())),
            preferred_element_type=F32)

    def finish(moe):
        xo = r["x_mid"][...].reshape(nst, G, D) + gf[None] * moe.reshape(nst, G, D)
        if final:
            ms = jnp.mean(xo * xo, axis=-1, keepdims=True)
            xo = xo * lax.rsqrt(ms + EPS) * r["fin_g"][...]
        r["x_out"][...] = xo.reshape(TB, D)

    for c in range(LAST):
        moe = moe + unsort_rows(c)
    tail_used = r["nrows"][pl.program_id(0)] > LAST * SUB

    @pl.when(tail_used)
    def _():
        finish(moe + unsort_rows(LAST))

    @pl.when(jnp.logical_not(tail_used))
    def _():
        finish(moe)


def _combine_kernel(final, *refs):
    names = ["tab", "tab_next", "nrows", "ys", "x_mid", "xn", "lpos", "wts", "mod", "sg", "su", "sd"]
    names += ["fin_g"] if final else []
    names += ["x_out", "loc", "sem"]
    r = dict(zip(names, refs))
    b = pl.program_id(0)
    slot = b % 2

    def gather(tab_ref, s):
        r["loc"][s, NCH_MIN * ROWS:, :] = jnp.zeros((LS - NCH_MIN * ROWS, D), BF16)
        _group_copies(tab_ref, lambda l, g, n: pltpu.make_async_copy(
            r["ys"].at[pl.ds(g, n)], r["loc"].at[s, pl.ds(l, n)], r["sem"].at[s]))

    @pl.when(b == 0)
    def _():
        gather(r["tab"], 0)

    @pl.when(b + 1 < NBLK)
    def _():
        gather(r["tab_next"], 1 - slot)

    @pl.when(b < PBLK)
    def _():
        _combine_tile(PB, r, slot, r["mod"][0:PB, 5 * D:6 * D], final)

    @pl.when(b >= PBLK)
    def _():
        _combine_tile(SB, r, slot, r["mod"][PB:PB + SB, 5 * D:6 * D], final)


def _combine(tab, nrows, ys, x_mid, xn, lpos_c, wts, mod_l, w, fin_g):
    final = fin_g is not None
    nb = PB + SB
    in_specs = [
        pl.BlockSpec((1, 1, TABW), lambda b: (b, 0, 0), memory_space=pltpu.SMEM),
        pl.BlockSpec((1, 1, TABW), lambda b: (jnp.minimum(b + 1, NBLK - 1), 0, 0),
                     memory_space=pltpu.SMEM),
        pl.BlockSpec(memory_space=pltpu.SMEM),
        pl.BlockSpec(memory_space=pl.ANY),
        pl.BlockSpec((TB, D), lambda b: (b, 0)),
        pl.BlockSpec((TB, D), lambda b: (b, 0)),
        pl.BlockSpec((1, TOPK, TB), lambda b: (b, 0, 0)),
        pl.BlockSpec((1, TOPK, TB), lambda b: (b, 0, 0)),
        _const_spec((nb, MOD_COLS)),
        _const_spec((D, ED)),
        _const_spec((D, ED)),
        _const_spec((ED, D)),
    ]
    args = [tab, tab, nrows, ys, x_mid, xn, lpos_c, wts, mod_l, w["sg"], w["su"], w["sd"]]
    if final:
        in_specs.append(_const_spec((1, D)))
        args.append(fin_g)
    return pl.pallas_call(
        functools.partial(_combine_kernel, final),
        grid=(NBLK,),
        in_specs=in_specs,
        out_specs=pl.BlockSpec((TB, D), lambda b: (b, 0)),
        out_shape=jax.ShapeDtypeStruct((NT, D), F32),
        scratch_shapes=[pltpu.VMEM((2, LS, D), BF16), pltpu.SemaphoreType.DMA((2,))],
        compiler_params=pltpu.CompilerParams(
            dimension_semantics=("arbitrary",), vmem_limit_bytes=VMEM_LIMIT),
        name="combine_final" if final else "combine",
    )(*args)


def _plan(cnt):
    c = cnt[:, :, :BPT].transpose(0, 2, 1).reshape(NBLK, NE).astype(I32)
    pc = (c + ROWS - 1) // ROWS * ROWS
    lend = jnp.cumsum(pc, axis=1)
    lstart = lend - pc
    etot = jnp.sum(pc, axis=0)
    epad = (etot + ETILE - 1) // ETILE * ETILE
    eend = jnp.cumsum(epad)
    eoff = eend - epad
    gstart = eoff[None, :] + jnp.cumsum(pc, axis=0) - pc
    eids = jnp.arange(NE, dtype=I32)

    def compact(mask, vals, width):
        pos = jnp.cumsum(mask.astype(I32), axis=1) - 1
        sel = mask[:, None, :] & (pos[:, None, :] == jnp.arange(width, dtype=I32)[None, :, None])
        return jnp.sum(jnp.where(sel, vals[:, None, :], 0), axis=-1), jnp.sum(mask.astype(I32), axis=1)

    nchunk = pc // ROWS
    packed = (gstart // ROWS) * 256 + lstart // ROWS
    two, n_two = compact(nchunk == 2, packed, NE)
    three, n_three = compact(nchunk == 3, packed, NE)
    lj = jnp.arange(NCH, dtype=I32)
    l8 = lj * ROWS
    ej = jnp.minimum(jnp.sum((lend[:, None, :] <= l8[None, :, None]).astype(I32), axis=-1), NE - 1)
    hot = ej[..., None] == eids
    row = jnp.sum(jnp.where(hot, (gstart - lstart)[:, None, :], 0), axis=-1) + l8[None, :]
    single = (nchunk != 2) & (nchunk != 3)
    is_one = (l8[None, :] < lend[:, -1:]) & jnp.any(hot & single[:, None, :], axis=-1)
    one, n_one = compact(is_one, (row // ROWS) * 256 + lj[None, :], NCH)
    counts = jnp.stack([n_two, n_three, n_one] + [jnp.zeros_like(n_two)] * (TABW - TAB_COUNTS - 3), axis=1)
    tab = jnp.concatenate([two, three, one, counts], axis=1).astype(I32).reshape(NBLK, 1, TABW)

    def lookup(table, idx):
        return jnp.sum(jnp.where(idx[..., None] == eids, table, 0), axis=-1)

    used = eend[-1] // ETILE
    v = jnp.arange(N_ETILES, dtype=I32)
    tin = jnp.minimum(v, used - 1)
    e = jnp.minimum(jnp.sum((eend[None, :] <= (tin * ETILE)[:, None]).astype(I32), axis=1), NE - 1)
    flag = (v < used).astype(I32)
    new = jnp.concatenate([jnp.ones((1,), I32), (e[1:] != e[:-1]).astype(I32)])
    live = etot > 0
    wslot = lookup((jnp.cumsum(live.astype(I32)) - 1) % 2, e)
    later = live[None, :] & (eids[None, :] > eids[:, None])
    nxt = jnp.min(jnp.where(later, eids[None, :], NE), axis=1)
    nxt = lookup(jnp.where(nxt < NE, nxt, -1), e)
    meta = (tin, e, flag, new, wslot.astype(I32), nxt.astype(I32))
    fill = jnp.concatenate([used[None], eoff + etot, (epad - etot) // ROWS]).astype(I32)
    return tab, fill, lend[:, -1], meta


def _block_diag(wh):
    out = jnp.zeros((2, 4, HEAD, 4, HEAD), wh.dtype)
    for hh in range(4):
        out = out.at[:, hh, :, hh, :].set(wh.reshape(2, 4, HEAD, HEAD)[:, hh])
    return out.reshape(2, 4 * HEAD, 4 * HEAD)


def _layer_weights(l, p):
    tril = jnp.tril(jnp.ones((CHUNK, CHUNK), bool))
    wt = jnp.where(tril, p["sgu_w"][l], 0.0)
    wpair = wt.reshape(4, 2, CHUNK, CHUNK).transpose(0, 2, 1, 3).reshape(4, CHUNK, 2 * CHUNK)
    bmap = jnp.repeat(p["sgu_b"][l].T, HEAD, axis=1)
    w8 = jnp.where(tril[:DEC_SEQ, :DEC_SEQ], p["sgu_w"][l][:, :DEC_SEQ, :DEC_SEQ], 0.0)
    coef = jnp.repeat(w8.transpose(2, 1, 0), HEAD, axis=2)
    bias8 = jnp.repeat(p["sgu_b"][l][:, :DEC_SEQ].T, HEAD, axis=1)
    gid = jnp.arange(256) // HEAD
    tok = jnp.arange(TILE)
    same_blk = (tok[:, None] // TB) == (tok[None, :] // TB)
    return dict(
        norm_mix_g=p["norm_mix_g"][l][None], norm_ffn_g=p["norm_ffn_g"][l][None],
        w_in=p["w_in"][l].astype(BF16), w_out=p["w_out"][l].astype(BF16),
        conv_w=p["conv_w"][l], conv_b=p["conv_b"][l][None],
        wr=_block_diag(p["gate_r_w"][l]).astype(BF16), wi=_block_diag(p["gate_i_w"][l]).astype(BF16),
        br=p["gate_r_b"][l][None], bi=p["gate_i_b"][l][None], lam=p["lru_lambda"][l][None],
        sgu_g=p["sgu_norm_g"][l][None], wpair=wpair.astype(BF16), bmap=bmap, coef=coef, bias8=bias8,
        out_g=p["out_norm_g"][l][None],
        bd=(gid[:, None] == gid[None, :]).astype(BF16),
        rw_t=p["router_w"][l].T.astype(BF16), rbias=p["router_bias"][l][:, None],
        tri=(same_blk & (tok[:, None] < tok[None, :])).astype(BF16),
        bsel=((tok[:, None] // TB) == jnp.arange(128)[None, :]).astype(BF16),
        ltri=(jnp.arange(NE)[:, None] > jnp.arange(NE)[None, :]).astype(BF16),
        sg=p["shared_w_gate"][l].astype(BF16), su=p["shared_w_up"][l].astype(BF16),
        sd=p["shared_w_down"][l].astype(BF16),
    )


def kernel(x_prompt, x_sample, c_prompt, c_sample, state_lru_h, state_conv, ada_w, ada_b, norm_mix_g, norm_ffn_g, w_in, conv_w, conv_b, gate_r_w, gate_r_b, gate_i_w, gate_i_b, lru_lambda, sgu_norm_g, sgu_w, sgu_b, out_norm_g, w_out, router_w, router_bias, exp_w_gate, exp_w_up, exp_w_down, shared_w_gate, shared_w_up, shared_w_down, final_norm_g):
    p = dict(norm_mix_g=norm_mix_g, norm_ffn_g=norm_ffn_g, w_in=w_in, conv_w=conv_w, conv_b=conv_b,
             gate_r_w=gate_r_w, gate_r_b=gate_r_b, gate_i_w=gate_i_w, gate_i_b=gate_i_b,
             lru_lambda=lru_lambda, sgu_norm_g=sgu_norm_g, sgu_w=sgu_w, sgu_b=sgu_b,
             out_norm_g=out_norm_g, w_out=w_out, router_w=router_w, router_bias=router_bias,
             shared_w_gate=shared_w_gate, shared_w_up=shared_w_up, shared_w_down=shared_w_down)
    x_all = jnp.concatenate([x_prompt.transpose(1, 0, 2).reshape(NP, D),
                             x_sample.transpose(1, 0, 2).reshape(NS, D)], axis=0)
    mod = _modulations(jnp.concatenate([c_prompt, c_sample], axis=0), ada_w, ada_b)

    hp, cp, hs, cs, vs = [], [], [], [], []
    for l in range(DEPTH):
        w = _layer_weights(l, p)
        w["h0s"] = state_lru_h[l]
        w["convs"] = state_conv[l].transpose(1, 0, 2).reshape((CONV_W - 1) * SB, LW)
        yn, hlp, cvp, hls, cvs, v_s = _mixer(x_all, mod[l], w)
        x_mid, xn, lpos_t, w_t, cnt = _router(x_all, yn, mod[l], w)
        tab, used, nrows, meta = _plan(cnt)
        def per_block(a):
            return a.reshape(N_TILES, TOPK, BPT, TB).transpose(0, 2, 1, 3).reshape(NBLK, TOPK, TB)

        lpos_d = per_block(lpos_t)
        xs = _dispatch(tab, used, nrows, lpos_d, xn)
        ys = _experts(l, meta, xs, exp_w_gate, exp_w_up, exp_w_down)
        fin = final_norm_g[None] if l == DEPTH - 1 else None
        x_all = _combine(tab, nrows, ys, x_mid, xn, lpos_d, per_block(w_t), mod[l], w, fin)
        hp.append(hlp)
        cp.append(cvp.reshape(CONV_W - 1, PB, LW).transpose(1, 0, 2))
        hs.append(hls)
        cs.append(cvs.reshape(CONV_W - 1, SB, LW).transpose(1, 0, 2))
        vs.append(v_s.reshape(DEC_SEQ, SB, SW).transpose(1, 0, 2))

    y_prompt = x_all[:NP].reshape(SEQ, PB, D).transpose(1, 0, 2)
    y_sample = x_all[NP:].reshape(DEC_SEQ, SB, D).transpose(1, 0, 2)
    return (y_prompt, y_sample, jnp.stack(hp), jnp.stack(cp), jnp.stack(hs), jnp.stack(cs),
            jnp.stack(vs))
```

```python
import functools

import jax
import jax.numpy as jnp
from jax import lax
from jax.experimental import pallas as pl
from jax.experimental.pallas import tpu as pltpu

F32 = jnp.float32
BF16 = jnp.bfloat16
I32 = jnp.int32
I16 = jnp.int16

D = 1024
DEPTH = 4
PB = 8
SEQ = 2048
SB = 128
DEC_SEQ = 8
NP = PB * SEQ
NS = SB * DEC_SEQ
NT = NP + NS
TILE = 1024
N_TILES = NT // TILE
P_TILES = NP // TILE
LW = 512
SW = 512
HEAD = 64
CHUNK = 128
CONV_W = 4
LRU_C = 8.0
NE = 64
TOPK = 8
ED = 256
ROUTED_SCALE = 2.5
EPS = 1e-6
MOD_COLS = 6 * D
MOD_BLK = 1536
VMEM_LIMIT = 56 * 1024 * 1024

TB = 256
NBLK = NT // TB
PBLK = NP // TB
BPT = TILE // TB
ROWS = 16
LS = 3072
NCH = LS // ROWS
NCH_MIN = TB * TOPK // ROWS
SUB = 512
LAST = LS // SUB - 1
ETILE = 512
NSLOT_MAX = -(-(NT * TOPK + NBLK * NE * (ROWS - 1) + NE * (ETILE - ROWS)) // ETILE) * ETILE
N_ETILES = NSLOT_MAX // ETILE
TAB_TWO, TAB_THREE, TAB_ONE = 0, NE, 2 * NE
TAB_COUNTS = 2 * NE + NCH
TABW = TAB_COUNTS + 8
COPY_CLASSES = ((TAB_TWO, 2), (TAB_THREE, 3), (TAB_ONE, 1))

assert NSLOT_MAX % ETILE == 0 and LS >= TB * TOPK + NE * (ROWS - 1) and LS % SUB == 0


def _dot(a, b):
    return jnp.dot(a, b, preferred_element_type=F32)


def _const_spec(shape):
    nd = len(shape)
    return pl.BlockSpec(shape, lambda *_: (0,) * nd)


def _mod_kernel(c_ref, w_ref, b_ref, o_ref):
    s = jax.nn.silu(c_ref[...]).astype(BF16)
    o_ref[0] = _dot(s, w_ref[0].astype(BF16)) + b_ref[0]


def _modulations(c_all, ada_w, ada_b):
    nb = c_all.shape[0]
    return pl.pallas_call(
        _mod_kernel,
        grid=(DEPTH, MOD_COLS // MOD_BLK),
        in_specs=[
            pl.BlockSpec((nb, D), lambda l, j: (0, 0)),
            pl.BlockSpec((1, D, MOD_BLK), lambda l, j: (l, 0, j)),
            pl.BlockSpec((1, 1, MOD_BLK), lambda l, j: (l, 0, j)),
        ],
        out_specs=pl.BlockSpec((1, nb, MOD_BLK), lambda l, j: (l, 0, j)),
        out_shape=jax.ShapeDtypeStruct((DEPTH, nb, MOD_COLS), F32),
        compiler_params=pltpu.CompilerParams(
            dimension_semantics=("arbitrary", "arbitrary"), vmem_limit_bytes=VMEM_LIMIT),
        name="adaln_mod",
    )(c_all, ada_w, ada_b.reshape(DEPTH, 1, MOD_COLS))


def _sgu_prompt(v, wpair_ref, bmap_ref, vbuf, sbuf):
    for j in range(4):
        vbuf[j] = v[:, 128 * j:128 * (j + 1)]
    lane = lax.broadcasted_iota(I32, (CHUNK, 128), 1)
    left = lane < HEAD
    for b in range(PB):
        for j in range(4):
            vp = vbuf[j, pl.ds(b, CHUNK, stride=PB), :]
            rhs = jnp.concatenate(
                [jnp.where(left, vp, 0.0), jnp.where(left, 0.0, vp)], axis=0).astype(BF16)
            sj = _dot(wpair_ref[j], rhs) + bmap_ref[:, 128 * j:128 * (j + 1)]
            sbuf[j, pl.ds(b, CHUNK, stride=PB), :] = sj
    return jnp.concatenate([sbuf[j] for j in range(4)], axis=1)


def _sgu_sample(v, coef_ref, bias_ref):
    vt = [v[t * SB:(t + 1) * SB] for t in range(DEC_SEQ)]
    rows = []
    for p in range(DEC_SEQ):
        acc = bias_ref[p:p + 1, :] + coef_ref[0, p:p + 1, :] * vt[0]
        for q in range(1, p + 1):
            acc = acc + coef_ref[q, p:p + 1, :] * vt[q]
        rows.append(acc)
    return jnp.concatenate(rows, axis=0)


def _mixer_tile(G, x, sh, sc, h_in, tail_in, sgu_fn, r, hbuf):
    nst = TILE // G
    x3 = x.reshape(nst, G, D)
    ms = jnp.mean(x3 * x3, axis=-1, keepdims=True)
    xn = x3 * lax.rsqrt(ms + EPS) * r["norm_g"][...]
    xn = (xn * (1.0 + sc[None]) + sh[None]).reshape(TILE, D).astype(BF16)

    xa = _dot(xn, r["w_in"][:, 0:LW])
    xp = jnp.concatenate([tail_in, xa], axis=0)
    cw = r["conv_w"]
    xc = r["conv_b"][...] + cw[0:1, :] * xp[0:TILE]
    for k in range(1, CONV_W):
        xc = xc + cw[k:k + 1, :] * xp[k * G:k * G + TILE]
    new_tail = xa[TILE - (CONV_W - 1) * G:]
    xcb = xc.astype(BF16)
    half = LW // 2
    r_pre = jnp.concatenate(
        [_dot(xcb[:, :half], r["wr"][0]), _dot(xcb[:, half:], r["wr"][1])], axis=1) + r["br"][...]
    i_pre = jnp.concatenate(
        [_dot(xcb[:, :half], r["wi"][0]), _dot(xcb[:, half:], r["wi"][1])], axis=1) + r["bi"][...]
    rg = jax.nn.sigmoid(r_pre)
    ig = jax.nn.sigmoid(i_pre)
    log_a = -LRU_C * rg * jax.nn.softplus(-r["lam"][...])
    a = jnp.exp(log_a)
    th = jnp.tanh(log_a)
    num = -2.0 * th
    mult = jnp.where(num > 0.0, num * lax.rsqrt(num * (1.0 - th)), 0.0)
    u = mult * (ig * xc)
    h = h_in
    for s in range(nst):
        h = a[s * G:(s + 1) * G] * h + u[s * G:(s + 1) * G]
        hbuf[s * G:(s + 1) * G, :] = h
    ya = hbuf[...] * jax.nn.gelu(_dot(xn, r["w_in"][:, LW:2 * LW]))

    ug = jax.nn.gelu(_dot(xn, r["w_in"][:, 2 * LW:2 * LW + SW]))
    vg = jax.nn.gelu(_dot(xn, r["w_in"][:, 2 * LW + SW:]))
    v = vg * lax.rsqrt(jnp.mean(vg * vg, axis=-1, keepdims=True) + EPS) * r["sgu_g"][...]
    yb = ug * sgu_fn(v)

    y = jnp.concatenate([ya, yb], axis=1)
    bd = r["bd"][...]
    outs = []
    for j in range(D // 256):
        ysl = y[:, 256 * j:256 * (j + 1)]
        sq = ysl * ysl
        hi = sq.astype(BF16)
        lo = (sq - hi.astype(F32)).astype(BF16)
        msq = (_dot(hi, bd) + _dot(lo, bd)) * (1.0 / HEAD)
        outs.append(ysl * lax.rsqrt(msq + EPS) * r["out_g"][:, 256 * j:256 * (j + 1)])
    yn = jnp.concatenate(outs, axis=1).astype(BF16)
    return yn, h, new_tail, v


_MIX_IN = ("x", "mod", "norm_g", "w_in", "conv_w", "conv_b", "wr", "wi", "br", "bi", "lam",
           "sgu_g", "wpair", "bmap", "coef", "bias8", "out_g", "bd", "h0s", "convs")
_MIX_OUT = ("y", "hlast_p", "conv_p", "hlast_s", "conv_s", "v_s")
_MIX_SCR = ("hbuf", "vbuf", "sbuf", "h_carry", "tail_carry")


def _mixer_kernel(*refs):
    names = _MIX_IN + _MIX_OUT + _MIX_SCR
    r = dict(zip(names, refs))
    i = pl.program_id(0)

    @pl.when(i == 0)
    def _():
        r["h_carry"][...] = jnp.zeros((PB, LW), F32)
        r["tail_carry"][...] = jnp.zeros(((CONV_W - 1) * PB, LW), F32)

    @pl.when(i < P_TILES)
    def _():
        sh = r["mod"][0:PB, 0:D]
        sc = r["mod"][0:PB, D:2 * D]
        sgu = functools.partial(_sgu_prompt, wpair_ref=r["wpair"], bmap_ref=r["bmap"],
                                vbuf=r["vbuf"], sbuf=r["sbuf"])
        yn, h, tail, _ = _mixer_tile(PB, r["x"][...], sh, sc, r["h_carry"][...],
                                     r["tail_carry"][...], sgu, r, r["hbuf"])
        r["y"][...] = yn
        r["h_carry"][...] = h
        r["tail_carry"][...] = tail
        r["hlast_p"][...] = h
        r["conv_p"][...] = tail

    @pl.when(i == P_TILES)
    def _():
        sh = r["mod"][PB:PB + SB, 0:D]
        sc = r["mod"][PB:PB + SB, D:2 * D]
        sgu = functools.partial(_sgu_sample, coef_ref=r["coef"], bias_ref=r["bias8"])
        yn, h, tail, v = _mixer_tile(SB, r["x"][...], sh, sc, r["h0s"][...],
                                     r["convs"][...], sgu, r, r["hbuf"])
        r["y"][...] = yn
        r["hlast_s"][...] = h
        r["conv_s"][...] = tail
        r["v_s"][...] = v


def _mixer(x_all, mod_l, w):
    nb = PB + SB
    in_specs = [
        pl.BlockSpec((TILE, D), lambda i: (i, 0)),
        _const_spec((nb, MOD_COLS)),
        _const_spec((1, D)),
        _const_spec((D, 2 * LW + 2 * SW)),
        _const_spec((CONV_W, LW)),
        _const_spec((1, LW)),
        _const_spec((2, LW // 2, LW // 2)),
        _const_spec((2, LW // 2, LW // 2)),
        _const_spec((1, LW)),
        _const_spec((1, LW)),
        _const_spec((1, LW)),
        _const_spec((1, SW)),
        _const_spec((4, CHUNK, 2 * CHUNK)),
        _const_spec((CHUNK, SW)),
        _const_spec((DEC_SEQ, DEC_SEQ, SW)),
        _const_spec((DEC_SEQ, SW)),
        _const_spec((1, D)),
        _const_spec((256, 256)),
        _const_spec((SB, LW)),
        _const_spec(((CONV_W - 1) * SB, LW)),
    ]
    out_shape = (
        jax.ShapeDtypeStruct((NT, D), BF16),
        jax.ShapeDtypeStruct((PB, LW), F32),
        jax.ShapeDtypeStruct(((CONV_W - 1) * PB, LW), F32),
        jax.ShapeDtypeStruct((SB, LW), F32),
        jax.ShapeDtypeStruct(((CONV_W - 1) * SB, LW), F32),
        jax.ShapeDtypeStruct((NS, SW), F32),
    )
    out_specs = (
        pl.BlockSpec((TILE, D), lambda i: (i, 0)),
        _const_spec((PB, LW)),
        _const_spec(((CONV_W - 1) * PB, LW)),
        _const_spec((SB, LW)),
        _const_spec(((CONV_W - 1) * SB, LW)),
        _const_spec((NS, SW)),
    )
    scratch = [
        pltpu.VMEM((TILE, LW), F32),
        pltpu.VMEM((4, TILE, 128), F32),
        pltpu.VMEM((4, TILE, 128), F32),
        pltpu.VMEM((PB, LW), F32),
        pltpu.VMEM(((CONV_W - 1) * PB, LW), F32),
    ]
    return pl.pallas_call(
        _mixer_kernel,
        grid=(N_TILES,),
        in_specs=in_specs,
        out_specs=out_specs,
        out_shape=out_shape,
        scratch_shapes=scratch,
        compiler_params=pltpu.CompilerParams(
            dimension_semantics=("arbitrary",), vmem_limit_bytes=VMEM_LIMIT),
        name="mixer",
    )(x_all, mod_l, w["norm_mix_g"], w["w_in"], w["conv_w"], w["conv_b"], w["wr"], w["wi"],
      w["br"], w["bi"], w["lam"], w["sgu_g"], w["wpair"], w["bmap"], w["coef"], w["bias8"],
      w["out_g"], w["bd"], w["h0s"], w["convs"])


def _router_tile(G, r, gm, sc, sh):
    nst = TILE // G
    out = _dot(r["y"][...], r["w_out"][...])
    xm3 = r["x"][...].reshape(nst, G, D) + gm[None] * out.reshape(nst, G, D)
    r["x_mid"][...] = xm3.reshape(TILE, D)
    ms = jnp.mean(xm3 * xm3, axis=-1, keepdims=True)
    xn3 = xm3 * lax.rsqrt(ms + EPS) * r["norm_g"][...]
    xn = (xn3 * (1.0 + sc[None]) + sh[None]).reshape(TILE, D).astype(BF16)
    r["xn"][...] = xn

    logits = lax.dot_general(r["rw_t"][...], xn, (((1,), (1,)), ((), ())),
                             preferred_element_type=F32)
    sig = jax.nn.sigmoid(logits)
    sel = sig + r["rbias"][...]
    eiota = lax.broadcasted_iota(I32, (NE, TILE), 0)
    chosen = jnp.zeros((NE, TILE), F32)
    w_rows, hots = [], []
    for _ in range(TOPK):
        m = jnp.max(sel, axis=0, keepdims=True)
        idx = jnp.min(jnp.where(sel == m, eiota, NE), axis=0, keepdims=True)
        hot = eiota == idx
        hots.append(hot)
        w_rows.append(jnp.sum(jnp.where(hot, sig, 0.0), axis=0, keepdims=True))
        chosen = jnp.where(hot, 1.0, chosen)
        sel = jnp.where(hot, -jnp.inf, sel)
    den = w_rows[0]
    for k in range(1, TOPK):
        den = den + w_rows[k]
    scale = ROUTED_SCALE / den
    r["w_t"][0] = jnp.concatenate([wk * scale for wk in w_rows], axis=0)

    cb = chosen.astype(BF16)
    rank = _dot(cb, r["tri"][...])
    cnt = _dot(cb, r["bsel"][...])
    groups = jnp.floor((cnt + (ROWS - 1.0)) * (1.0 / ROWS))
    start = ROWS * _dot(r["ltri"][...], groups.astype(BF16))
    pos = rank + jnp.concatenate(
        [jnp.broadcast_to(start[:, j:j + 1], (NE, TB)) for j in range(BPT)], axis=1)
    lp_rows = [jnp.sum(jnp.where(hot, pos, 0.0), axis=0, keepdims=True) for hot in hots]
    r["lpos_t"][0] = jnp.concatenate(lp_rows, axis=0).astype(I32)
    r["cnt"][0] = cnt


_RT_IN = ("x", "y", "mod", "w_out", "norm_g", "rw_t", "rbias", "tri", "bsel", "ltri")
_RT_OUT = ("x_mid", "xn", "lpos_t", "w_t", "cnt")


def _router_kernel(*refs):
    r = dict(zip(_RT_IN + _RT_OUT, refs))
    i = pl.program_id(0)

    def mods(lo, n):
        m = r["mod"]
        return m[lo:lo + n, 2 * D:3 * D], m[lo:lo + n, 4 * D:5 * D], m[lo:lo + n, 3 * D:4 * D]

    @pl.when(i < P_TILES)
    def _():
        _router_tile(PB, r, *mods(0, PB))

    @pl.when(i == P_TILES)
    def _():
        _router_tile(SB, r, *mods(PB, SB))


def _router(x_all, y_all, mod_l, w):
    nb = PB + SB
    in_specs = [
        pl.BlockSpec((TILE, D), lambda i: (i, 0)),
        pl.BlockSpec((TILE, D), lambda i: (i, 0)),
        _const_spec((nb, MOD_COLS)),
        _const_spec((D, D)),
        _const_spec((1, D)),
        _const_spec((NE, D)),
        _const_spec((NE, 1)),
        _const_spec((TILE, TILE)),
        _const_spec((TILE, 128)),
        _const_spec((NE, NE)),
    ]
    out_shape = (
        jax.ShapeDtypeStruct((NT, D), F32),
        jax.ShapeDtypeStruct((NT, D), BF16),
        jax.ShapeDtypeStruct((N_TILES, TOPK, TILE), I32),
        jax.ShapeDtypeStruct((N_TILES, TOPK, TILE), F32),
        jax.ShapeDtypeStruct((N_TILES, NE, 128), F32),
    )
    out_specs = (
        pl.BlockSpec((TILE, D), lambda i: (i, 0)),
        pl.BlockSpec((TILE, D), lambda i: (i, 0)),
        pl.BlockSpec((1, TOPK, TILE), lambda i: (i, 0, 0)),
        pl.BlockSpec((1, TOPK, TILE), lambda i: (i, 0, 0)),
        pl.BlockSpec((1, NE, 128), lambda i: (i, 0, 0)),
    )
    return pl.pallas_call(
        _router_kernel,
        grid=(N_TILES,),
        in_specs=in_specs,
        out_specs=out_specs,
        out_shape=out_shape,
        compiler_params=pltpu.CompilerParams(
            dimension_semantics=("arbitrary",), vmem_limit_bytes=VMEM_LIMIT),
        name="router",
    )(x_all, y_all, mod_l, w["w_out"], w["norm_ffn_g"], w["rw_t"], w["rbias"], w["tri"],
      w["bsel"], w["ltri"])


def _group_copies(tab_ref, make_copy):
    for cls, (base, nch) in enumerate(COPY_CLASSES):
        def body(i, carry, base=base, nch=nch, cls=cls):
            p = tab_ref[0, 0, base + i]
            l = pl.multiple_of((p & 255) * ROWS, ROWS)
            g = pl.multiple_of((p >> 8) * ROWS, ROWS)
            make_copy(l, g, nch * ROWS).start(priority=cls % 2)
            return carry

        lax.fori_loop(0, tab_ref[0, 0, TAB_COUNTS + cls], body, 0)


def _chunk_waits(nrows, whole_copy, chunk_copy):
    whole_copy.wait()

    def body(i, carry):
        chunk_copy.wait()
        return carry

    lax.fori_loop(0, (nrows - NCH_MIN * ROWS) // ROWS, body, 0)


def _dispatch_kernel(tab_ref, fill_ref, nrows_ref, lpos_ref, xn_ref, xs_ref, loc, zbuf, sem):
    b = pl.program_id(0)
    slot = b % 2

    def gap_chunk(e, j):
        row = pl.multiple_of(fill_ref[1 + e] + j * ROWS, ROWS)
        return pltpu.make_async_copy(
            zbuf.at[pl.ds(0, ROWS)], xs_ref.at[pl.ds(row, ROWS)], sem.at[2])

    def zero_tile(t):
        return pltpu.make_async_copy(
            zbuf, xs_ref.at[pl.ds(pl.multiple_of(t * ETILE, ETILE), ETILE)], sem.at[2])

    def start_gap(e, carry):
        lax.fori_loop(0, fill_ref[1 + NE + e], lambda j, c: (gap_chunk(e, j).start(), c)[1], 0)
        return carry

    def wait_gap(e, carry):
        lax.fori_loop(0, fill_ref[1 + NE + e], lambda j, c: (gap_chunk(e, j).wait(), c)[1], 0)
        return carry

    def start_tile(t, carry):
        zero_tile(t).start()
        return carry

    def wait_tile(t, carry):
        zero_tile(t).wait()
        return carry

    @pl.when(b == 0)
    def _():
        zbuf[...] = jnp.zeros((ETILE, D), BF16)
        lax.fori_loop(0, NE, start_gap, 0)
        lax.fori_loop(fill_ref[0], N_ETILES, start_tile, 0)

    xb = xn_ref[...]
    lp = lpos_ref[0].astype(I16)
    one = jnp.ones((), BF16)

    def sort_rows(c):
        siota = (lax.broadcasted_iota(I32, (SUB, TB), 0) + c * SUB).astype(I16)
        p = jnp.zeros((SUB, TB), BF16)
        for k in range(TOPK):
            p = jnp.where(siota == lp[k:k + 1, :], one, p)
        loc[slot, c * SUB:(c + 1) * SUB, :] = _dot(p, xb).astype(BF16)

    for c in range(LAST):
        sort_rows(c)
    tail_used = nrows_ref[b] > LAST * SUB

    @pl.when(tail_used)
    def _():
        sort_rows(LAST)

    @pl.when(jnp.logical_not(tail_used))
    def _():
        loc[slot, LAST * SUB:, :] = jnp.zeros((SUB, D), BF16)

    _group_copies(tab_ref, lambda l, g, n: pltpu.make_async_copy(
        loc.at[slot, pl.ds(l, n)], xs_ref.at[pl.ds(g, n)], sem.at[slot]))

    def drain(s, nrows):
        _chunk_waits(
            nrows,
            pltpu.make_async_copy(loc.at[s, pl.ds(0, NCH_MIN * ROWS)],
                                  xs_ref.at[pl.ds(0, NCH_MIN * ROWS)], sem.at[s]),
            pltpu.make_async_copy(loc.at[s, pl.ds(0, ROWS)], xs_ref.at[pl.ds(0, ROWS)], sem.at[s]))

    @pl.when(b > 0)
    def _():
        drain(1 - slot, nrows_ref[jnp.maximum(b - 1, 0)])

    @pl.when(b == NBLK - 1)
    def _():
        drain(slot, nrows_ref[b])
        lax.fori_loop(0, NE, wait_gap, 0)
        lax.fori_loop(fill_ref[0], N_ETILES, wait_tile, 0)


def _dispatch(tab, fill_from, nrows, lpos_d, xn):
    return pl.pallas_call(
        _dispatch_kernel,
        grid=(NBLK,),
        in_specs=[
            pl.BlockSpec((1, 1, TABW), lambda b: (b, 0, 0), memory_space=pltpu.SMEM),
            pl.BlockSpec(memory_space=pltpu.SMEM),
            pl.BlockSpec(memory_space=pltpu.SMEM),
            pl.BlockSpec((1, TOPK, TB), lambda b: (b, 0, 0)),
            pl.BlockSpec((TB, D), lambda b: (b, 0)),
        ],
        out_specs=pl.BlockSpec(memory_space=pl.ANY),
        out_shape=jax.ShapeDtypeStruct((NSLOT_MAX, D), BF16),
        scratch_shapes=[pltpu.VMEM((2, LS, D), BF16), pltpu.VMEM((ETILE, D), BF16),
                        pltpu.SemaphoreType.DMA((3,))],
        compiler_params=pltpu.CompilerParams(
            dimension_semantics=("arbitrary",), vmem_limit_bytes=VMEM_LIMIT),
        name="dispatch",
    )(tab, fill_from, nrows, lpos_d, xn)


def _expert_kernel(l, tin_ref, exp_ref, flag_ref, new_ref, wslot_ref, next_ref, xs_ref, wg_hbm,
                   wu_hbm, wd_hbm, ys_ref, wg_f, wu_f, wd_f, wg_b, wu_b, wd_b, wsem):
    v = pl.program_id(0)
    flag = flag_ref[v]

    def weight_copies(e, p):
        return (pltpu.make_async_copy(wg_hbm.at[l, e], wg_f.at[p], wsem.at[p]),
                pltpu.make_async_copy(wu_hbm.at[l, e], wu_f.at[p], wsem.at[p]),
                pltpu.make_async_copy(wd_hbm.at[l, e], wd_f.at[p], wsem.at[p]))

    @pl.when(v == 0)
    def _():
        for c in weight_copies(exp_ref[0], 0):
            c.start()

    @pl.when(new_ref[v] == 1)
    def _():
        p = wslot_ref[v]
        for c in weight_copies(exp_ref[v], p):
            c.wait()

        @pl.when(next_ref[v] >= 0)
        def _():
            for c in weight_copies(next_ref[v], 1 - p):
                c.start()

        wg_b[...] = wg_f[p].astype(BF16)
        wu_b[...] = wu_f[p].astype(BF16)
        wd_b[...] = wd_f[p].astype(BF16)

    @pl.when(flag == 1)
    def _():
        x = xs_ref[...]
        hid = jax.nn.silu(_dot(x, wg_b[...])) * _dot(x, wu_b[...])
        ys_ref[...] = _dot(hid.astype(BF16), wd_b[...]).astype(BF16)


def _experts(l, meta, xs, wg, wu, wd):
    nmeta = len(meta)
    grid_spec = pltpu.PrefetchScalarGridSpec(
        num_scalar_prefetch=nmeta,
        grid=(N_ETILES,),
        in_specs=[
            pl.BlockSpec((ETILE, D), lambda v, ti, *_: (ti[v], 0)),
            pl.BlockSpec(memory_space=pl.ANY),
            pl.BlockSpec(memory_space=pl.ANY),
            pl.BlockSpec(memory_space=pl.ANY),
        ],
        out_specs=pl.BlockSpec((ETILE, D), lambda v, ti, *_: (ti[v], 0)),
        scratch_shapes=[pltpu.VMEM((2, D, ED), F32), pltpu.VMEM((2, D, ED), F32),
                        pltpu.VMEM((2, ED, D), F32), pltpu.VMEM((D, ED), BF16),
                        pltpu.VMEM((D, ED), BF16), pltpu.VMEM((ED, D), BF16),
                        pltpu.SemaphoreType.DMA((2,))],
    )
    return pl.pallas_call(
        functools.partial(_expert_kernel, l),
        grid_spec=grid_spec,
        out_shape=jax.ShapeDtypeStruct((NSLOT_MAX, D), BF16),
        input_output_aliases={nmeta: 0},
        compiler_params=pltpu.CompilerParams(
            dimension_semantics=("arbitrary",), vmem_limit_bytes=VMEM_LIMIT),
        name="experts",
    )(*meta, xs, wg, wu, wd)


def _combine_tile(G, r, slot, gf, final, out_ref):
    nst = TB // G
    xb = r["xn"][...]
    hid = jax.nn.silu(_dot(xb, r["sg"][...])) * _dot(xb, r["su"][...])
    moe = _dot(hid.astype(BF16), r["sd"][...])
    _chunk_waits(
        r["nrows"][pl.program_id(0)],
        pltpu.make_async_copy(r["ys"].at[pl.ds(0, NCH_MIN * ROWS)],
                              r["loc"].at[slot, pl.ds(0, NCH_MIN * ROWS)], r["sem"].at[slot]),
        pltpu.make_async_copy(r["ys"].at[pl.ds(0, ROWS)], r["loc"].at[slot, pl.ds(0, ROWS)],
                              r["sem"].at[slot]))
    lp = r["lpos"][0].astype(I16)
    wts = r["wts"][0].astype(BF16)

    def unsort_rows(c):
        siota = (lax.broadcasted_iota(I32, (SUB, TB), 0) + c * SUB).astype(I16)
        p = jnp.zeros((SUB, TB), BF16)
        for k in range(TOPK):
            p = jnp.where(siota == lp[k:k + 1, :], wts[k:k + 1, :], p)
        return lax.dot_general(
            p, r["loc"][slot, c * SUB:(c + 1) * SUB, :], (((0,), (0,)), ((), ())),
            preferred_element_type=F32)

    def finish(moe):
        xo = r["x_mid"][...].reshape(nst, G, D) + gf[None] * moe.reshape(nst, G, D)
        if final:
            ms = jnp.mean(xo * xo, axis=-1, keepdims=True)
            xo = xo * lax.rsqrt(ms + EPS) * r["fin_g"][...]
        out_ref[...] = xo.reshape(TB, D)

    for c in range(LAST):
        moe = moe + unsort_rows(c)
    tail_used = r["nrows"][pl.program_id(0)] > LAST * SUB

    @pl.when(tail_used)
    def _():
        finish(moe + unsort_rows(LAST))

    @pl.when(jnp.logical_not(tail_used))
    def _():
        finish(moe)


def _combine_kernel(final, *refs):
    names = ["tab", "tab_next", "nrows", "ys", "x_mid", "xn", "lpos", "wts", "mod", "sg", "su", "sd"]
    names += ["fin_g"] if final else []
    names += ["x_out"] + (["s_out"] if final else []) + ["loc", "sem"]
    r = dict(zip(names, refs))
    b = pl.program_id(0)
    slot = b % 2

    def gather(tab_ref, s):
        r["loc"][s, NCH_MIN * ROWS:, :] = jnp.zeros((LS - NCH_MIN * ROWS, D), BF16)
        _group_copies(tab_ref, lambda l, g, n: pltpu.make_async_copy(
            r["ys"].at[pl.ds(g, n)], r["loc"].at[s, pl.ds(l, n)], r["sem"].at[s]))

    @pl.when(b == 0)
    def _():
        gather(r["tab"], 0)

    @pl.when(b + 1 < NBLK)
    def _():
        gather(r["tab_next"], 1 - slot)

    @pl.when(b < PBLK)
    def _():
        _combine_tile(PB, r, slot, r["mod"][0:PB, 5 * D:6 * D], final, r["x_out"])

    @pl.when(b >= PBLK)
    def _():
        _combine_tile(SB, r, slot, r["mod"][PB:PB + SB, 5 * D:6 * D], final,
                      r["s_out"] if final else r["x_out"])


def _combine(tab, nrows, ys, x_mid, xn, lpos_c, wts, mod_l, w, fin_g):
    final = fin_g is not None
    nb = PB + SB
    in_specs = [
        pl.BlockSpec((1, 1, TABW), lambda b: (b, 0, 0), memory_space=pltpu.SMEM),
        pl.BlockSpec((1, 1, TABW), lambda b: (jnp.minimum(b + 1, NBLK - 1), 0, 0),
                     memory_space=pltpu.SMEM),
        pl.BlockSpec(memory_space=pltpu.SMEM),
        pl.BlockSpec(memory_space=pl.ANY),
        pl.BlockSpec((TB, D), lambda b: (b, 0)),
        pl.BlockSpec((TB, D), lambda b: (b, 0)),
        pl.BlockSpec((1, TOPK, TB), lambda b: (b, 0, 0)),
        pl.BlockSpec((1, TOPK, TB), lambda b: (b, 0, 0)),
        _const_spec((nb, MOD_COLS)),
        _const_spec((D, ED)),
        _const_spec((D, ED)),
        _const_spec((ED, D)),
    ]
    args = [tab, tab, nrows, ys, x_mid, xn, lpos_c, wts, mod_l, w["sg"], w["su"], w["sd"]]
    out_specs = pl.BlockSpec((TB, D), lambda b: (b, 0))
    out_shape = jax.ShapeDtypeStruct((NT, D), F32)
    if final:
        in_specs.append(_const_spec((1, D)))
        args.append(fin_g)
        out_specs = (pl.BlockSpec((TB, D), lambda b: (jnp.minimum(b, PBLK - 1), 0)),
                     pl.BlockSpec((TB, D), lambda b: (jnp.maximum(b - PBLK, 0), 0)))
        out_shape = (jax.ShapeDtypeStruct((NP, D), F32), jax.ShapeDtypeStruct((NS, D), F32))
    return pl.pallas_call(
        functools.partial(_combine_kernel, final),
        grid=(NBLK,),
        in_specs=in_specs,
        out_specs=out_specs,
        out_shape=out_shape,
        scratch_shapes=[pltpu.VMEM((2, LS, D), BF16), pltpu.SemaphoreType.DMA((2,))],
        compiler_params=pltpu.CompilerParams(
            dimension_semantics=("arbitrary",), vmem_limit_bytes=VMEM_LIMIT),
        name="combine_final" if final else "combine",
    )(*args)


def _plan(cnt):
    c = cnt[:, :, :BPT].transpose(0, 2, 1).reshape(NBLK, NE).astype(I32)
    pc = (c + ROWS - 1) // ROWS * ROWS
    lend = jnp.cumsum(pc, axis=1)
    lstart = lend - pc
    etot = jnp.sum(pc, axis=0)
    epad = (etot + ETILE - 1) // ETILE * ETILE
    eend = jnp.cumsum(epad)
    eoff = eend - epad
    gstart = eoff[None, :] + jnp.cumsum(pc, axis=0) - pc
    eids = jnp.arange(NE, dtype=I32)

    def compact(mask, vals, width):
        pos = jnp.cumsum(mask.astype(I32), axis=1) - 1
        sel = mask[:, None, :] & (pos[:, None, :] == jnp.arange(width, dtype=I32)[None, :, None])
        return jnp.sum(jnp.where(sel, vals[:, None, :], 0), axis=-1), jnp.sum(mask.astype(I32), axis=1)

    nchunk = pc // ROWS
    packed = (gstart // ROWS) * 256 + lstart // ROWS
    two, n_two = compact(nchunk == 2, packed, NE)
    three, n_three = compact(nchunk == 3, packed, NE)
    lj = jnp.arange(NCH, dtype=I32)
    l8 = lj * ROWS
    ej = jnp.minimum(jnp.sum((lend[:, None, :] <= l8[None, :, None]).astype(I32), axis=-1), NE - 1)
    hot = ej[..., None] == eids
    row = jnp.sum(jnp.where(hot, (gstart - lstart)[:, None, :], 0), axis=-1) + l8[None, :]
    single = (nchunk != 2) & (nchunk != 3)
    is_one = (l8[None, :] < lend[:, -1:]) & jnp.any(hot & single[:, None, :], axis=-1)
    one, n_one = compact(is_one, (row // ROWS) * 256 + lj[None, :], NCH)
    counts = jnp.stack([n_two, n_three, n_one] + [jnp.zeros_like(n_two)] * (TABW - TAB_COUNTS - 3), axis=1)
    tab = jnp.concatenate([two, three, one, counts], axis=1).astype(I32).reshape(NBLK, 1, TABW)

    def lookup(table, idx):
        return jnp.sum(jnp.where(idx[..., None] == eids, table, 0), axis=-1)

    used = eend[-1] // ETILE
    v = jnp.arange(N_ETILES, dtype=I32)
    tin = jnp.minimum(v, used - 1)
    e = jnp.minimum(jnp.sum((eend[None, :] <= (tin * ETILE)[:, None]).astype(I32), axis=1), NE - 1)
    flag = (v < used).astype(I32)
    new = jnp.concatenate([jnp.ones((1,), I32), (e[1:] != e[:-1]).astype(I32)])
    live = etot > 0
    wslot = lookup((jnp.cumsum(live.astype(I32)) - 1) % 2, e)
    later = live[None, :] & (eids[None, :] > eids[:, None])
    nxt = jnp.min(jnp.where(later, eids[None, :], NE), axis=1)
    nxt = lookup(jnp.where(nxt < NE, nxt, -1), e)
    meta = (tin, e, flag, new, wslot.astype(I32), nxt.astype(I32))
    fill = jnp.concatenate([used[None], eoff + etot, (epad - etot) // ROWS]).astype(I32)
    return tab, fill, lend[:, -1], meta


def _block_diag(wh):
    out = jnp.zeros((2, 4, HEAD, 4, HEAD), wh.dtype)
    for hh in range(4):
        out = out.at[:, hh, :, hh, :].set(wh.reshape(2, 4, HEAD, HEAD)[:, hh])
    return out.reshape(2, 4 * HEAD, 4 * HEAD)


def _layer_weights(l, p):
    tril = jnp.tril(jnp.ones((CHUNK, CHUNK), bool))
    wt = jnp.where(tril, p["sgu_w"][l], 0.0)
    wpair = wt.reshape(4, 2, CHUNK, CHUNK).transpose(0, 2, 1, 3).reshape(4, CHUNK, 2 * CHUNK)
    bmap = jnp.repeat(p["sgu_b"][l].T, HEAD, axis=1)
    w8 = jnp.where(tril[:DEC_SEQ, :DEC_SEQ], p["sgu_w"][l][:, :DEC_SEQ, :DEC_SEQ], 0.0)
    coef = jnp.repeat(w8.transpose(2, 1, 0), HEAD, axis=2)
    bias8 = jnp.repeat(p["sgu_b"][l][:, :DEC_SEQ].T, HEAD, axis=1)
    gid = jnp.arange(256) // HEAD
    tok = jnp.arange(TILE)
    same_blk = (tok[:, None] // TB) == (tok[None, :] // TB)
    return dict(
        norm_mix_g=p["norm_mix_g"][l][None], norm_ffn_g=p["norm_ffn_g"][l][None],
        w_in=p["w_in"][l].astype(BF16), w_out=p["w_out"][l].astype(BF16),
        conv_w=p["conv_w"][l], conv_b=p["conv_b"][l][None],
        wr=_block_diag(p["gate_r_w"][l]).astype(BF16), wi=_block_diag(p["gate_i_w"][l]).astype(BF16),
        br=p["gate_r_b"][l][None], bi=p["gate_i_b"][l][None], lam=p["lru_lambda"][l][None],
        sgu_g=p["sgu_norm_g"][l][None], wpair=wpair.astype(BF16), bmap=bmap, coef=coef, bias8=bias8,
        out_g=p["out_norm_g"][l][None],
        bd=(gid[:, None] == gid[None, :]).astype(BF16),
        rw_t=p["router_w"][l].T.astype(BF16), rbias=p["router_bias"][l][:, None],
        tri=(same_blk & (tok[:, None] < tok[None, :])).astype(BF16),
        bsel=((tok[:, None] // TB) == jnp.arange(128)[None, :]).astype(BF16),
        ltri=(jnp.arange(NE)[:, None] > jnp.arange(NE)[None, :]).astype(BF16),
        sg=p["shared_w_gate"][l].astype(BF16), su=p["shared_w_up"][l].astype(BF16),
        sd=p["shared_w_down"][l].astype(BF16),
    )


def kernel(x_prompt, x_sample, c_prompt, c_sample, state_lru_h, state_conv, ada_w, ada_b, norm_mix_g, norm_ffn_g, w_in, conv_w, conv_b, gate_r_w, gate_r_b, gate_i_w, gate_i_b, lru_lambda, sgu_norm_g, sgu_w, sgu_b, out_norm_g, w_out, router_w, router_bias, exp_w_gate, exp_w_up, exp_w_down, shared_w_gate, shared_w_up, shared_w_down, final_norm_g):
    p = dict(norm_mix_g=norm_mix_g, norm_ffn_g=norm_ffn_g, w_in=w_in, conv_w=conv_w, conv_b=conv_b,
             gate_r_w=gate_r_w, gate_r_b=gate_r_b, gate_i_w=gate_i_w, gate_i_b=gate_i_b,
             lru_lambda=lru_lambda, sgu_norm_g=sgu_norm_g, sgu_w=sgu_w, sgu_b=sgu_b,
             out_norm_g=out_norm_g, w_out=w_out, router_w=router_w, router_bias=router_bias,
             shared_w_gate=shared_w_gate, shared_w_up=shared_w_up, shared_w_down=shared_w_down)
    x_all = jnp.concatenate([x_prompt.transpose(1, 0, 2).reshape(NP, D),
                             x_sample.transpose(1, 0, 2).reshape(NS, D)], axis=0)
    mod = _modulations(jnp.concatenate([c_prompt, c_sample], axis=0), ada_w, ada_b)

    hp, cp, hs, cs, vs = [], [], [], [], []
    for l in range(DEPTH):
        w = _layer_weights(l, p)
        w["h0s"] = state_lru_h[l]
        w["convs"] = state_conv[l].transpose(1, 0, 2).reshape((CONV_W - 1) * SB, LW)
        yn, hlp, cvp, hls, cvs, v_s = _mixer(x_all, mod[l], w)
        x_mid, xn, lpos_t, w_t, cnt = _router(x_all, yn, mod[l], w)
        tab, used, nrows, meta = _plan(cnt)
        def per_block(a):
            return a.reshape(N_TILES, TOPK, BPT, TB).transpose(0, 2, 1, 3).reshape(NBLK, TOPK, TB)

        lpos_d = per_block(lpos_t)
        xs = _dispatch(tab, used, nrows, lpos_d, xn)
        ys = _experts(l, meta, xs, exp_w_gate, exp_w_up, exp_w_down)
        fin = final_norm_g[None] if l == DEPTH - 1 else None
        x_all = _combine(tab, nrows, ys, x_mid, xn, lpos_d, per_block(w_t), mod[l], w, fin)
        hp.append(hlp)
        cp.append(cvp.reshape(CONV_W - 1, PB, LW).transpose(1, 0, 2))
        hs.append(hls)
        cs.append(cvs.reshape(CONV_W - 1, SB, LW).transpose(1, 0, 2))
        vs.append(v_s.reshape(DEC_SEQ, SB, SW).transpose(1, 0, 2))

    y_prompt = x_all[0].reshape(SEQ, PB, D).transpose(1, 0, 2)
    y_sample = x_all[1].reshape(DEC_SEQ, SB, D).transpose(1, 0, 2)
    return (y_prompt, y_sample, jnp.stack(hp), jnp.stack(cp), jnp.stack(hs), jnp.stack(cs),
            jnp.stack(vs))
```

```python
import functools

import jax
import jax.numpy as jnp
from jax import lax
from jax.experimental import pallas as pl
from jax.experimental.pallas import tpu as pltpu

F32 = jnp.float32
BF16 = jnp.bfloat16
I32 = jnp.int32
I16 = jnp.int16

D = 1024
DEPTH = 4
PB = 8
SEQ = 2048
SB = 128
DEC_SEQ = 8
NP = PB * SEQ
NS = SB * DEC_SEQ
NT = NP + NS
TILE = 1024
N_TILES = NT // TILE
P_TILES = NP // TILE
LW = 512
SW = 512
HEAD = 64
CHUNK = 128
CONV_W = 4
LRU_C = 8.0
NE = 64
TOPK = 8
ED = 256
ROUTED_SCALE = 2.5
EPS = 1e-6
MOD_COLS = 6 * D
MOD_BLK = 1536
VMEM_LIMIT = 56 * 1024 * 1024

TB = 256
NBLK = NT // TB
PBLK = NP // TB
BPT = TILE // TB
ROWS = 16
LS = 3072
NCH = LS // ROWS
NCH_MIN = TB * TOPK // ROWS
SUB = 512
LAST = LS // SUB - 1
ETILE = 1024
NSLOT_MAX = -(-(NT * TOPK + NBLK * NE * (ROWS - 1) + NE * (ETILE - ROWS)) // ETILE) * ETILE
N_ETILES = NSLOT_MAX // ETILE
TAB_TWO, TAB_THREE, TAB_ONE = 0, NE, 2 * NE
TAB_COUNTS = 2 * NE + NCH
TABW = TAB_COUNTS + 8
COPY_CLASSES = ((TAB_TWO, 2), (TAB_THREE, 3), (TAB_ONE, 1))
LOCAL_BITS = 8

assert NSLOT_MAX % ETILE == 0 and LS >= TB * TOPK + NE * (ROWS - 1) and LS % SUB == 0
assert NCH <= 1 << LOCAL_BITS and NSLOT_MAX // ROWS < 1 << (31 - LOCAL_BITS)


def _dot(a, b):
    return jnp.dot(a, b, preferred_element_type=F32)


def _const_spec(shape):
    nd = len(shape)
    return pl.BlockSpec(shape, lambda *_: (0,) * nd)


def _mod_kernel(c_ref, w_ref, b_ref, o_ref):
    s = jax.nn.silu(c_ref[...]).astype(BF16)
    o_ref[0] = _dot(s, w_ref[0].astype(BF16)) + b_ref[0]


def _modulations(c_all, ada_w, ada_b):
    nb = c_all.shape[0]
    return pl.pallas_call(
        _mod_kernel,
        grid=(DEPTH, MOD_COLS // MOD_BLK),
        in_specs=[
            pl.BlockSpec((nb, D), lambda l, j: (0, 0)),
            pl.BlockSpec((1, D, MOD_BLK), lambda l, j: (l, 0, j)),
            pl.BlockSpec((1, 1, MOD_BLK), lambda l, j: (l, 0, j)),
        ],
        out_specs=pl.BlockSpec((1, nb, MOD_BLK), lambda l, j: (l, 0, j)),
        out_shape=jax.ShapeDtypeStruct((DEPTH, nb, MOD_COLS), F32),
        compiler_params=pltpu.CompilerParams(
            dimension_semantics=("arbitrary", "arbitrary"), vmem_limit_bytes=VMEM_LIMIT),
        name="adaln_mod",
    )(c_all, ada_w, ada_b.reshape(DEPTH, 1, MOD_COLS))


def _sgu_prompt(v, wpair_ref, bmap_ref, vbuf, sbuf):
    for j in range(4):
        vbuf[j] = v[:, 128 * j:128 * (j + 1)]
    lane = lax.broadcasted_iota(I32, (CHUNK, 128), 1)
    left = lane < HEAD
    for b in range(PB):
        for j in range(4):
            vp = vbuf[j, pl.ds(b, CHUNK, stride=PB), :]
            rhs = jnp.concatenate(
                [jnp.where(left, vp, 0.0), jnp.where(left, 0.0, vp)], axis=0).astype(BF16)
            sj = _dot(wpair_ref[j], rhs) + bmap_ref[:, 128 * j:128 * (j + 1)]
            sbuf[j, pl.ds(b, CHUNK, stride=PB), :] = sj
    return jnp.concatenate([sbuf[j] for j in range(4)], axis=1)


def _sgu_sample(v, coef_ref, bias_ref):
    vt = [v[t * SB:(t + 1) * SB] for t in range(DEC_SEQ)]
    rows = []
    for p in range(DEC_SEQ):
        acc = bias_ref[p:p + 1, :] + coef_ref[0, p:p + 1, :] * vt[0]
        for q in range(1, p + 1):
            acc = acc + coef_ref[q, p:p + 1, :] * vt[q]
        rows.append(acc)
    return jnp.concatenate(rows, axis=0)


def _mixer_tile(G, x, sh, sc, h_in, tail_in, sgu_fn, r, hbuf):
    nst = TILE // G
    x3 = x.reshape(nst, G, D)
    ms = jnp.mean(x3 * x3, axis=-1, keepdims=True)
    xn = x3 * lax.rsqrt(ms + EPS) * r["norm_g"][...]
    xn = (xn * (1.0 + sc[None]) + sh[None]).reshape(TILE, D).astype(BF16)

    xa = _dot(xn, r["w_in"][:, 0:LW])
    xp = jnp.concatenate([tail_in, xa], axis=0)
    cw = r["conv_w"]
    xc = r["conv_b"][...] + cw[0:1, :] * xp[0:TILE]
    for k in range(1, CONV_W):
        xc = xc + cw[k:k + 1, :] * xp[k * G:k * G + TILE]
    new_tail = xa[TILE - (CONV_W - 1) * G:]
    xcb = xc.astype(BF16)
    half = LW // 2
    r_pre = jnp.concatenate(
        [_dot(xcb[:, :half], r["wr"][0]), _dot(xcb[:, half:], r["wr"][1])], axis=1) + r["br"][...]
    i_pre = jnp.concatenate(
        [_dot(xcb[:, :half], r["wi"][0]), _dot(xcb[:, half:], r["wi"][1])], axis=1) + r["bi"][...]
    rg = jax.nn.sigmoid(r_pre)
    ig = jax.nn.sigmoid(i_pre)
    log_a = -LRU_C * rg * jax.nn.softplus(-r["lam"][...])
    a = jnp.exp(log_a)
    th = jnp.tanh(log_a)
    num = -2.0 * th
    mult = jnp.where(num > 0.0, num * lax.rsqrt(num * (1.0 - th)), 0.0)
    u = mult * (ig * xc)
    h = h_in
    for s in range(nst):
        h = a[s * G:(s + 1) * G] * h + u[s * G:(s + 1) * G]
        hbuf[s * G:(s + 1) * G, :] = h
    ya = hbuf[...] * jax.nn.gelu(_dot(xn, r["w_in"][:, LW:2 * LW]))

    ug = jax.nn.gelu(_dot(xn, r["w_in"][:, 2 * LW:2 * LW + SW]))
    vg = jax.nn.gelu(_dot(xn, r["w_in"][:, 2 * LW + SW:]))
    v = vg * lax.rsqrt(jnp.mean(vg * vg, axis=-1, keepdims=True) + EPS) * r["sgu_g"][...]
    yb = ug * sgu_fn(v)

    y = jnp.concatenate([ya, yb], axis=1)
    bd = r["bd"][...]
    outs = []
    for j in range(D // 256):
        ysl = y[:, 256 * j:256 * (j + 1)]
        sq = ysl * ysl
        hi = sq.astype(BF16)
        lo = (sq - hi.astype(F32)).astype(BF16)
        msq = (_dot(hi, bd) + _dot(lo, bd)) * (1.0 / HEAD)
        outs.append(ysl * lax.rsqrt(msq + EPS) * r["out_g"][:, 256 * j:256 * (j + 1)])
    yn = jnp.concatenate(outs, axis=1).astype(BF16)
    return yn, h, new_tail, v


_MIX_IN = ("x", "mod", "norm_g", "w_in", "conv_w", "conv_b", "wr", "wi", "br", "bi", "lam",
           "sgu_g", "wpair", "bmap", "coef", "bias8", "out_g", "bd", "h0s", "convs")
_MIX_OUT = ("y", "hlast_p", "conv_p", "hlast_s", "conv_s", "v_s")
_MIX_SCR = ("hbuf", "vbuf", "sbuf", "h_carry", "tail_carry")


def _mixer_kernel(*refs):
    names = _MIX_IN + _MIX_OUT + _MIX_SCR
    r = dict(zip(names, refs))
    i = pl.program_id(0)

    @pl.when(i == 0)
    def _():
        r["h_carry"][...] = jnp.zeros((PB, LW), F32)
        r["tail_carry"][...] = jnp.zeros(((CONV_W - 1) * PB, LW), F32)

    @pl.when(i < P_TILES)
    def _():
        sh = r["mod"][0:PB, 0:D]
        sc = r["mod"][0:PB, D:2 * D]
        sgu = functools.partial(_sgu_prompt, wpair_ref=r["wpair"], bmap_ref=r["bmap"],
                                vbuf=r["vbuf"], sbuf=r["sbuf"])
        yn, h, tail, _ = _mixer_tile(PB, r["x"][...], sh, sc, r["h_carry"][...],
                                     r["tail_carry"][...], sgu, r, r["hbuf"])
        r["y"][...] = yn
        r["h_carry"][...] = h
        r["tail_carry"][...] = tail
        r["hlast_p"][...] = h
        r["conv_p"][...] = tail

    @pl.when(i == P_TILES)
    def _():
        sh = r["mod"][PB:PB + SB, 0:D]
        sc = r["mod"][PB:PB + SB, D:2 * D]
        sgu = functools.partial(_sgu_sample, coef_ref=r["coef"], bias_ref=r["bias8"])
        yn, h, tail, v = _mixer_tile(SB, r["x"][...], sh, sc, r["h0s"][...],
                                     r["convs"][...], sgu, r, r["hbuf"])
        r["y"][...] = yn
        r["hlast_s"][...] = h
        r["conv_s"][...] = tail
        r["v_s"][...] = v


def _mixer(x_all, mod_l, w):
    nb = PB + SB
    in_specs = [
        pl.BlockSpec((TILE, D), lambda i: (i, 0)),
        _const_spec((nb, MOD_COLS)),
        _const_spec((1, D)),
        _const_spec((D, 2 * LW + 2 * SW)),
        _const_spec((CONV_W, LW)),
        _const_spec((1, LW)),
        _const_spec((2, LW // 2, LW // 2)),
        _const_spec((2, LW // 2, LW // 2)),
        _const_spec((1, LW)),
        _const_spec((1, LW)),
        _const_spec((1, LW)),
        _const_spec((1, SW)),
        _const_spec((4, CHUNK, 2 * CHUNK)),
        _const_spec((CHUNK, SW)),
        _const_spec((DEC_SEQ, DEC_SEQ, SW)),
        _const_spec((DEC_SEQ, SW)),
        _const_spec((1, D)),
        _const_spec((256, 256)),
        _const_spec((SB, LW)),
        _const_spec(((CONV_W - 1) * SB, LW)),
    ]
    out_shape = (
        jax.ShapeDtypeStruct((NT, D), BF16),
        jax.ShapeDtypeStruct((PB, LW), F32),
        jax.ShapeDtypeStruct(((CONV_W - 1) * PB, LW), F32),
        jax.ShapeDtypeStruct((SB, LW), F32),
        jax.ShapeDtypeStruct(((CONV_W - 1) * SB, LW), F32),
        jax.ShapeDtypeStruct((NS, SW), F32),
    )
    out_specs = (
        pl.BlockSpec((TILE, D), lambda i: (i, 0)),
        _const_spec((PB, LW)),
        _const_spec(((CONV_W - 1) * PB, LW)),
        _const_spec((SB, LW)),
        _const_spec(((CONV_W - 1) * SB, LW)),
        _const_spec((NS, SW)),
    )
    scratch = [
        pltpu.VMEM((TILE, LW), F32),
        pltpu.VMEM((4, TILE, 128), F32),
        pltpu.VMEM((4, TILE, 128), F32),
        pltpu.VMEM((PB, LW), F32),
        pltpu.VMEM(((CONV_W - 1) * PB, LW), F32),
    ]
    return pl.pallas_call(
        _mixer_kernel,
        grid=(N_TILES,),
        in_specs=in_specs,
        out_specs=out_specs,
        out_shape=out_shape,
        scratch_shapes=scratch,
        compiler_params=pltpu.CompilerParams(
            dimension_semantics=("arbitrary",), vmem_limit_bytes=VMEM_LIMIT),
        name="mixer",
    )(x_all, mod_l, w["norm_mix_g"], w["w_in"], w["conv_w"], w["conv_b"], w["wr"], w["wi"],
      w["br"], w["bi"], w["lam"], w["sgu_g"], w["wpair"], w["bmap"], w["coef"], w["bias8"],
      w["out_g"], w["bd"], w["h0s"], w["convs"])


def _router_tile(G, r, gm, sc, sh):
    nst = TILE // G
    out = _dot(r["y"][...], r["w_out"][...])
    xm3 = r["x"][...].reshape(nst, G, D) + gm[None] * out.reshape(nst, G, D)
    r["x_mid"][...] = xm3.reshape(TILE, D)
    ms = jnp.mean(xm3 * xm3, axis=-1, keepdims=True)
    xn3 = xm3 * lax.rsqrt(ms + EPS) * r["norm_g"][...]
    xn = (xn3 * (1.0 + sc[None]) + sh[None]).reshape(TILE, D).astype(BF16)
    r["xn"][...] = xn

    logits = lax.dot_general(r["rw_t"][...], xn, (((1,), (1,)), ((), ())),
                             preferred_element_type=F32)
    sig = jax.nn.sigmoid(logits)
    sel = sig + r["rbias"][...]
    eiota = lax.broadcasted_iota(I32, (NE, TILE), 0)
    chosen = jnp.zeros((NE, TILE), F32)
    w_rows, hots = [], []
    for _ in range(TOPK):
        m = jnp.max(sel, axis=0, keepdims=True)
        idx = jnp.min(jnp.where(sel == m, eiota, NE), axis=0, keepdims=True)
        hot = eiota == idx
        hots.append(hot)
        w_rows.append(jnp.sum(jnp.where(hot, sig, 0.0), axis=0, keepdims=True))
        chosen = jnp.where(hot, 1.0, chosen)
        sel = jnp.where(hot, -jnp.inf, sel)
    den = w_rows[0]
    for k in range(1, TOPK):
        den = den + w_rows[k]
    scale = ROUTED_SCALE / den
    r["w_t"][0] = jnp.concatenate([wk * scale for wk in w_rows], axis=0)

    cb = chosen.astype(BF16)
    rank = _dot(cb, r["tri"][...])
    cnt = _dot(cb, r["bsel"][...])
    groups = jnp.floor((cnt + (ROWS - 1.0)) * (1.0 / ROWS))
    start = ROWS * _dot(r["ltri"][...], groups.astype(BF16))
    pos = rank + jnp.concatenate(
        [jnp.broadcast_to(start[:, j:j + 1], (NE, TB)) for j in range(BPT)], axis=1)
    lp_rows = [jnp.sum(jnp.where(hot, pos, 0.0), axis=0, keepdims=True) for hot in hots]
    r["lpos_t"][0] = jnp.concatenate(lp_rows, axis=0).astype(I32)
    r["cnt"][0] = cnt


_RT_IN = ("x", "y", "mod", "w_out", "norm_g", "rw_t", "rbias", "tri", "bsel", "ltri")
_RT_OUT = ("x_mid", "xn", "lpos_t", "w_t", "cnt")


def _router_kernel(*refs):
    r = dict(zip(_RT_IN + _RT_OUT, refs))
    i = pl.program_id(0)

    def mods(lo, n):
        m = r["mod"]
        return m[lo:lo + n, 2 * D:3 * D], m[lo:lo + n, 4 * D:5 * D], m[lo:lo + n, 3 * D:4 * D]

    @pl.when(i < P_TILES)
    def _():
        _router_tile(PB, r, *mods(0, PB))

    @pl.when(i == P_TILES)
    def _():
        _router_tile(SB, r, *mods(PB, SB))


def _router(x_all, y_all, mod_l, w):
    nb = PB + SB
    in_specs = [
        pl.BlockSpec((TILE, D), lambda i: (i, 0)),
        pl.BlockSpec((TILE, D), lambda i: (i, 0)),
        _const_spec((nb, MOD_COLS)),
        _const_spec((D, D)),
        _const_spec((1, D)),
        _const_spec((NE, D)),
        _const_spec((NE, 1)),
        _const_spec((TILE, TILE)),
        _const_spec((TILE, 128)),
        _const_spec((NE, NE)),
    ]
    out_shape = (
        jax.ShapeDtypeStruct((NT, D), F32),
        jax.ShapeDtypeStruct((NT, D), BF16),
        jax.ShapeDtypeStruct((N_TILES, TOPK, TILE), I32),
        jax.ShapeDtypeStruct((N_TILES, TOPK, TILE), F32),
        jax.ShapeDtypeStruct((N_TILES, NE, 128), F32),
    )
    out_specs = (
        pl.BlockSpec((TILE, D), lambda i: (i, 0)),
        pl.BlockSpec((TILE, D), lambda i: (i, 0)),
        pl.BlockSpec((1, TOPK, TILE), lambda i: (i, 0, 0)),
        pl.BlockSpec((1, TOPK, TILE), lambda i: (i, 0, 0)),
        pl.BlockSpec((1, NE, 128), lambda i: (i, 0, 0)),
    )
    return pl.pallas_call(
        _router_kernel,
        grid=(N_TILES,),
        in_specs=in_specs,
        out_specs=out_specs,
        out_shape=out_shape,
        compiler_params=pltpu.CompilerParams(
            dimension_semantics=("arbitrary",), vmem_limit_bytes=VMEM_LIMIT),
        name="router",
    )(x_all, y_all, mod_l, w["w_out"], w["norm_ffn_g"], w["rw_t"], w["rbias"], w["tri"],
      w["bsel"], w["ltri"])


def _group_copies(tab_ref, make_copy):
    for cls, (base, nch) in enumerate(COPY_CLASSES):
        def body(i, carry, base=base, nch=nch, cls=cls):
            p = tab_ref[0, 0, base + i]
            l = pl.multiple_of((p & ((1 << LOCAL_BITS) - 1)) * ROWS, ROWS)
            g = pl.multiple_of((p >> LOCAL_BITS) * ROWS, ROWS)
            make_copy(l, g, nch * ROWS).start(priority=cls % 2)
            return carry

        lax.fori_loop(0, tab_ref[0, 0, TAB_COUNTS + cls], body, 0)


def _chunk_waits(nrows, whole_copy, chunk_copy):
    whole_copy.wait()

    def body(i, carry):
        chunk_copy.wait()
        return carry

    lax.fori_loop(0, (nrows - NCH_MIN * ROWS) // ROWS, body, 0)


def _dispatch_kernel(tab_ref, fill_ref, nrows_ref, lpos_ref, xn_ref, xs_ref, loc, zbuf, sem):
    b = pl.program_id(0)
    slot = b % 2

    def gap_chunk(e, j):
        row = pl.multiple_of(fill_ref[1 + e] + j * ROWS, ROWS)
        return pltpu.make_async_copy(
            zbuf.at[pl.ds(0, ROWS)], xs_ref.at[pl.ds(row, ROWS)], sem.at[2])

    def zero_tile(t):
        return pltpu.make_async_copy(
            zbuf, xs_ref.at[pl.ds(pl.multiple_of(t * ETILE, ETILE), ETILE)], sem.at[2])

    def start_gap(e, carry):
        lax.fori_loop(0, fill_ref[1 + NE + e], lambda j, c: (gap_chunk(e, j).start(), c)[1], 0)
        return carry

    def wait_gap(e, carry):
        lax.fori_loop(0, fill_ref[1 + NE + e], lambda j, c: (gap_chunk(e, j).wait(), c)[1], 0)
        return carry

    def start_tile(t, carry):
        zero_tile(t).start()
        return carry

    def wait_tile(t, carry):
        zero_tile(t).wait()
        return carry

    @pl.when(b == 0)
    def _():
        zbuf[...] = jnp.zeros((ETILE, D), BF16)
        lax.fori_loop(0, NE, start_gap, 0)
        lax.fori_loop(fill_ref[0], N_ETILES, start_tile, 0)

    xb = xn_ref[...]
    lp = lpos_ref[0].astype(I16)
    one = jnp.ones((), BF16)

    def sort_rows(c):
        siota = (lax.broadcasted_iota(I32, (SUB, TB), 0) + c * SUB).astype(I16)
        p = jnp.zeros((SUB, TB), BF16)
        for k in range(TOPK):
            p = jnp.where(siota == lp[k:k + 1, :], one, p)
        loc[slot, c * SUB:(c + 1) * SUB, :] = _dot(p, xb).astype(BF16)

    for c in range(LAST):
        sort_rows(c)
    tail_used = nrows_ref[b] > LAST * SUB

    @pl.when(tail_used)
    def _():
        sort_rows(LAST)

    @pl.when(jnp.logical_not(tail_used))
    def _():
        loc[slot, LAST * SUB:, :] = jnp.zeros((SUB, D), BF16)

    _group_copies(tab_ref, lambda l, g, n: pltpu.make_async_copy(
        loc.at[slot, pl.ds(l, n)], xs_ref.at[pl.ds(g, n)], sem.at[slot]))

    def drain(s, nrows):
        _chunk_waits(
            nrows,
            pltpu.make_async_copy(loc.at[s, pl.ds(0, NCH_MIN * ROWS)],
                                  xs_ref.at[pl.ds(0, NCH_MIN * ROWS)], sem.at[s]),
            pltpu.make_async_copy(loc.at[s, pl.ds(0, ROWS)], xs_ref.at[pl.ds(0, ROWS)], sem.at[s]))

    @pl.when(b > 0)
    def _():
        drain(1 - slot, nrows_ref[jnp.maximum(b - 1, 0)])

    @pl.when(b == NBLK - 1)
    def _():
        drain(slot, nrows_ref[b])
        lax.fori_loop(0, NE, wait_gap, 0)
        lax.fori_loop(fill_ref[0], N_ETILES, wait_tile, 0)


def _dispatch(tab, fill_from, nrows, lpos_d, xn):
    return pl.pallas_call(
        _dispatch_kernel,
        grid=(NBLK,),
        in_specs=[
            pl.BlockSpec((1, 1, TABW), lambda b: (b, 0, 0), memory_space=pltpu.SMEM),
            pl.BlockSpec(memory_space=pltpu.SMEM),
            pl.BlockSpec(memory_space=pltpu.SMEM),
            pl.BlockSpec((1, TOPK, TB), lambda b: (b, 0, 0)),
            pl.BlockSpec((TB, D), lambda b: (b, 0)),
        ],
        out_specs=pl.BlockSpec(memory_space=pl.ANY),
        out_shape=jax.ShapeDtypeStruct((NSLOT_MAX, D), BF16),
        scratch_shapes=[pltpu.VMEM((2, LS, D), BF16), pltpu.VMEM((ETILE, D), BF16),
                        pltpu.SemaphoreType.DMA((3,))],
        compiler_params=pltpu.CompilerParams(
            dimension_semantics=("arbitrary",), vmem_limit_bytes=VMEM_LIMIT),
        name="dispatch",
    )(tab, fill_from, nrows, lpos_d, xn)


def _expert_kernel(l, tin_ref, exp_ref, flag_ref, new_ref, wslot_ref, next_ref, xs_ref, wg_hbm,
                   wu_hbm, wd_hbm, ys_ref, wg_f, wu_f, wd_f, wg_b, wu_b, wd_b, wsem):
    v = pl.program_id(0)
    flag = flag_ref[v]

    def weight_copies(e, p):
        return (pltpu.make_async_copy(wg_hbm.at[l, e], wg_f.at[p], wsem.at[p]),
                pltpu.make_async_copy(wu_hbm.at[l, e], wu_f.at[p], wsem.at[p]),
                pltpu.make_async_copy(wd_hbm.at[l, e], wd_f.at[p], wsem.at[p]))

    @pl.when(v == 0)
    def _():
        for c in weight_copies(exp_ref[0], 0):
            c.start()

    @pl.when(new_ref[v] == 1)
    def _():
        p = wslot_ref[v]
        for c in weight_copies(exp_ref[v], p):
            c.wait()

        @pl.when(next_ref[v] >= 0)
        def _():
            for c in weight_copies(next_ref[v], 1 - p):
                c.start()

        wg_b[...] = wg_f[p].astype(BF16)
        wu_b[...] = wu_f[p].astype(BF16)
        wd_b[...] = wd_f[p].astype(BF16)

    @pl.when(flag == 1)
    def _():
        x = xs_ref[...]
        hid = jax.nn.silu(_dot(x, wg_b[...])) * _dot(x, wu_b[...])
        ys_ref[...] = _dot(hid.astype(BF16), wd_b[...]).astype(BF16)


def _experts(l, meta, xs, wg, wu, wd):
    nmeta = len(meta)
    grid_spec = pltpu.PrefetchScalarGridSpec(
        num_scalar_prefetch=nmeta,
        grid=(N_ETILES,),
        in_specs=[
            pl.BlockSpec((ETILE, D), lambda v, ti, *_: (ti[v], 0)),
            pl.BlockSpec(memory_space=pl.ANY),
            pl.BlockSpec(memory_space=pl.ANY),
            pl.BlockSpec(memory_space=pl.ANY),
        ],
        out_specs=pl.BlockSpec((ETILE, D), lambda v, ti, *_: (ti[v], 0)),
        scratch_shapes=[pltpu.VMEM((2, D, ED), F32), pltpu.VMEM((2, D, ED), F32),
                        pltpu.VMEM((2, ED, D), F32), pltpu.VMEM((D, ED), BF16),
                        pltpu.VMEM((D, ED), BF16), pltpu.VMEM((ED, D), BF16),
                        pltpu.SemaphoreType.DMA((2,))],
    )
    return pl.pallas_call(
        functools.partial(_expert_kernel, l),
        grid_spec=grid_spec,
        out_shape=jax.ShapeDtypeStruct((NSLOT_MAX, D), BF16),
        input_output_aliases={nmeta: 0},
        compiler_params=pltpu.CompilerParams(
            dimension_semantics=("arbitrary",), vmem_limit_bytes=VMEM_LIMIT),
        name="experts",
    )(*meta, xs, wg, wu, wd)


def _combine_tile(G, r, slot, gf, final, out_ref):
    nst = TB // G
    xb = r["xn"][...]
    hid = jax.nn.silu(_dot(xb, r["sg"][...])) * _dot(xb, r["su"][...])
    moe = _dot(hid.astype(BF16), r["sd"][...])
    _chunk_waits(
        r["nrows"][pl.program_id(0)],
        pltpu.make_async_copy(r["ys"].at[pl.ds(0, NCH_MIN * ROWS)],
                              r["loc"].at[slot, pl.ds(0, NCH_MIN * ROWS)], r["sem"].at[slot]),
        pltpu.make_async_copy(r["ys"].at[pl.ds(0, ROWS)], r["loc"].at[slot, pl.ds(0, ROWS)],
                              r["sem"].at[slot]))
    lp = r["lpos"][0].astype(I16)
    wts = r["wts"][0].astype(BF16)

    def unsort_rows(c):
        siota = (lax.broadcasted_iota(I32, (SUB, TB), 0) + c * SUB).astype(I16)
        p = jnp.zeros((SUB, TB), BF16)
        for k in range(TOPK):
            p = jnp.where(siota == lp[k:k + 1, :], wts[k:k + 1, :], p)
        return lax.dot_general(
            p, r["loc"][slot, c * SUB:(c + 1) * SUB, :], (((0,), (0,)), ((), ())),
            preferred_element_type=F32)

    def finish(moe):
        xo = r["x_mid"][...].reshape(nst, G, D) + gf[None] * moe.reshape(nst, G, D)
        if final:
            ms = jnp.mean(xo * xo, axis=-1, keepdims=True)
            xo = xo * lax.rsqrt(ms + EPS) * r["fin_g"][...]
        out_ref[...] = xo.reshape(TB, D)

    for c in range(LAST):
        moe = moe + unsort_rows(c)
    tail_used = r["nrows"][pl.program_id(0)] > LAST * SUB

    @pl.when(tail_used)
    def _():
        finish(moe + unsort_rows(LAST))

    @pl.when(jnp.logical_not(tail_used))
    def _():
        finish(moe)


def _combine_kernel(final, *refs):
    names = ["tab", "tab_next", "nrows", "ys", "x_mid", "xn", "lpos", "wts", "mod", "sg", "su", "sd"]
    names += ["fin_g"] if final else []
    names += ["x_out"] + (["s_out"] if final else []) + ["loc", "sem"]
    r = dict(zip(names, refs))
    b = pl.program_id(0)
    slot = b % 2

    def gather(tab_ref, s):
        r["loc"][s, NCH_MIN * ROWS:, :] = jnp.zeros((LS - NCH_MIN * ROWS, D), BF16)
        _group_copies(tab_ref, lambda l, g, n: pltpu.make_async_copy(
            r["ys"].at[pl.ds(g, n)], r["loc"].at[s, pl.ds(l, n)], r["sem"].at[s]))

    @pl.when(b == 0)
    def _():
        gather(r["tab"], 0)

    @pl.when(b + 1 < NBLK)
    def _():
        gather(r["tab_next"], 1 - slot)

    @pl.when(b < PBLK)
    def _():
        _combine_tile(PB, r, slot, r["mod"][0:PB, 5 * D:6 * D], final, r["x_out"])

    @pl.when(b >= PBLK)
    def _():
        _combine_tile(SB, r, slot, r["mod"][PB:PB + SB, 5 * D:6 * D], final,
                      r["s_out"] if final else r["x_out"])


def _combine(tab, nrows, ys, x_mid, xn, lpos_c, wts, mod_l, w, fin_g):
    final = fin_g is not None
    nb = PB + SB
    in_specs = [
        pl.BlockSpec((1, 1, TABW), lambda b: (b, 0, 0), memory_space=pltpu.SMEM),
        pl.BlockSpec((1, 1, TABW), lambda b: (jnp.minimum(b + 1, NBLK - 1), 0, 0),
                     memory_space=pltpu.SMEM),
        pl.BlockSpec(memory_space=pltpu.SMEM),
        pl.BlockSpec(memory_space=pl.ANY),
        pl.BlockSpec((TB, D), lambda b: (b, 0)),
        pl.BlockSpec((TB, D), lambda b: (b, 0)),
        pl.BlockSpec((1, TOPK, TB), lambda b: (b, 0, 0)),
        pl.BlockSpec((1, TOPK, TB), lambda b: (b, 0, 0)),
        _const_spec((nb, MOD_COLS)),
        _const_spec((D, ED)),
        _const_spec((D, ED)),
        _const_spec((ED, D)),
    ]
    args = [tab, tab, nrows, ys, x_mid, xn, lpos_c, wts, mod_l, w["sg"], w["su"], w["sd"]]
    out_specs = pl.BlockSpec((TB, D), lambda b: (b, 0))
    out_shape = jax.ShapeDtypeStruct((NT, D), F32)
    if final:
        in_specs.append(_const_spec((1, D)))
        args.append(fin_g)
        out_specs = (pl.BlockSpec((TB, D), lambda b: (jnp.minimum(b, PBLK - 1), 0)),
                     pl.BlockSpec((TB, D), lambda b: (jnp.maximum(b - PBLK, 0), 0)))
        out_shape = (jax.ShapeDtypeStruct((NP, D), F32), jax.ShapeDtypeStruct((NS, D), F32))
    return pl.pallas_call(
        functools.partial(_combine_kernel, final),
        grid=(NBLK,),
        in_specs=in_specs,
        out_specs=out_specs,
        out_shape=out_shape,
        scratch_shapes=[pltpu.VMEM((2, LS, D), BF16), pltpu.SemaphoreType.DMA((2,))],
        compiler_params=pltpu.CompilerParams(
            dimension_semantics=("arbitrary",), vmem_limit_bytes=VMEM_LIMIT),
        name="combine_final" if final else "combine",
    )(*args)


def _plan(cnt):
    c = cnt[:, :, :BPT].transpose(0, 2, 1).reshape(NBLK, NE).astype(I32)
    pc = (c + ROWS - 1) // ROWS * ROWS
    lend = jnp.cumsum(pc, axis=1)
    lstart = lend - pc
    etot = jnp.sum(pc, axis=0)
    epad = (etot + ETILE - 1) // ETILE * ETILE
    eend = jnp.cumsum(epad)
    eoff = eend - epad
    gstart = eoff[None, :] + jnp.cumsum(pc, axis=0) - pc
    eids = jnp.arange(NE, dtype=I32)

    def compact(mask, vals, width):
        pos = jnp.cumsum(mask.astype(I32), axis=1) - 1
        sel = mask[:, None, :] & (pos[:, None, :] == jnp.arange(width, dtype=I32)[None, :, None])
        return jnp.sum(jnp.where(sel, vals[:, None, :], 0), axis=-1), jnp.sum(mask.astype(I32), axis=1)

    nchunk = pc // ROWS
    packed = (gstart // ROWS) * (1 << LOCAL_BITS) + lstart // ROWS
    two, n_two = compact(nchunk == 2, packed, NE)
    three, n_three = compact(nchunk == 3, packed, NE)
    lj = jnp.arange(NCH, dtype=I32)
    l8 = lj * ROWS
    ej = jnp.minimum(jnp.sum((lend[:, None, :] <= l8[None, :, None]).astype(I32), axis=-1), NE - 1)
    hot = ej[..., None] == eids
    row = jnp.sum(jnp.where(hot, (gstart - lstart)[:, None, :], 0), axis=-1) + l8[None, :]
    single = (nchunk != 2) & (nchunk != 3)
    is_one = (l8[None, :] < lend[:, -1:]) & jnp.any(hot & single[:, None, :], axis=-1)
    one, n_one = compact(is_one, (row // ROWS) * (1 << LOCAL_BITS) + lj[None, :], NCH)
    counts = jnp.stack([n_two, n_three, n_one] + [jnp.zeros_like(n_two)] * (TABW - TAB_COUNTS - 3), axis=1)
    tab = jnp.concatenate([two, three, one, counts], axis=1).astype(I32).reshape(NBLK, 1, TABW)

    def lookup(table, idx):
        return jnp.sum(jnp.where(idx[..., None] == eids, table, 0), axis=-1)

    used = eend[-1] // ETILE
    v = jnp.arange(N_ETILES, dtype=I32)
    tin = jnp.minimum(v, used - 1)
    e = jnp.minimum(jnp.sum((eend[None, :] <= (tin * ETILE)[:, None]).astype(I32), axis=1), NE - 1)
    flag = (v < used).astype(I32)
    new = jnp.concatenate([jnp.ones((1,), I32), (e[1:] != e[:-1]).astype(I32)])
    live = etot > 0
    wslot = lookup((jnp.cumsum(live.astype(I32)) - 1) % 2, e)
    later = live[None, :] & (eids[None, :] > eids[:, None])
    nxt = jnp.min(jnp.where(later, eids[None, :], NE), axis=1)
    nxt = lookup(jnp.where(nxt < NE, nxt, -1), e)
    meta = (tin, e, flag, new, wslot.astype(I32), nxt.astype(I32))
    fill = jnp.concatenate([used[None], eoff + etot, (epad - etot) // ROWS]).astype(I32)
    return tab, fill, lend[:, -1], meta


def _block_diag(wh):
    out = jnp.zeros((2, 4, HEAD, 4, HEAD), wh.dtype)
    for hh in range(4):
        out = out.at[:, hh, :, hh, :].set(wh.reshape(2, 4, HEAD, HEAD)[:, hh])
    return out.reshape(2, 4 * HEAD, 4 * HEAD)


def _layer_weights(l, p):
    tril = jnp.tril(jnp.ones((CHUNK, CHUNK), bool))
    wt = jnp.where(tril, p["sgu_w"][l], 0.0)
    wpair = wt.reshape(4, 2, CHUNK, CHUNK).transpose(0, 2, 1, 3).reshape(4, CHUNK, 2 * CHUNK)
    bmap = jnp.repeat(p["sgu_b"][l].T, HEAD, axis=1)
    w8 = jnp.where(tril[:DEC_SEQ, :DEC_SEQ], p["sgu_w"][l][:, :DEC_SEQ, :DEC_SEQ], 0.0)
    coef = jnp.repeat(w8.transpose(2, 1, 0), HEAD, axis=2)
    bias8 = jnp.repeat(p["sgu_b"][l][:, :DEC_SEQ].T, HEAD, axis=1)
    gid = jnp.arange(256) // HEAD
    tok = jnp.arange(TILE)
    same_blk = (tok[:, None] // TB) == (tok[None, :] // TB)
    return dict(
        norm_mix_g=p["norm_mix_g"][l][None], norm_ffn_g=p["norm_ffn_g"][l][None],
        w_in=p["w_in"][l].astype(BF16), w_out=p["w_out"][l].astype(BF16),
        conv_w=p["conv_w"][l], conv_b=p["conv_b"][l][None],
        wr=_block_diag(p["gate_r_w"][l]).astype(BF16), wi=_block_diag(p["gate_i_w"][l]).astype(BF16),
        br=p["gate_r_b"][l][None], bi=p["gate_i_b"][l][None], lam=p["lru_lambda"][l][None],
        sgu_g=p["sgu_norm_g"][l][None], wpair=wpair.astype(BF16), bmap=bmap, coef=coef, bias8=bias8,
        out_g=p["out_norm_g"][l][None],
        bd=(gid[:, None] == gid[None, :]).astype(BF16),
        rw_t=p["router_w"][l].T.astype(BF16), rbias=p["router_bias"][l][:, None],
        tri=(same_blk & (tok[:, None] < tok[None, :])).astype(BF16),
        bsel=((tok[:, None] // TB) == jnp.arange(128)[None, :]).astype(BF16),
        ltri=(jnp.arange(NE)[:, None] > jnp.arange(NE)[None, :]).astype(BF16),
        sg=p["shared_w_gate"][l].astype(BF16), su=p["shared_w_up"][l].astype(BF16),
        sd=p["shared_w_down"][l].astype(BF16),
    )


def kernel(x_prompt, x_sample, c_prompt, c_sample, state_lru_h, state_conv, ada_w, ada_b, norm_mix_g, norm_ffn_g, w_in, conv_w, conv_b, gate_r_w, gate_r_b, gate_i_w, gate_i_b, lru_lambda, sgu_norm_g, sgu_w, sgu_b, out_norm_g, w_out, router_w, router_bias, exp_w_gate, exp_w_up, exp_w_down, shared_w_gate, shared_w_up, shared_w_down, final_norm_g):
    p = dict(norm_mix_g=norm_mix_g, norm_ffn_g=norm_ffn_g, w_in=w_in, conv_w=conv_w, conv_b=conv_b,
             gate_r_w=gate_r_w, gate_r_b=gate_r_b, gate_i_w=gate_i_w, gate_i_b=gate_i_b,
             lru_lambda=lru_lambda, sgu_norm_g=sgu_norm_g, sgu_w=sgu_w, sgu_b=sgu_b,
             out_norm_g=out_norm_g, w_out=w_out, router_w=router_w, router_bias=router_bias,
             shared_w_gate=shared_w_gate, shared_w_up=shared_w_up, shared_w_down=shared_w_down)
    x_all = jnp.concatenate([x_prompt.transpose(1, 0, 2).reshape(NP, D),
                             x_sample.transpose(1, 0, 2).reshape(NS, D)], axis=0)
    mod = _modulations(jnp.concatenate([c_prompt, c_sample], axis=0), ada_w, ada_b)

    hp, cp, hs, cs, vs = [], [], [], [], []
    for l in range(DEPTH):
        w = _layer_weights(l, p)
        w["h0s"] = state_lru_h[l]
        w["convs"] = state_conv[l].transpose(1, 0, 2).reshape((CONV_W - 1) * SB, LW)
        yn, hlp, cvp, hls, cvs, v_s = _mixer(x_all, mod[l], w)
        x_mid, xn, lpos_t, w_t, cnt = _router(x_all, yn, mod[l], w)
        tab, used, nrows, meta = _plan(cnt)
        def per_block(a):
            return a.reshape(N_TILES, TOPK, BPT, TB).transpose(0, 2, 1, 3).reshape(NBLK, TOPK, TB)

        lpos_d = per_block(lpos_t)
        xs = _dispatch(tab, used, nrows, lpos_d, xn)
        ys = _experts(l, meta, xs, exp_w_gate, exp_w_up, exp_w_down)
        fin = final_norm_g[None] if l == DEPTH - 1 else None
        x_all = _combine(tab, nrows, ys, x_mid, xn, lpos_d, per_block(w_t), mod[l], w, fin)
        hp.append(hlp)
        cp.append(cvp.reshape(CONV_W - 1, PB, LW).transpose(1, 0, 2))
        hs.append(hls)
        cs.append(cvs.reshape(CONV_W - 1, SB, LW).transpose(1, 0, 2))
        vs.append(v_s.reshape(DEC_SEQ, SB, SW).transpose(1, 0, 2))

    y_prompt = x_all[0].reshape(SEQ, PB, D).transpose(1, 0, 2)
    y_sample = x_all[1].reshape(DEC_SEQ, SB, D).transpose(1, 0, 2)
    return (y_prompt, y_sample, jnp.stack(hp), jnp.stack(cp), jnp.stack(hs), jnp.stack(cs),
            jnp.stack(vs))
```

```python
import functools

import jax
import jax.numpy as jnp
from jax import lax
from jax.experimental import pallas as pl
from jax.experimental.pallas import tpu as pltpu

F32 = jnp.float32
BF16 = jnp.bfloat16
I32 = jnp.int32
I16 = jnp.int16

D = 1024
DEPTH = 4
PB = 8
SEQ = 2048
SB = 128
DEC_SEQ = 8
NP = PB * SEQ
NS = SB * DEC_SEQ
NT = NP + NS
TILE = 1024
N_TILES = NT // TILE
P_TILES = NP // TILE
LW = 512
SW = 512
HEAD = 64
CHUNK = 128
CONV_W = 4
LRU_C = 8.0
NE = 64
TOPK = 8
ED = 256
ROUTED_SCALE = 2.5
EPS = 1e-6
MOD_COLS = 6 * D
MOD_BLK = 1536
VMEM_LIMIT = 56 * 1024 * 1024

TB = 256
NBLK = NT // TB
PBLK = NP // TB
BPT = TILE // TB
ROWS = 16
LS = 3072
NCH = LS // ROWS
NCH_MIN = TB * TOPK // ROWS
SUB = 512
LAST = LS // SUB - 1
ETILE = 1024
NSLOT_MAX = -(-(NT * TOPK + NBLK * NE * (ROWS - 1) + NE * (ETILE - ROWS)) // ETILE) * ETILE
N_ETILES = NSLOT_MAX // ETILE
XRING = 3
TAB_TWO, TAB_THREE, TAB_ONE = 0, NE, 2 * NE
TAB_COUNTS = 2 * NE + NCH
TABW = TAB_COUNTS + 8
COPY_CLASSES = ((TAB_TWO, 2), (TAB_THREE, 3), (TAB_ONE, 1))
LOCAL_BITS = 8

assert NSLOT_MAX % ETILE == 0 and LS >= TB * TOPK + NE * (ROWS - 1) and LS % SUB == 0
assert NCH <= 1 << LOCAL_BITS and NSLOT_MAX // ROWS < 1 << (31 - LOCAL_BITS)


def _dot(a, b):
    return jnp.dot(a, b, preferred_element_type=F32)


def _const_spec(shape):
    nd = len(shape)
    return pl.BlockSpec(shape, lambda *_: (0,) * nd)


def _mod_kernel(c_ref, w_ref, b_ref, o_ref):
    s = jax.nn.silu(c_ref[...]).astype(BF16)
    o_ref[0] = _dot(s, w_ref[0].astype(BF16)) + b_ref[0]


def _modulations(c_all, ada_w, ada_b):
    nb = c_all.shape[0]
    return pl.pallas_call(
        _mod_kernel,
        grid=(DEPTH, MOD_COLS // MOD_BLK),
        in_specs=[
            pl.BlockSpec((nb, D), lambda l, j: (0, 0)),
            pl.BlockSpec((1, D, MOD_BLK), lambda l, j: (l, 0, j)),
            pl.BlockSpec((1, 1, MOD_BLK), lambda l, j: (l, 0, j)),
        ],
        out_specs=pl.BlockSpec((1, nb, MOD_BLK), lambda l, j: (l, 0, j)),
        out_shape=jax.ShapeDtypeStruct((DEPTH, nb, MOD_COLS), F32),
        compiler_params=pltpu.CompilerParams(
            dimension_semantics=("arbitrary", "arbitrary"), vmem_limit_bytes=VMEM_LIMIT),
        name="adaln_mod",
    )(c_all, ada_w, ada_b.reshape(DEPTH, 1, MOD_COLS))


def _sgu_prompt(v, wpair_ref, bmap_ref, vbuf, sbuf):
    for j in range(4):
        vbuf[j] = v[:, 128 * j:128 * (j + 1)]
    lane = lax.broadcasted_iota(I32, (CHUNK, 128), 1)
    left = lane < HEAD
    for b in range(PB):
        for j in range(4):
            vp = vbuf[j, pl.ds(b, CHUNK, stride=PB), :]
            rhs = jnp.concatenate(
                [jnp.where(left, vp, 0.0), jnp.where(left, 0.0, vp)], axis=0).astype(BF16)
            sj = _dot(wpair_ref[j], rhs) + bmap_ref[:, 128 * j:128 * (j + 1)]
            sbuf[j, pl.ds(b, CHUNK, stride=PB), :] = sj
    return jnp.concatenate([sbuf[j] for j in range(4)], axis=1)


def _sgu_sample(v, coef_ref, bias_ref):
    vt = [v[t * SB:(t + 1) * SB] for t in range(DEC_SEQ)]
    rows = []
    for p in range(DEC_SEQ):
        acc = bias_ref[p:p + 1, :] + coef_ref[0, p:p + 1, :] * vt[0]
        for q in range(1, p + 1):
            acc = acc + coef_ref[q, p:p + 1, :] * vt[q]
        rows.append(acc)
    return jnp.concatenate(rows, axis=0)


def _mixer_tile(G, x, sh, sc, h_in, tail_in, sgu_fn, r, hbuf):
    nst = TILE // G
    x3 = x.reshape(nst, G, D)
    ms = jnp.mean(x3 * x3, axis=-1, keepdims=True)
    xn = x3 * lax.rsqrt(ms + EPS) * r["norm_g"][...]
    xn = (xn * (1.0 + sc[None]) + sh[None]).reshape(TILE, D).astype(BF16)

    xa = _dot(xn, r["w_in"][:, 0:LW])
    xp = jnp.concatenate([tail_in, xa], axis=0)
    cw = r["conv_w"]
    xc = r["conv_b"][...] + cw[0:1, :] * xp[0:TILE]
    for k in range(1, CONV_W):
        xc = xc + cw[k:k + 1, :] * xp[k * G:k * G + TILE]
    new_tail = xa[TILE - (CONV_W - 1) * G:]
    xcb = xc.astype(BF16)
    half = LW // 2
    r_pre = jnp.concatenate(
        [_dot(xcb[:, :half], r["wr"][0]), _dot(xcb[:, half:], r["wr"][1])], axis=1) + r["br"][...]
    i_pre = jnp.concatenate(
        [_dot(xcb[:, :half], r["wi"][0]), _dot(xcb[:, half:], r["wi"][1])], axis=1) + r["bi"][...]
    rg = jax.nn.sigmoid(r_pre)
    ig = jax.nn.sigmoid(i_pre)
    log_a = -LRU_C * rg * jax.nn.softplus(-r["lam"][...])
    a = jnp.exp(log_a)
    th = jnp.tanh(log_a)
    num = -2.0 * th
    mult = jnp.where(num > 0.0, num * lax.rsqrt(num * (1.0 - th)), 0.0)
    u = mult * (ig * xc)
    h = h_in
    for s in range(nst):
        h = a[s * G:(s + 1) * G] * h + u[s * G:(s + 1) * G]
        hbuf[s * G:(s + 1) * G, :] = h
    ya = hbuf[...] * jax.nn.gelu(_dot(xn, r["w_in"][:, LW:2 * LW]))

    ug = jax.nn.gelu(_dot(xn, r["w_in"][:, 2 * LW:2 * LW + SW]))
    vg = jax.nn.gelu(_dot(xn, r["w_in"][:, 2 * LW + SW:]))
    v = vg * lax.rsqrt(jnp.mean(vg * vg, axis=-1, keepdims=True) + EPS) * r["sgu_g"][...]
    yb = ug * sgu_fn(v)

    y = jnp.concatenate([ya, yb], axis=1)
    bd = r["bd"][...]
    outs = []
    for j in range(D // 256):
        ysl = y[:, 256 * j:256 * (j + 1)]
        sq = ysl * ysl
        hi = sq.astype(BF16)
        lo = (sq - hi.astype(F32)).astype(BF16)
        msq = (_dot(hi, bd) + _dot(lo, bd)) * (1.0 / HEAD)
        outs.append(ysl * lax.rsqrt(msq + EPS) * r["out_g"][:, 256 * j:256 * (j + 1)])
    yn = jnp.concatenate(outs, axis=1).astype(BF16)
    return yn, h, new_tail, v


_MIX_IN = ("x", "mod", "norm_g", "w_in", "conv_w", "conv_b", "wr", "wi", "br", "bi", "lam",
           "sgu_g", "wpair", "bmap", "coef", "bias8", "out_g", "bd", "h0s", "convs")
_MIX_OUT = ("y", "hlast_p", "conv_p", "hlast_s", "conv_s", "v_s")
_MIX_SCR = ("hbuf", "vbuf", "sbuf", "h_carry", "tail_carry")


def _mixer_kernel(*refs):
    names = _MIX_IN + _MIX_OUT + _MIX_SCR
    r = dict(zip(names, refs))
    i = pl.program_id(0)

    @pl.when(i == 0)
    def _():
        r["h_carry"][...] = jnp.zeros((PB, LW), F32)
        r["tail_carry"][...] = jnp.zeros(((CONV_W - 1) * PB, LW), F32)

    @pl.when(i < P_TILES)
    def _():
        sh = r["mod"][0:PB, 0:D]
        sc = r["mod"][0:PB, D:2 * D]
        sgu = functools.partial(_sgu_prompt, wpair_ref=r["wpair"], bmap_ref=r["bmap"],
                                vbuf=r["vbuf"], sbuf=r["sbuf"])
        yn, h, tail, _ = _mixer_tile(PB, r["x"][...], sh, sc, r["h_carry"][...],
                                     r["tail_carry"][...], sgu, r, r["hbuf"])
        r["y"][...] = yn
        r["h_carry"][...] = h
        r["tail_carry"][...] = tail
        r["hlast_p"][...] = h
        r["conv_p"][...] = tail

    @pl.when(i == P_TILES)
    def _():
        sh = r["mod"][PB:PB + SB, 0:D]
        sc = r["mod"][PB:PB + SB, D:2 * D]
        sgu = functools.partial(_sgu_sample, coef_ref=r["coef"], bias_ref=r["bias8"])
        yn, h, tail, v = _mixer_tile(SB, r["x"][...], sh, sc, r["h0s"][...],
                                     r["convs"][...], sgu, r, r["hbuf"])
        r["y"][...] = yn
        r["hlast_s"][...] = h
        r["conv_s"][...] = tail
        r["v_s"][...] = v


def _mixer(x_all, mod_l, w):
    nb = PB + SB
    in_specs = [
        pl.BlockSpec((TILE, D), lambda i: (i, 0)),
        _const_spec((nb, MOD_COLS)),
        _const_spec((1, D)),
        _const_spec((D, 2 * LW + 2 * SW)),
        _const_spec((CONV_W, LW)),
        _const_spec((1, LW)),
        _const_spec((2, LW // 2, LW // 2)),
        _const_spec((2, LW // 2, LW // 2)),
        _const_spec((1, LW)),
        _const_spec((1, LW)),
        _const_spec((1, LW)),
        _const_spec((1, SW)),
        _const_spec((4, CHUNK, 2 * CHUNK)),
        _const_spec((CHUNK, SW)),
        _const_spec((DEC_SEQ, DEC_SEQ, SW)),
        _const_spec((DEC_SEQ, SW)),
        _const_spec((1, D)),
        _const_spec((256, 256)),
        _const_spec((SB, LW)),
        _const_spec(((CONV_W - 1) * SB, LW)),
    ]
    out_shape = (
        jax.ShapeDtypeStruct((NT, D), BF16),
        jax.ShapeDtypeStruct((PB, LW), F32),
        jax.ShapeDtypeStruct(((CONV_W - 1) * PB, LW), F32),
        jax.ShapeDtypeStruct((SB, LW), F32),
        jax.ShapeDtypeStruct(((CONV_W - 1) * SB, LW), F32),
        jax.ShapeDtypeStruct((NS, SW), F32),
    )
    out_specs = (
        pl.BlockSpec((TILE, D), lambda i: (i, 0)),
        _const_spec((PB, LW)),
        _const_spec(((CONV_W - 1) * PB, LW)),
        _const_spec((SB, LW)),
        _const_spec(((CONV_W - 1) * SB, LW)),
        _const_spec((NS, SW)),
    )
    scratch = [
        pltpu.VMEM((TILE, LW), F32),
        pltpu.VMEM((4, TILE, 128), F32),
        pltpu.VMEM((4, TILE, 128), F32),
        pltpu.VMEM((PB, LW), F32),
        pltpu.VMEM(((CONV_W - 1) * PB, LW), F32),
    ]
    return pl.pallas_call(
        _mixer_kernel,
        grid=(N_TILES,),
        in_specs=in_specs,
        out_specs=out_specs,
        out_shape=out_shape,
        scratch_shapes=scratch,
        compiler_params=pltpu.CompilerParams(
            dimension_semantics=("arbitrary",), vmem_limit_bytes=VMEM_LIMIT),
        name="mixer",
    )(x_all, mod_l, w["norm_mix_g"], w["w_in"], w["conv_w"], w["conv_b"], w["wr"], w["wi"],
      w["br"], w["bi"], w["lam"], w["sgu_g"], w["wpair"], w["bmap"], w["coef"], w["bias8"],
      w["out_g"], w["bd"], w["h0s"], w["convs"])


def _router_tile(G, r, gm, sc, sh):
    nst = TILE // G
    out = _dot(r["y"][...], r["w_out"][...])
    xm3 = r["x"][...].reshape(nst, G, D) + gm[None] * out.reshape(nst, G, D)
    r["x_mid"][...] = xm3.reshape(TILE, D)
    ms = jnp.mean(xm3 * xm3, axis=-1, keepdims=True)
    xn3 = xm3 * lax.rsqrt(ms + EPS) * r["norm_g"][...]
    xn = (xn3 * (1.0 + sc[None]) + sh[None]).reshape(TILE, D).astype(BF16)
    r["xn"][...] = xn

    logits = lax.dot_general(r["rw_t"][...], xn, (((1,), (1,)), ((), ())),
                             preferred_element_type=F32)
    sig = jax.nn.sigmoid(logits)
    sel = sig + r["rbias"][...]
    eiota = lax.broadcasted_iota(I32, (NE, TILE), 0)
    chosen = jnp.zeros((NE, TILE), F32)
    w_rows, hots = [], []
    for _ in range(TOPK):
        m = jnp.max(sel, axis=0, keepdims=True)
        idx = jnp.min(jnp.where(sel == m, eiota, NE), axis=0, keepdims=True)
        hot = eiota == idx
        hots.append(hot)
        w_rows.append(jnp.sum(jnp.where(hot, sig, 0.0), axis=0, keepdims=True))
        chosen = jnp.where(hot, 1.0, chosen)
        sel = jnp.where(hot, -jnp.inf, sel)
    den = w_rows[0]
    for k in range(1, TOPK):
        den = den + w_rows[k]
    scale = ROUTED_SCALE / den
    r["w_t"][0] = jnp.concatenate([wk * scale for wk in w_rows], axis=0)

    cb = chosen.astype(BF16)
    rank = _dot(cb, r["tri"][...])
    cnt = _dot(cb, r["bsel"][...])
    groups = jnp.floor((cnt + (ROWS - 1.0)) * (1.0 / ROWS))
    start = ROWS * _dot(r["ltri"][...], groups.astype(BF16))
    pos = rank + jnp.concatenate(
        [jnp.broadcast_to(start[:, j:j + 1], (NE, TB)) for j in range(BPT)], axis=1)
    lp_rows = [jnp.sum(jnp.where(hot, pos, 0.0), axis=0, keepdims=True) for hot in hots]
    r["lpos_t"][0] = jnp.concatenate(lp_rows, axis=0).astype(I32)
    r["cnt"][0] = cnt


_RT_IN = ("x", "y", "mod", "w_out", "norm_g", "rw_t", "rbias", "tri", "bsel", "ltri")
_RT_OUT = ("x_mid", "xn", "lpos_t", "w_t", "cnt")


def _router_kernel(*refs):
    r = dict(zip(_RT_IN + _RT_OUT, refs))
    i = pl.program_id(0)

    def mods(lo, n):
        m = r["mod"]
        return m[lo:lo + n, 2 * D:3 * D], m[lo:lo + n, 4 * D:5 * D], m[lo:lo + n, 3 * D:4 * D]

    @pl.when(i < P_TILES)
    def _():
        _router_tile(PB, r, *mods(0, PB))

    @pl.when(i == P_TILES)
    def _():
        _router_tile(SB, r, *mods(PB, SB))


def _router(x_all, y_all, mod_l, w):
    nb = PB + SB
    in_specs = [
        pl.BlockSpec((TILE, D), lambda i: (i, 0)),
        pl.BlockSpec((TILE, D), lambda i: (i, 0)),
        _const_spec((nb, MOD_COLS)),
        _const_spec((D, D)),
        _const_spec((1, D)),
        _const_spec((NE, D)),
        _const_spec((NE, 1)),
        _const_spec((TILE, TILE)),
        _const_spec((TILE, 128)),
        _const_spec((NE, NE)),
    ]
    out_shape = (
        jax.ShapeDtypeStruct((NT, D), F32),
        jax.ShapeDtypeStruct((NT, D), BF16),
        jax.ShapeDtypeStruct((N_TILES, TOPK, TILE), I32),
        jax.ShapeDtypeStruct((N_TILES, TOPK, TILE), F32),
        jax.ShapeDtypeStruct((N_TILES, NE, 128), F32),
    )
    out_specs = (
        pl.BlockSpec((TILE, D), lambda i: (i, 0)),
        pl.BlockSpec((TILE, D), lambda i: (i, 0)),
        pl.BlockSpec((1, TOPK, TILE), lambda i: (i, 0, 0)),
        pl.BlockSpec((1, TOPK, TILE), lambda i: (i, 0, 0)),
        pl.BlockSpec((1, NE, 128), lambda i: (i, 0, 0)),
    )
    return pl.pallas_call(
        _router_kernel,
        grid=(N_TILES,),
        in_specs=in_specs,
        out_specs=out_specs,
        out_shape=out_shape,
        compiler_params=pltpu.CompilerParams(
            dimension_semantics=("arbitrary",), vmem_limit_bytes=VMEM_LIMIT),
        name="router",
    )(x_all, y_all, mod_l, w["w_out"], w["norm_ffn_g"], w["rw_t"], w["rbias"], w["tri"],
      w["bsel"], w["ltri"])


def _group_copies(tab_ref, make_copy):
    for cls, (base, nch) in enumerate(COPY_CLASSES):
        def body(i, carry, base=base, nch=nch, cls=cls):
            p = tab_ref[0, 0, base + i]
            l = pl.multiple_of((p & ((1 << LOCAL_BITS) - 1)) * ROWS, ROWS)
            g = pl.multiple_of((p >> LOCAL_BITS) * ROWS, ROWS)
            make_copy(l, g, nch * ROWS).start(priority=cls % 2)
            return carry

        lax.fori_loop(0, tab_ref[0, 0, TAB_COUNTS + cls], body, 0)


def _chunk_waits(nrows, whole_copy, chunk_copy):
    whole_copy.wait()

    def body(i, carry):
        chunk_copy.wait()
        return carry

    lax.fori_loop(0, (nrows - NCH_MIN * ROWS) // ROWS, body, 0)


def _dispatch_kernel(tab_ref, fill_ref, nrows_ref, lpos_ref, xn_ref, xs_ref, loc, zbuf, sem):
    b = pl.program_id(0)
    slot = b % 2

    def gap_chunk(e, j):
        row = pl.multiple_of(fill_ref[1 + e] + j * ROWS, ROWS)
        return pltpu.make_async_copy(
            zbuf.at[pl.ds(0, ROWS)], xs_ref.at[pl.ds(row, ROWS)], sem.at[2])

    def zero_tile(t):
        return pltpu.make_async_copy(
            zbuf, xs_ref.at[pl.ds(pl.multiple_of(t * ETILE, ETILE), ETILE)], sem.at[2])

    def start_gap(e, carry):
        lax.fori_loop(0, fill_ref[1 + NE + e], lambda j, c: (gap_chunk(e, j).start(), c)[1], 0)
        return carry

    def wait_gap(e, carry):
        lax.fori_loop(0, fill_ref[1 + NE + e], lambda j, c: (gap_chunk(e, j).wait(), c)[1], 0)
        return carry

    def start_tile(t, carry):
        zero_tile(t).start()
        return carry

    def wait_tile(t, carry):
        zero_tile(t).wait()
        return carry

    @pl.when(b == 0)
    def _():
        zbuf[...] = jnp.zeros((ETILE, D), BF16)
        lax.fori_loop(0, NE, start_gap, 0)
        lax.fori_loop(fill_ref[0], N_ETILES, start_tile, 0)

    xb = xn_ref[...]
    lp = lpos_ref[0].astype(I16)
    one = jnp.ones((), BF16)

    def sort_rows(c):
        siota = (lax.broadcasted_iota(I32, (SUB, TB), 0) + c * SUB).astype(I16)
        p = jnp.zeros((SUB, TB), BF16)
        for k in range(TOPK):
            p = jnp.where(siota == lp[k:k + 1, :], one, p)
        loc[slot, c * SUB:(c + 1) * SUB, :] = _dot(p, xb).astype(BF16)

    for c in range(LAST):
        sort_rows(c)
    tail_used = nrows_ref[b] > LAST * SUB

    @pl.when(tail_used)
    def _():
        sort_rows(LAST)

    @pl.when(jnp.logical_not(tail_used))
    def _():
        loc[slot, LAST * SUB:, :] = jnp.zeros((SUB, D), BF16)

    _group_copies(tab_ref, lambda l, g, n: pltpu.make_async_copy(
        loc.at[slot, pl.ds(l, n)], xs_ref.at[pl.ds(g, n)], sem.at[slot]))

    def drain(s, nrows):
        _chunk_waits(
            nrows,
            pltpu.make_async_copy(loc.at[s, pl.ds(0, NCH_MIN * ROWS)],
                                  xs_ref.at[pl.ds(0, NCH_MIN * ROWS)], sem.at[s]),
            pltpu.make_async_copy(loc.at[s, pl.ds(0, ROWS)], xs_ref.at[pl.ds(0, ROWS)], sem.at[s]))

    @pl.when(b > 0)
    def _():
        drain(1 - slot, nrows_ref[jnp.maximum(b - 1, 0)])

    @pl.when(b == NBLK - 1)
    def _():
        drain(slot, nrows_ref[b])
        lax.fori_loop(0, NE, wait_gap, 0)
        lax.fori_loop(fill_ref[0], N_ETILES, wait_tile, 0)


def _dispatch(tab, fill_from, nrows, lpos_d, xn):
    return pl.pallas_call(
        _dispatch_kernel,
        grid=(NBLK,),
        in_specs=[
            pl.BlockSpec((1, 1, TABW), lambda b: (b, 0, 0), memory_space=pltpu.SMEM),
            pl.BlockSpec(memory_space=pltpu.SMEM),
            pl.BlockSpec(memory_space=pltpu.SMEM),
            pl.BlockSpec((1, TOPK, TB), lambda b: (b, 0, 0)),
            pl.BlockSpec((TB, D), lambda b: (b, 0)),
        ],
        out_specs=pl.BlockSpec(memory_space=pl.ANY),
        out_shape=jax.ShapeDtypeStruct((NSLOT_MAX, D), BF16),
        scratch_shapes=[pltpu.VMEM((2, LS, D), BF16), pltpu.VMEM((ETILE, D), BF16),
                        pltpu.SemaphoreType.DMA((3,))],
        compiler_params=pltpu.CompilerParams(
            dimension_semantics=("arbitrary",), vmem_limit_bytes=VMEM_LIMIT),
        name="dispatch",
    )(tab, fill_from, nrows, lpos_d, xn)


def _expert_kernel(l, tin_ref, exp_ref, flag_ref, new_ref, wslot_ref, next_ref, xs_hbm, wg_hbm,
                   wu_hbm, wd_hbm, ys_ref, xbuf, xsem, wg_f, wu_f, wd_f, wg_b, wu_b, wd_b, wsem):
    v = pl.program_id(0)
    flag = flag_ref[v]

    def tile_copy(s, k):
        row = pl.multiple_of(tin_ref[s] * ETILE, ETILE)
        return pltpu.make_async_copy(xs_hbm.at[pl.ds(row, ETILE)], xbuf.at[k], xsem.at[k])

    @pl.when(v == 0)
    def _():
        for s in range(XRING - 1):
            @pl.when(flag_ref[s] == 1)
            def _():
                tile_copy(s, s).start()

    ahead = jnp.minimum(v + XRING - 1, N_ETILES - 1)

    @pl.when((v + XRING - 1 < N_ETILES) & (flag_ref[ahead] == 1))
    def _():
        tile_copy(ahead, lax.rem(ahead, XRING)).start()

    def weight_copies(e, p):
        return (pltpu.make_async_copy(wg_hbm.at[l, e], wg_f.at[p], wsem.at[p]),
                pltpu.make_async_copy(wu_hbm.at[l, e], wu_f.at[p], wsem.at[p]),
                pltpu.make_async_copy(wd_hbm.at[l, e], wd_f.at[p], wsem.at[p]))

    @pl.when(v == 0)
    def _():
        for c in weight_copies(exp_ref[0], 0):
            c.start()

    @pl.when(new_ref[v] == 1)
    def _():
        p = wslot_ref[v]
        for c in weight_copies(exp_ref[v], p):
            c.wait()

        @pl.when(next_ref[v] >= 0)
        def _():
            for c in weight_copies(next_ref[v], 1 - p):
                c.start()

        wg_b[...] = wg_f[p].astype(BF16)
        wu_b[...] = wu_f[p].astype(BF16)
        wd_b[...] = wd_f[p].astype(BF16)

    @pl.when(flag == 0)
    def _():
        ys_ref[...] = jnp.zeros((ETILE, D), BF16)

    @pl.when(flag == 1)
    def _():
        k = lax.rem(v, XRING)
        tile_copy(v, k).wait()
        x = xbuf[k]
        hid = jax.nn.silu(_dot(x, wg_b[...])) * _dot(x, wu_b[...])
        ys_ref[...] = _dot(hid.astype(BF16), wd_b[...]).astype(BF16)


def _experts(l, meta, xs, wg, wu, wd):
    nmeta = len(meta)
    grid_spec = pltpu.PrefetchScalarGridSpec(
        num_scalar_prefetch=nmeta,
        grid=(N_ETILES,),
        in_specs=[
            pl.BlockSpec(memory_space=pl.ANY),
            pl.BlockSpec(memory_space=pl.ANY),
            pl.BlockSpec(memory_space=pl.ANY),
            pl.BlockSpec(memory_space=pl.ANY),
        ],
        out_specs=pl.BlockSpec((ETILE, D), lambda v, *_: (v, 0)),
        scratch_shapes=[pltpu.VMEM((XRING, ETILE, D), BF16), pltpu.SemaphoreType.DMA((XRING,)),
                        pltpu.VMEM((2, D, ED), F32), pltpu.VMEM((2, D, ED), F32),
                        pltpu.VMEM((2, ED, D), F32), pltpu.VMEM((D, ED), BF16),
                        pltpu.VMEM((D, ED), BF16), pltpu.VMEM((ED, D), BF16),
                        pltpu.SemaphoreType.DMA((2,))],
    )
    return pl.pallas_call(
        functools.partial(_expert_kernel, l),
        grid_spec=grid_spec,
        out_shape=jax.ShapeDtypeStruct((NSLOT_MAX, D), BF16),
        compiler_params=pltpu.CompilerParams(
            dimension_semantics=("arbitrary",), vmem_limit_bytes=VMEM_LIMIT),
        name="experts",
    )(*meta, xs, wg, wu, wd)


def _combine_tile(G, r, slot, gf, final, out_ref):
    nst = TB // G
    xb = r["xn"][...]
    hid = jax.nn.silu(_dot(xb, r["sg"][...])) * _dot(xb, r["su"][...])
    moe = _dot(hid.astype(BF16), r["sd"][...])
    _chunk_waits(
        r["nrows"][pl.program_id(0)],
        pltpu.make_async_copy(r["ys"].at[pl.ds(0, NCH_MIN * ROWS)],
                              r["loc"].at[slot, pl.ds(0, NCH_MIN * ROWS)], r["sem"].at[slot]),
        pltpu.make_async_copy(r["ys"].at[pl.ds(0, ROWS)], r["loc"].at[slot, pl.ds(0, ROWS)],
                              r["sem"].at[slot]))
    lp = r["lpos"][0].astype(I16)
    wts = r["wts"][0].astype(BF16)

    def unsort_rows(c):
        siota = (lax.broadcasted_iota(I32, (SUB, TB), 0) + c * SUB).astype(I16)
        p = jnp.zeros((SUB, TB), BF16)
        for k in range(TOPK):
            p = jnp.where(siota == lp[k:k + 1, :], wts[k:k + 1, :], p)
        return lax.dot_general(
            p, r["loc"][slot, c * SUB:(c + 1) * SUB, :], (((0,), (0,)), ((), ())),
            preferred_element_type=F32)

    def finish(moe):
        xo = r["x_mid"][...].reshape(nst, G, D) + gf[None] * moe.reshape(nst, G, D)
        if final:
            ms = jnp.mean(xo * xo, axis=-1, keepdims=True)
            xo = xo * lax.rsqrt(ms + EPS) * r["fin_g"][...]
        out_ref[...] = xo.reshape(TB, D)

    for c in range(LAST):
        moe = moe + unsort_rows(c)
    tail_used = r["nrows"][pl.program_id(0)] > LAST * SUB

    @pl.when(tail_used)
    def _():
        finish(moe + unsort_rows(LAST))

    @pl.when(jnp.logical_not(tail_used))
    def _():
        finish(moe)


def _combine_kernel(final, *refs):
    names = ["tab", "tab_next", "nrows", "ys", "x_mid", "xn", "lpos", "wts", "mod", "sg", "su", "sd"]
    names += ["fin_g"] if final else []
    names += ["x_out"] + (["s_out"] if final else []) + ["loc", "sem"]
    r = dict(zip(names, refs))
    b = pl.program_id(0)
    slot = b % 2

    def gather(tab_ref, s):
        r["loc"][s, NCH_MIN * ROWS:, :] = jnp.zeros((LS - NCH_MIN * ROWS, D), BF16)
        _group_copies(tab_ref, lambda l, g, n: pltpu.make_async_copy(
            r["ys"].at[pl.ds(g, n)], r["loc"].at[s, pl.ds(l, n)], r["sem"].at[s]))

    @pl.when(b == 0)
    def _():
        gather(r["tab"], 0)

    @pl.when(b + 1 < NBLK)
    def _():
        gather(r["tab_next"], 1 - slot)

    @pl.when(b < PBLK)
    def _():
        _combine_tile(PB, r, slot, r["mod"][0:PB, 5 * D:6 * D], final, r["x_out"])

    @pl.when(b >= PBLK)
    def _():
        _combine_tile(SB, r, slot, r["mod"][PB:PB + SB, 5 * D:6 * D], final,
                      r["s_out"] if final else r["x_out"])


def _combine(tab, nrows, ys, x_mid, xn, lpos_c, wts, mod_l, w, fin_g):
    final = fin_g is not None
    nb = PB + SB
    in_specs = [
        pl.BlockSpec((1, 1, TABW), lambda b: (b, 0, 0), memory_space=pltpu.SMEM),
        pl.BlockSpec((1, 1, TABW), lambda b: (jnp.minimum(b + 1, NBLK - 1), 0, 0),
                     memory_space=pltpu.SMEM),
        pl.BlockSpec(memory_space=pltpu.SMEM),
        pl.BlockSpec(memory_space=pl.ANY),
        pl.BlockSpec((TB, D), lambda b: (b, 0)),
        pl.BlockSpec((TB, D), lambda b: (b, 0)),
        pl.BlockSpec((1, TOPK, TB), lambda b: (b, 0, 0)),
        pl.BlockSpec((1, TOPK, TB), lambda b: (b, 0, 0)),
        _const_spec((nb, MOD_COLS)),
        _const_spec((D, ED)),
        _const_spec((D, ED)),
        _const_spec((ED, D)),
    ]
    args = [tab, tab, nrows, ys, x_mid, xn, lpos_c, wts, mod_l, w["sg"], w["su"], w["sd"]]
    out_specs = pl.BlockSpec((TB, D), lambda b: (b, 0))
    out_shape = jax.ShapeDtypeStruct((NT, D), F32)
    if final:
        in_specs.append(_const_spec((1, D)))
        args.append(fin_g)
        out_specs = (pl.BlockSpec((TB, D), lambda b: (jnp.minimum(b, PBLK - 1), 0)),
                     pl.BlockSpec((TB, D), lambda b: (jnp.maximum(b - PBLK, 0), 0)))
        out_shape = (jax.ShapeDtypeStruct((NP, D), F32), jax.ShapeDtypeStruct((NS, D), F32))
    return pl.pallas_call(
        functools.partial(_combine_kernel, final),
        grid=(NBLK,),
        in_specs=in_specs,
        out_specs=out_specs,
        out_shape=out_shape,
        scratch_shapes=[pltpu.VMEM((2, LS, D), BF16), pltpu.SemaphoreType.DMA((2,))],
        compiler_params=pltpu.CompilerParams(
            dimension_semantics=("arbitrary",), vmem_limit_bytes=VMEM_LIMIT),
        name="combine_final" if final else "combine",
    )(*args)


def _plan(cnt):
    c = cnt[:, :, :BPT].transpose(0, 2, 1).reshape(NBLK, NE).astype(I32)
    pc = (c + ROWS - 1) // ROWS * ROWS
    lend = jnp.cumsum(pc, axis=1)
    lstart = lend - pc
    etot = jnp.sum(pc, axis=0)
    epad = (etot + ETILE - 1) // ETILE * ETILE
    eend = jnp.cumsum(epad)
    eoff = eend - epad
    gstart = eoff[None, :] + jnp.cumsum(pc, axis=0) - pc
    eids = jnp.arange(NE, dtype=I32)

    def compact(mask, vals, width):
        pos = jnp.cumsum(mask.astype(I32), axis=1) - 1
        sel = mask[:, None, :] & (pos[:, None, :] == jnp.arange(width, dtype=I32)[None, :, None])
        return jnp.sum(jnp.where(sel, vals[:, None, :], 0), axis=-1), jnp.sum(mask.astype(I32), axis=1)

    nchunk = pc // ROWS
    packed = (gstart // ROWS) * (1 << LOCAL_BITS) + lstart // ROWS
    two, n_two = compact(nchunk == 2, packed, NE)
    three, n_three = compact(nchunk == 3, packed, NE)
    lj = jnp.arange(NCH, dtype=I32)
    l8 = lj * ROWS
    ej = jnp.minimum(jnp.sum((lend[:, None, :] <= l8[None, :, None]).astype(I32), axis=-1), NE - 1)
    hot = ej[..., None] == eids
    row = jnp.sum(jnp.where(hot, (gstart - lstart)[:, None, :], 0), axis=-1) + l8[None, :]
    single = (nchunk != 2) & (nchunk != 3)
    is_one = (l8[None, :] < lend[:, -1:]) & jnp.any(hot & single[:, None, :], axis=-1)
    one, n_one = compact(is_one, (row // ROWS) * (1 << LOCAL_BITS) + lj[None, :], NCH)
    counts = jnp.stack([n_two, n_three, n_one] + [jnp.zeros_like(n_two)] * (TABW - TAB_COUNTS - 3), axis=1)
    tab = jnp.concatenate([two, three, one, counts], axis=1).astype(I32).reshape(NBLK, 1, TABW)

    def lookup(table, idx):
        return jnp.sum(jnp.where(idx[..., None] == eids, table, 0), axis=-1)

    used = eend[-1] // ETILE
    v = jnp.arange(N_ETILES, dtype=I32)
    tin = jnp.minimum(v, used - 1)
    e = jnp.minimum(jnp.sum((eend[None, :] <= (tin * ETILE)[:, None]).astype(I32), axis=1), NE - 1)
    flag = (v < used).astype(I32)
    new = jnp.concatenate([jnp.ones((1,), I32), (e[1:] != e[:-1]).astype(I32)])
    live = etot > 0
    wslot = lookup((jnp.cumsum(live.astype(I32)) - 1) % 2, e)
    later = live[None, :] & (eids[None, :] > eids[:, None])
    nxt = jnp.min(jnp.where(later, eids[None, :], NE), axis=1)
    nxt = lookup(jnp.where(nxt < NE, nxt, -1), e)
    meta = (tin, e, flag, new, wslot.astype(I32), nxt.astype(I32))
    fill = jnp.concatenate([used[None], eoff + etot, (epad - etot) // ROWS]).astype(I32)
    return tab, fill, lend[:, -1], meta


def _block_diag(wh):
    out = jnp.zeros((2, 4, HEAD, 4, HEAD), wh.dtype)
    for hh in range(4):
        out = out.at[:, hh, :, hh, :].set(wh.reshape(2, 4, HEAD, HEAD)[:, hh])
    return out.reshape(2, 4 * HEAD, 4 * HEAD)


def _layer_weights(l, p):
    tril = jnp.tril(jnp.ones((CHUNK, CHUNK), bool))
    wt = jnp.where(tril, p["sgu_w"][l], 0.0)
    wpair = wt.reshape(4, 2, CHUNK, CHUNK).transpose(0, 2, 1, 3).reshape(4, CHUNK, 2 * CHUNK)
    bmap = jnp.repeat(p["sgu_b"][l].T, HEAD, axis=1)
    w8 = jnp.where(tril[:DEC_SEQ, :DEC_SEQ], p["sgu_w"][l][:, :DEC_SEQ, :DEC_SEQ], 0.0)
    coef = jnp.repeat(w8.transpose(2, 1, 0), HEAD, axis=2)
    bias8 = jnp.repeat(p["sgu_b"][l][:, :DEC_SEQ].T, HEAD, axis=1)
    gid = jnp.arange(256) // HEAD
    tok = jnp.arange(TILE)
    same_blk = (tok[:, None] // TB) == (tok[None, :] // TB)
    return dict(
        norm_mix_g=p["norm_mix_g"][l][None], norm_ffn_g=p["norm_ffn_g"][l][None],
        w_in=p["w_in"][l].astype(BF16), w_out=p["w_out"][l].astype(BF16),
        conv_w=p["conv_w"][l], conv_b=p["conv_b"][l][None],
        wr=_block_diag(p["gate_r_w"][l]).astype(BF16), wi=_block_diag(p["gate_i_w"][l]).astype(BF16),
        br=p["gate_r_b"][l][None], bi=p["gate_i_b"][l][None], lam=p["lru_lambda"][l][None],
        sgu_g=p["sgu_norm_g"][l][None], wpair=wpair.astype(BF16), bmap=bmap, coef=coef, bias8=bias8,
        out_g=p["out_norm_g"][l][None],
        bd=(gid[:, None] == gid[None, :]).astype(BF16),
        rw_t=p["router_w"][l].T.astype(BF16), rbias=p["router_bias"][l][:, None],
        tri=(same_blk & (tok[:, None] < tok[None, :])).astype(BF16),
        bsel=((tok[:, None] // TB) == jnp.arange(128)[None, :]).astype(BF16),
        ltri=(jnp.arange(NE)[:, None] > jnp.arange(NE)[None, :]).astype(BF16),
        sg=p["shared_w_gate"][l].astype(BF16), su=p["shared_w_up"][l].astype(BF16),
        sd=p["shared_w_down"][l].astype(BF16),
    )


def kernel(x_prompt, x_sample, c_prompt, c_sample, state_lru_h, state_conv, ada_w, ada_b, norm_mix_g, norm_ffn_g, w_in, conv_w, conv_b, gate_r_w, gate_r_b, gate_i_w, gate_i_b, lru_lambda, sgu_norm_g, sgu_w, sgu_b, out_norm_g, w_out, router_w, router_bias, exp_w_gate, exp_w_up, exp_w_down, shared_w_gate, shared_w_up, shared_w_down, final_norm_g):
    p = dict(norm_mix_g=norm_mix_g, norm_ffn_g=norm_ffn_g, w_in=w_in, conv_w=conv_w, conv_b=conv_b,
             gate_r_w=gate_r_w, gate_r_b=gate_r_b, gate_i_w=gate_i_w, gate_i_b=gate_i_b,
             lru_lambda=lru_lambda, sgu_norm_g=sgu_norm_g, sgu_w=sgu_w, sgu_b=sgu_b,
             out_norm_g=out_norm_g, w_out=w_out, router_w=router_w, router_bias=router_bias,
             shared_w_gate=shared_w_gate, shared_w_up=shared_w_up, shared_w_down=shared_w_down)
    x_all = jnp.concatenate([x_prompt.transpose(1, 0, 2).reshape(NP, D),
                             x_sample.transpose(1, 0, 2).reshape(NS, D)], axis=0)
    mod = _modulations(jnp.concatenate([c_prompt, c_sample], axis=0), ada_w, ada_b)

    hp, cp, hs, cs, vs = [], [], [], [], []
    for l in range(DEPTH):
        w = _layer_weights(l, p)
        w["h0s"] = state_lru_h[l]
        w["convs"] = state_conv[l].transpose(1, 0, 2).reshape((CONV_W - 1) * SB, LW)
        yn, hlp, cvp, hls, cvs, v_s = _mixer(x_all, mod[l], w)
        x_mid, xn, lpos_t, w_t, cnt = _router(x_all, yn, mod[l], w)
        tab, used, nrows, meta = _plan(cnt)
        def per_block(a):
            return a.reshape(N_TILES, TOPK, BPT, TB).transpose(0, 2, 1, 3).reshape(NBLK, TOPK, TB)

        lpos_d = per_block(lpos_t)
        xs = _dispatch(tab, used, nrows, lpos_d, xn)
        ys = _experts(l, meta, xs, exp_w_gate, exp_w_up, exp_w_down)
        fin = final_norm_g[None] if l == DEPTH - 1 else None
        x_all = _combine(tab, nrows, ys, x_mid, xn, lpos_d, per_block(w_t), mod[l], w, fin)
        hp.append(hlp)
        cp.append(cvp.reshape(CONV_W - 1, PB, LW).transpose(1, 0, 2))
        hs.append(hls)
        cs.append(cvs.reshape(CONV_W - 1, SB, LW).transpose(1, 0, 2))
        vs.append(v_s.reshape(DEC_SEQ, SB, SW).transpose(1, 0, 2))

    y_prompt = x_all[0].reshape(SEQ, PB, D).transpose(1, 0, 2)
    y_sample = x_all[1].reshape(DEC_SEQ, SB, D).transpose(1, 0, 2)
    return (y_prompt, y_sample, jnp.stack(hp), jnp.stack(cp), jnp.stack(hs), jnp.stack(cs),
            jnp.stack(vs))
```
